```python
import math
import functools
import jax
import jax.numpy as jnp
from jax import lax
import numpy as np

D_MODEL = 2048
BATCH = 8
SEQ = 2048
DEPTH = 2

GRID_W = 64
CTX_LEN = 256
HEAD_DIM = 128
N_HEADS = D_MODEL // HEAD_DIM
NA_HEADS = N_HEADS // 2
NA_WIN_H = 8
NA_WIN_W = 16
NA_KCOLS = 2 * NA_WIN_W
RET_HEADS = N_HEADS // 2
RET_DK = HEAD_DIM
RET_DV = 2 * HEAD_DIM
RET_CHUNK = 128
DIFF_HEADS = N_HEADS // 2
DIFF_DK = HEAD_DIM // 2
DIFF_DV = HEAD_DIM
HG_HEADS = N_HEADS // 2
HG_DK = HEAD_DIM
HG_DV = HEAD_DIM
HG_CHUNK = 64
Q_BLOCK = 128
N_GROUPS = 4
EXPERTS_PER_GROUP = 8
N_EXPERTS = N_GROUPS * EXPERTS_PER_GROUP
TOP_K = 2
EXPERT_FF = D_MODEL // 2
MOE_BLOCK = 128
ROPE_BASE = 10000.0
EPS = 1e-6
N_EVEN = (DEPTH + 1) // 2
N_ODD = DEPTH // 2
F32 = jnp.float32
EVEN_IN_SIZES = (NA_HEADS * HEAD_DIM,) * 3 + (RET_HEADS * RET_DK,) * 2 + (RET_HEADS * RET_DV,) * 2
ODD_IN_SIZES = (DIFF_HEADS * 2 * DIFF_DK,) * 2 + (DIFF_HEADS * DIFF_DV,) + (HG_HEADS * HG_DK,) * 3 + (HG_HEADS * HG_DV,) * 2
EVEN_MIX = NA_HEADS * HEAD_DIM + RET_HEADS * RET_DV
ODD_MIX = DIFF_HEADS * DIFF_DV + HG_HEADS * HG_DV

kernel_name = 'hybrid_natten_retnet_diffattn_hgrn2_hmoe_dit'


def rms_norm(x, w):
    xf = x.astype(F32)
    y = xf * lax.rsqrt(jnp.mean(xf * xf, axis=-1, keepdims=True) + EPS)
    return y.astype(x.dtype) * w


def head_rms(o):
    of = o.astype(F32)
    return (of * lax.rsqrt(jnp.mean(of * of, axis=-1, keepdims=True) + EPS)).astype(o.dtype)


def split_cols(t, sizes):
    cuts = [int(s) for s in np.cumsum(sizes)[:-1]]
    return jnp.split(t, cuts, axis=-1)


def heads(t, n):
    b, s, _ = t.shape
    return t.reshape(b, s, n, -1).transpose(0, 2, 1, 3)


def merge(o):
    b, h, s, d = o.shape
    return o.transpose(0, 2, 1, 3).reshape(b, s, h * d)


def diff_heads(t):
    b, s, _ = t.shape
    return t.reshape(b, s, DIFF_HEADS, 2, DIFF_DK).transpose(0, 2, 3, 1, 4)


def rope_tables(n_tok, dim):
    n_f = dim // 4
    inv = ROPE_BASE ** (-jnp.arange(n_f, dtype=F32) / n_f)
    t = jnp.arange(n_tok)
    pos = jnp.stack([t // GRID_W, t % GRID_W], axis=-1).astype(F32)
    ang = pos[:, :, None] * inv
    return jnp.cos(ang), jnp.sin(ang)


def apply_rope(x, cos, sin):
    n_f = cos.shape[-1]
    x5 = x.reshape(x.shape[:-1] + (2, 2, n_f))
    a, b = x5[..., 0, :], x5[..., 1, :]
    out = jnp.stack([a * cos - b * sin, b * cos + a * sin], axis=-2)
    return out.reshape(x.shape).astype(x.dtype)


def softmax_attention(q, k, v):
    s = jnp.einsum('bhqd,bhkd->bhqk', q * q.shape[-1] ** -0.5, k).astype(F32)
    p = jax.nn.softmax(s, axis=-1).astype(v.dtype)
    return jnp.einsum('bhqk,bhkd->bhqd', p, v)


def neighborhood_attention(q, k, v, k_ctx, v_ctx, rpb):
    b, h, n_tok, d = q.shape
    rows = n_tok // GRID_W
    kh = min(NA_WIN_H, rows)
    kw = NA_WIN_W
    ncb = GRID_W // kw
    qg = (q * d ** -0.5).reshape(b, h, rows, GRID_W, d)
    kg = k.reshape(b, h, rows, GRID_W, d)
    vg = v.reshape(b, h, rows, GRID_W, d)
    r_start = jnp.clip(jnp.arange(rows) - kh // 2, 0, rows - kh)
    cidx = jnp.arange(GRID_W)
    c_start = jnp.clip(cidx - kw // 2, 0, GRID_W - kw)
    qcol = cidx.reshape(ncb, kw)
    blk_start = jnp.minimum(c_start[qcol[:, 0]], GRID_W - NA_KCOLS)
    kcol = blk_start[:, None] + jnp.arange(NA_KCOLS)
    qs = c_start[qcol][:, :, None]
    kc3 = kcol[:, None, :]
    col_mask = (kc3 >= qs) & (kc3 < qs + kw)
    col_off = jnp.clip(kc3 - qcol[:, :, None] + NA_WIN_W - 1, 0, 2 * NA_WIN_W - 2)
    rpb_c = rpb.astype(F32)[:, :, col_off]
    n_loc = kh * NA_KCOLS

    def row(r):
        rs = r_start[r]
        kb = lax.dynamic_slice_in_dim(kg, rs, kh, axis=2)[:, :, :, kcol]
        vb = lax.dynamic_slice_in_dim(vg, rs, kh, axis=2)[:, :, :, kcol]
        qr = lax.dynamic_index_in_dim(qg, r, axis=2, keepdims=False).reshape(b, h, ncb, kw, d)
        row_off = rs + jnp.arange(kh) - r + NA_WIN_H - 1
        bias = jnp.transpose(rpb_c[:, row_off], (0, 2, 3, 1, 4))
        s_loc = jnp.einsum('bhnqd,bhinkd->bhnqik', qr, kb).astype(F32) + bias
        s_loc = jnp.where(col_mask[:, :, None, :], s_loc, -jnp.inf)
        s_ctx = jnp.einsum('bhnqd,bhcd->bhnqc', qr, k_ctx).astype(F32)
        s = jnp.concatenate([s_loc.reshape(b, h, ncb, kw, n_loc), s_ctx], axis=-1)
        p = jax.nn.softmax(s, axis=-1).astype(v.dtype)
        o = jnp.einsum('bhnqik,bhinkd->bhnqd', p[..., :n_loc].reshape(b, h, ncb, kw, kh, NA_KCOLS), vb)
        o = o + jnp.einsum('bhnqc,bhcd->bhnqd', p[..., n_loc:], v_ctx)
        return o.reshape(b, h, GRID_W, d)

    o = lax.map(row, jnp.arange(rows))
    return jnp.transpose(o, (1, 2, 0, 3, 4)).reshape(b, h, n_tok, d)


def retention_scan(q, k, v, s0, log_g):
    b, h, n_tok, _ = q.shape
    dv = v.shape[-1]
    c = RET_CHUNK
    n = n_tok // c
    qf, kf, vf = (t.astype(F32).reshape(b, h, n, c, -1) for t in (q, k, v))
    pos = jnp.arange(c, dtype=F32)
    lg = log_g.astype(F32)[:, None, None]
    rel = pos[:, None] - pos[None, :]
    dmask = jnp.where(rel >= 0, jnp.exp(jnp.maximum(rel, 0.0) * lg), 0.0)
    scores = jnp.einsum('bhnid,bhnjd->bhnij', qf, kf) * dmask[:, None]
    intra = jnp.einsum('bhnij,bhnje->bhnie', scores, vf)
    q_dec = qf * jnp.exp(lg * (pos + 1.0))[..., None]
    k_dec = kf * jnp.exp(lg * (c - 1.0 - pos))[..., None]
    kv = jnp.einsum('bhncd,bhnce->nbhde', k_dec, vf)
    g_chunk = jnp.exp(log_g.astype(F32) * c)[None, :, None, None]

    def step(s, kv_i):
        return s * g_chunk + kv_i, s

    s_fin, s_prev = lax.scan(step, s0, kv)
    inter = jnp.einsum('bhncd,nbhde->bhnce', q_dec, s_prev)
    return (intra + inter).reshape(b, h, n_tok, dv).astype(q.dtype), s_fin


def hgrn_scan(q, k, v, logf, s0):
    b, h, n_tok, _ = q.shape
    dv = v.shape[-1]
    c = HG_CHUNK
    n = n_tok // c
    prep = lambda t: jnp.moveaxis(t.astype(F32).reshape(b, h, n, c, -1), 2, 0)
    tril = jnp.tril(jnp.ones((c, c), dtype=bool))

    def step(s, inp):
        qc, kc, vc, lf = inp
        cum = jnp.cumsum(lf, axis=2)
        inter = jnp.einsum('bhcd,bhde->bhce', qc * jnp.exp(cum), s)
        diff = cum[:, :, :, None, :] - cum[:, :, None, :, :]
        dec = jnp.exp(jnp.where(tril[:, :, None], diff, -jnp.inf))
        att = jnp.einsum('bhid,bhjd,bhijd->bhij', qc, kc, dec)
        intra = jnp.einsum('bhij,bhje->bhie', att, vc)
        last = cum[:, :, -1:, :]
        s_new = jnp.exp(last[:, :, 0])[..., None] * s + jnp.einsum('bhjd,bhje->bhde', kc * jnp.exp(last - cum), vc)
        return s_new, inter + intra

    s_fin, o = lax.scan(step, s0, (prep(q), prep(k), prep(v), prep(logf)))
    return jnp.moveaxis(o, 0, 2).reshape(b, h, n_tok, dv).astype(q.dtype), s_fin


def bidirectional(scan_f, scan_b, lat_f, ctx_f, lat_b, ctx_b, s0, with_ctx):
    flip = lambda ts: tuple(jnp.flip(t, axis=2) for t in ts)
    oc_f, sc_f = scan_f(*ctx_f, s0)
    ol_f, _ = scan_f(*lat_f, sc_f)
    oc_b, sc_b = scan_b(*flip(ctx_b), s0)
    ol_b, _ = scan_b(*flip(lat_b), sc_b)
    lat = ol_f + jnp.flip(ol_b, axis=2)
    ctx = oc_f + jnp.flip(oc_b, axis=2) if with_ctx else None
    return lat, ctx


def diff_attention(q, k, v, lam):
    b, h, _, tq, dk = q.shape
    nb = tq // Q_BLOCK
    qb = jnp.moveaxis(q.reshape(b, h, 2, nb, Q_BLOCK, dk), 3, 0)

    def blk(qi):
        s = jnp.einsum('bhtqd,bhtkd->bhtqk', qi, k).astype(F32)
        p = jax.nn.softmax(s, axis=-1)
        a = (p[:, :, 0] - lam * p[:, :, 1]).astype(v.dtype)
        return jnp.einsum('bhqk,bhkd->bhqd', a, v)

    o = lax.map(blk, qb)
    return jnp.moveaxis(o, 0, 2).reshape(b, h, tq, v.shape[-1])


def even_mixer(h, hc, w_in, w_out, rpb, ret_decay, with_ctx):
    b, n_tok, _ = h.shape
    qa, ka, va, rq, rk, rv, rg = split_cols(h @ w_in, EVEN_IN_SIZES)
    cqa, cka, cva, crq, crk, crv, crg = split_cols(hc @ w_in, EVEN_IN_SIZES)
    ka_c, va_c = heads(cka, NA_HEADS), heads(cva, NA_HEADS)
    a_l = neighborhood_attention(heads(qa, NA_HEADS), heads(ka, NA_HEADS), heads(va, NA_HEADS), ka_c, va_c, rpb)
    cos, sin = rope_tables(n_tok, RET_DK)
    kscale = RET_DK ** -0.5
    lat = (apply_rope(heads(rq, RET_HEADS), cos, sin), apply_rope(heads(rk, RET_HEADS), cos, sin) * kscale, heads(rv, RET_HEADS))
    cx = (heads(crq, RET_HEADS), heads(crk, RET_HEADS) * kscale, heads(crv, RET_HEADS))
    log_g = -jnp.exp(ret_decay.astype(F32))
    s0 = jnp.zeros((b, RET_HEADS, RET_DK, RET_DV), F32)
    fwd = functools.partial(retention_scan, log_g=log_g[0])
    bwd = functools.partial(retention_scan, log_g=log_g[1])
    r_l, r_c = bidirectional(fwd, bwd, lat, cx, lat, cx, s0, with_ctx)
    ret_l = merge(head_rms(r_l)) * jax.nn.silu(rg)
    y = jnp.concatenate([merge(a_l), ret_l], axis=-1) @ w_out
    yc = None
    if with_ctx:
        a_c = softmax_attention(heads(cqa, NA_HEADS), ka_c, va_c)
        ret_c = merge(head_rms(r_c)) * jax.nn.silu(crg)
        yc = jnp.concatenate([merge(a_c), ret_c], axis=-1) @ w_out
    return y, yc


def odd_mixer(h, hc, w_in, w_out, lam_p, lb, layer_idx, with_ctx):
    b, n_tok, _ = h.shape
    dq, dk, dv, hq, hff, hfb, hi, hg = split_cols(h @ w_in, ODD_IN_SIZES)
    cdq, cdk, cdv, chq, chff, chfb, chi, chg = split_cols(hc @ w_in, ODD_IN_SIZES)
    cos, sin = rope_tables(n_tok, DIFF_DK)
    scale = DIFF_DK ** -0.5
    q_l = apply_rope(diff_heads(dq), cos, sin) * scale
    k_l = apply_rope(diff_heads(dk), cos, sin)
    v_l = heads(dv, DIFF_HEADS)
    k_c, v_c = diff_heads(cdk), heads(cdv, DIFF_HEADS)
    lp = lam_p.astype(F32)
    lam_init = 0.8 - 0.6 * math.exp(-0.3 * layer_idx)
    lam = jnp.exp(jnp.sum(lp[0] * lp[1])) - jnp.exp(jnp.sum(lp[2] * lp[3])) + lam_init
    o_l = diff_attention(q_l, jnp.concatenate([k_l, k_c], axis=3), jnp.concatenate([v_l, v_c], axis=2), lam)
    d_l = merge(head_rms(o_l)) * (1.0 - lam_init)

    def gates(fr):
        f = lb + (1.0 - lb) * jax.nn.sigmoid(fr.astype(F32))
        return heads(jnp.log(f), HG_HEADS), heads(1.0 - f, HG_HEADS)

    def hg_inputs(q_raw, ff_raw, fb_raw, i_raw):
        q = heads(jax.nn.silu(q_raw), HG_HEADS)
        v = heads(i_raw, HG_HEADS)
        lf_f, k_f = gates(ff_raw)
        lf_b, k_b = gates(fb_raw)
        return (q, k_f, v, lf_f), (q, k_b, v, lf_b)

    lat_f, lat_b = hg_inputs(hq, hff, hfb, hi)
    ctx_f, ctx_b = hg_inputs(chq, chff, chfb, chi)
    s0 = jnp.zeros((b, HG_HEADS, HG_DK, HG_DV), F32)
    og_l, og_c = bidirectional(hgrn_scan, hgrn_scan, lat_f, ctx_f, lat_b, ctx_b, s0, with_ctx)
    g_l = merge(head_rms(og_l)) * jax.nn.silu(hg)
    y = jnp.concatenate([d_l, g_l], axis=-1) @ w_out
    yc = None
    if with_ctx:
        o_c = diff_attention(diff_heads(cdq) * scale, k_c, v_c, lam)
        d_c = merge(head_rms(o_c)) * (1.0 - lam_init)
        g_c = merge(head_rms(og_c)) * jax.nn.silu(chg)
        yc = jnp.concatenate([d_c, g_c], axis=-1) @ w_out
    return y, yc


def hier_moe(tok, wg, bg, we, be, w1, w3, w2):
    n, d = tok.shape
    lg_g = (tok @ wg).astype(F32) + bg.astype(F32)
    p_g = jax.nn.softmax(lg_g, axis=-1)
    g_top = jnp.argmax(lg_g, axis=-1)
    w_grp = jnp.take_along_axis(p_g, g_top[:, None], axis=1)
    lg_e = ((tok @ we).astype(F32) + be.astype(F32)).reshape(n, N_GROUPS, EXPERTS_PER_GROUP)
    lg_in = jnp.take_along_axis(lg_e, g_top[:, None, None], axis=1)[:, 0]
    top_v, top_i = lax.top_k(lg_in, TOP_K)
    gates = w_grp * jax.nn.softmax(top_v, axis=-1)
    eid = (g_top[:, None] * EXPERTS_PER_GROUP + top_i).reshape(-1)
    n_slots = n * TOP_K
    order = jnp.argsort(eid)
    s_e = eid[order]
    s_t = (jnp.arange(n_slots) // TOP_K)[order]
    s_w = gates.reshape(-1)[order]
    counts = jnp.zeros((N_EXPERTS,), jnp.int32).at[eid].add(1)
    padded = (counts + MOE_BLOCK - 1) // MOE_BLOCK * MOE_BLOCK
    p_end = jnp.cumsum(padded)
    p_start = p_end - padded
    u_start = jnp.cumsum(counts) - counts
    dest = p_start[s_e] + jnp.arange(n_slots) - u_start[s_e]
    n_rows = -(-n_slots // MOE_BLOCK) * MOE_BLOCK + N_EXPERTS * MOE_BLOCK
    n_blocks = n_rows // MOE_BLOCK
    row_tok = jnp.full((n_rows,), n, jnp.int32).at[dest].set(s_t)
    row_w = jnp.zeros((n_rows,), F32).at[dest].set(s_w)
    blk_e = jnp.minimum(jnp.searchsorted(p_end, jnp.arange(n_blocks) * MOE_BLOCK, side='right'), N_EXPERTS - 1)
    xb = jnp.concatenate([tok, jnp.zeros((1, d), tok.dtype)], axis=0)[row_tok].reshape(n_blocks, MOE_BLOCK, d)

    def expert_block(args):
        xe, e = args
        return (jax.nn.silu(xe @ w1[e]) * (xe @ w3[e])) @ w2[e]

    yb = lax.map(expert_block, (xb, blk_e)).reshape(n_rows, d)
    y = jax.ops.segment_sum(yb * row_w[:, None].astype(yb.dtype), row_tok, num_segments=n + 1)
    return y[:n]


def setup_inputs(seed: int = 0) -> dict:
    key = jax.random.key(seed)
    ks = jax.random.split(key, 24)
    d = D_MODEL

    def nrm(k, shape, scale):
        return jax.random.normal(k, shape, F32) * scale

    ret_base = jnp.log(-jnp.log1p(-jnp.exp2(-5.0 - jnp.arange(RET_HEADS, dtype=F32))))
    return {
        'x': nrm(ks[0], (BATCH, SEQ, d), 1.0),
        'c': nrm(ks[1], (BATCH, d), 1.0),
        'ctx': nrm(ks[2], (BATCH, CTX_LEN, d), 1.0),
        'c_ctx': nrm(ks[3], (d,), 1.0),
        'ada_w': nrm(ks[4], (DEPTH, d, 6 * d), 0.5 * d ** -0.5),
        'ada_b': nrm(ks[5], (DEPTH, 6 * d), 0.02),
        'norm1_w': 1.0 + nrm(ks[6], (DEPTH, d), 0.02),
        'norm2_w': 1.0 + nrm(ks[7], (DEPTH, d), 0.02),
        'w_in_even': nrm(ks[8], (N_EVEN, d, sum(EVEN_IN_SIZES)), d ** -0.5),
        'w_out_even': nrm(ks[9], (N_EVEN, EVEN_MIX, d), EVEN_MIX ** -0.5),
        'na_rpb': nrm(ks[10], (N_EVEN, NA_HEADS, 2 * NA_WIN_H - 1, 2 * NA_WIN_W - 1), 0.1),
        'ret_decay': ret_base + nrm(ks[11], (N_EVEN, 2, RET_HEADS), 0.05),
        'w_in_odd': nrm(ks[12], (N_ODD, d, sum(ODD_IN_SIZES)), d ** -0.5),
        'w_out_odd': nrm(ks[13], (N_ODD, ODD_MIX, d), ODD_MIX ** -0.5),
        'diff_lambda': nrm(ks[14], (N_ODD, 4, DIFF_DK), 0.1),
        'hg_lb_logits': nrm(ks[15], (DEPTH, HG_HEADS * HG_DK), 0.5),
        'router_g_w': nrm(ks[16], (DEPTH, d, N_GROUPS), d ** -0.5),
        'router_g_b': nrm(ks[17], (DEPTH, N_GROUPS), 0.01),
        'router_e_w': nrm(ks[18], (DEPTH, d, N_EXPERTS), d ** -0.5),
        'router_e_b': nrm(ks[19], (DEPTH, N_EXPERTS), 0.01),
        'moe_w1': nrm(ks[20], (DEPTH, N_EXPERTS, d, EXPERT_FF), d ** -0.5),
        'moe_w3': nrm(ks[21], (DEPTH, N_EXPERTS, d, EXPERT_FF), d ** -0.5),
        'moe_w2': nrm(ks[22], (DEPTH, N_EXPERTS, EXPERT_FF, d), EXPERT_FF ** -0.5),
        'norm_f_w': 1.0 + nrm(ks[23], (d,), 0.02),
    }


def reference(x, c, ctx, c_ctx, ada_w, ada_b, norm1_w, norm2_w, w_in_even, w_out_even, na_rpb, ret_decay,
              w_in_odd, w_out_odd, diff_lambda, hg_lb_logits, router_g_w, router_g_b, router_e_w, router_e_b,
              moe_w1, moe_w3, moe_w2, norm_f_w):
    b, n_tok, d = x.shape
    n_ctx = ctx.shape[1]
    lb_all = jnp.cumsum(jax.nn.softmax(hg_lb_logits.astype(F32), axis=0), axis=0)
    lb_all = lb_all - lb_all[0]
    sc = jax.nn.silu(c)
    scc = jax.nn.silu(c_ctx)
    for l in range(DEPTH):
        with_ctx = l < DEPTH - 1
        mod = (sc @ ada_w[l] + ada_b[l])[:, None, :]
        mod_c = (scc @ ada_w[l] + ada_b[l])[None, None, :]
        sh1, s1, g1, sh2, s2, g2 = jnp.split(mod, 6, axis=-1)
        csh1, cs1, cg1, csh2, cs2, cg2 = jnp.split(mod_c, 6, axis=-1)
        h = rms_norm(x, norm1_w[l]) * (1.0 + s1) + sh1
        hc = rms_norm(ctx, norm1_w[l]) * (1.0 + cs1) + csh1
        j = l // 2
        if l % 2 == 0:
            y, yc = even_mixer(h, hc, w_in_even[j], w_out_even[j], na_rpb[j], ret_decay[j], with_ctx)
        else:
            y, yc = odd_mixer(h, hc, w_in_odd[j], w_out_odd[j], diff_lambda[j], lb_all[l], l, with_ctx)
        x = x + g1 * y
        h2 = rms_norm(x, norm2_w[l]) * (1.0 + s2) + sh2
        moe_args = (router_g_w[l], router_g_b[l], router_e_w[l], router_e_b[l], moe_w1[l], moe_w3[l], moe_w2[l])
        if with_ctx:
            ctx = ctx + cg1 * yc
            hc2 = rms_norm(ctx, norm2_w[l]) * (1.0 + cs2) + csh2
            tok = jnp.concatenate([h2.reshape(b * n_tok, d), hc2.reshape(b * n_ctx, d)], axis=0)
            out = hier_moe(tok, *moe_args)
            x = x + g2 * out[:b * n_tok].reshape(b, n_tok, d)
            ctx = ctx + cg2 * out[b * n_tok:].reshape(b, n_ctx, d)
        else:
            x = x + g2 * hier_moe(h2.reshape(b * n_tok, d), *moe_args).reshape(b, n_tok, d)
    return rms_norm(x, norm_f_w)
```

```python
import functools
import math

import numpy as np
import jax
import jax.numpy as jnp
from jax import lax
from jax.experimental import pallas as pl
from jax.experimental.pallas import tpu as pltpu

F32 = jnp.float32
BF16 = jnp.bfloat16

D = 2048
BATCH = 8
SEQ = 2048
CTX = 256
TB = SEQ + CTX
M = BATCH * TB
GRID_W = 64
GRID_H = SEQ // GRID_W
HEAD_DIM = 128
NH = 8
NA_WIN_H = 8
NA_WIN_W = 16
RET_DV = 256
RET_CHUNK = 128
DIFF_DK = 64
HG_CHUNK = 64
HG_SUB = 16
N_GROUPS = 4
EXPERTS_PER_GROUP = 8
N_EXPERTS = 32
TOP_K = 2
EXPERT_FF = 1024
ROPE_BASE = 10000.0
EPS = 1e-6
NEG = -1e30

RB = 256
BLOCKS_PER_BATCH = TB // RB
MOE_TM = 256
MOE_ROWS = M * TOP_K + N_EXPERTS * MOE_TM
MOE_BLOCKS = MOE_ROWS // MOE_TM
ROUTER_PAD = 128

V7X_VMEM_BYTES = 64 * 1024 * 1024


def _cparams(sem, vmem_mb):
    return pltpu.CompilerParams(dimension_semantics=sem, vmem_limit_bytes=vmem_mb * 1024 * 1024)


def _silu(x):
    return x * (1.0 / (1.0 + jnp.exp(-x)))


_NT = (((1,), (1,)), ((), ()))
_TN = (((0,), (0,)), ((), ()))


def _dot(a, b):
    return jnp.dot(a, b, preferred_element_type=F32)


def _dot_nt(a, b):
    return lax.dot_general(a, b, _NT, preferred_element_type=F32)


def _dot_tn(a, b):
    return lax.dot_general(a, b, _TN, preferred_element_type=F32)


def _ada_kernel(c_ref, w_ref, b_ref, o_ref):
    a = _silu(c_ref[...]).astype(BF16)
    o_ref[0] = _dot(a, w_ref[0].astype(BF16)) + b_ref[0]


def _ada_mod(c16, ada_w, ada_b):
    depth, _, n = ada_w.shape
    tn = 1024
    return pl.pallas_call(
        _ada_kernel,
        grid=(depth, n // tn),
        in_specs=[
            pl.BlockSpec((16, D), lambda l, j: (0, 0)),
            pl.BlockSpec((1, D, tn), lambda l, j: (l, 0, j)),
            pl.BlockSpec((1, 1, tn), lambda l, j: (l, 0, j)),
        ],
        out_specs=pl.BlockSpec((1, 16, tn), lambda l, j: (l, 0, j)),
        out_shape=jax.ShapeDtypeStruct((depth, 16, n), F32),
        compiler_params=_cparams(("arbitrary", "arbitrary"), 40),
        name="ada_mod",
    )(c16, ada_w, ada_b.reshape(depth, 1, n))


def _mod_index(i):
    return (jnp.where(i % BLOCKS_PER_BATCH == 0, BATCH, i // BLOCKS_PER_BATCH), 0, 0)


def _norm_body(x_ref, mul_ref, add_ref):
    x = x_ref[...]
    y = x * lax.rsqrt(jnp.mean(x * x, axis=-1, keepdims=True) + EPS)
    return y * mul_ref[0] + add_ref[0]


def _norm_kernel(x_ref, mul_ref, add_ref, o_ref):
    o_ref[...] = _norm_body(x_ref, mul_ref, add_ref).astype(BF16)


def _norm_router_kernel(x_ref, mul_ref, add_ref, whi_ref, wlo_ref, rb_ref, o_ref, lg_ref):
    h = _norm_body(x_ref, mul_ref, add_ref)
    hi = h.astype(BF16)
    lo = (h - hi.astype(F32)).astype(BF16)
    o_ref[...] = hi
    lg_ref[...] = (_dot(hi, whi_ref[...]) + _dot(hi, wlo_ref[...]) + _dot(lo, whi_ref[...])) + rb_ref[...]


def _norm_mod(tok, mul, add):
    spec_mod = pl.BlockSpec((1, 1, D), _mod_index)
    return pl.pallas_call(
        _norm_kernel,
        grid=(M // RB,),
        in_specs=[pl.BlockSpec((RB, D), lambda i: (i, 0)), spec_mod, spec_mod],
        out_specs=pl.BlockSpec((RB, D), lambda i: (i, 0)),
        out_shape=jax.ShapeDtypeStruct((M, D), BF16),
        compiler_params=_cparams(("arbitrary",), 32),
        name="norm_mod",
    )(tok, mul, add)


def _norm_router(tok, mul, add, w_hi, w_lo, r_bias):
    spec_mod = pl.BlockSpec((1, 1, D), _mod_index)
    spec_w = pl.BlockSpec((D, ROUTER_PAD), lambda i: (0, 0))
    return pl.pallas_call(
        _norm_router_kernel,
        grid=(M // RB,),
        in_specs=[pl.BlockSpec((RB, D), lambda i: (i, 0)), spec_mod, spec_mod, spec_w, spec_w,
                  pl.BlockSpec((1, ROUTER_PAD), lambda i: (0, 0))],
        out_specs=[pl.BlockSpec((RB, D), lambda i: (i, 0)), pl.BlockSpec((RB, ROUTER_PAD), lambda i: (i, 0))],
        out_shape=[jax.ShapeDtypeStruct((M, D), BF16), jax.ShapeDtypeStruct((M, ROUTER_PAD), F32)],
        compiler_params=_cparams(("arbitrary",), 32),
        name="norm_router",
    )(tok, mul, add, w_hi, w_lo, r_bias)


def _matmul_kernel(a_ref, w_ref, o_ref):
    o_ref[...] = _dot(a_ref[...], w_ref[...]).astype(o_ref.dtype)


def _matmul(a, w, out_dtype=F32):
    m, k = a.shape
    n = w.shape[1]
    tm, tn = TB, 512
    return pl.pallas_call(
        _matmul_kernel,
        grid=(m // tm, n // tn),
        in_specs=[pl.BlockSpec((tm, k), lambda i, j: (i, 0)), pl.BlockSpec((k, tn), lambda i, j: (0, j))],
        out_specs=pl.BlockSpec((tm, tn), lambda i, j: (i, j)),
        out_shape=jax.ShapeDtypeStruct((m, n), out_dtype),
        compiler_params=_cparams(("arbitrary", "arbitrary"), 48),
        name="proj_in",
    )(a, w)


def _out_proj_kernel(a1_ref, a2_ref, w1_ref, w2_ref, res_ref, gl_ref, gc_ref, o_ref, *, tm):
    acc = _dot(a1_ref[...], w1_ref[...]) + _dot(a2_ref[...], w2_ref[...])
    row = pl.program_id(0) * tm + lax.broadcasted_iota(jnp.int32, (tm, 1), 0)
    gate = jnp.where(row % TB < CTX, gc_ref[0], gl_ref[0])
    o_ref[...] = res_ref[...] + gate * acc


def _out_proj(a1, a2, w1, w2, tok, gate):
    tm, tn = TB // 2, 512
    k1, k2 = a1.shape[1], a2.shape[1]
    return pl.pallas_call(
        functools.partial(_out_proj_kernel, tm=tm),
        grid=(M // tm, D // tn),
        in_specs=[
            pl.BlockSpec((tm, k1), lambda i, j: (i, 0)),
            pl.BlockSpec((tm, k2), lambda i, j: (i, 0)),
            pl.BlockSpec((k1, tn), lambda i, j: (0, j)),
            pl.BlockSpec((k2, tn), lambda i, j: (0, j)),
            pl.BlockSpec((tm, tn), lambda i, j: (i, j)),
            pl.BlockSpec((1, 1, tn), lambda i, j: (i // 2, 0, j)),
            pl.BlockSpec((1, 1, tn), lambda i, j: (BATCH, 0, j)),
        ],
        out_specs=pl.BlockSpec((tm, tn), lambda i, j: (i, j)),
        out_shape=jax.ShapeDtypeStruct((M, D), F32),
        compiler_params=_cparams(("arbitrary", "arbitrary"), 48),
        name="proj_out",
    )(a1, a2, w1, w2, tok, gate, gate)


def _rope_tables(dim, reps):
    n_f = dim // 4
    inv = ROPE_BASE ** (-jnp.arange(n_f, dtype=F32) / n_f)
    t = jnp.arange(SEQ)
    pos = jnp.stack([t // GRID_W, t % GRID_W], axis=-1).astype(F32)
    ang = pos[:, :, None] * inv
    cos = jnp.cos(ang)[:, :, None, :]
    sin = jnp.sin(ang)[:, :, None, :]
    cos = jnp.broadcast_to(cos, (SEQ, 2, 2, n_f)).reshape(SEQ, dim)
    sin = jnp.concatenate([-sin, sin], axis=2).reshape(SEQ, dim)
    cos = jnp.tile(cos, (1, reps))
    sin = jnp.tile(sin, (1, reps))
    cos = jnp.concatenate([jnp.ones((CTX, dim * reps), F32), cos], axis=0)
    sin = jnp.concatenate([jnp.zeros((CTX, dim * reps), F32), sin], axis=0)
    return cos, sin


def _rope(x, cos, sin_signed, n_f):
    lane = lax.broadcasted_iota(jnp.int32, x.shape, x.ndim - 1)
    partner = jnp.where(lane % (2 * n_f) < n_f,
                        pltpu.roll(x, HEAD_DIM - n_f, axis=x.ndim - 1),
                        pltpu.roll(x, n_f, axis=x.ndim - 1))
    return x * cos + partner * sin_signed


_NA_GROUP_ROWS = RB // GRID_W
_NA_KEY_ROWS = 3 * _NA_GROUP_ROWS
_NA_PATTERN = (0, 1, 2, 2, 2, 2, 2, 3, 4)


def _na_key_block(g):
    return jnp.clip(g - 1, 0, GRID_H // _NA_GROUP_ROWS - 3)


def _na_bias_tables(rpb):
    tabs = []
    for g in (0, 1, 2, 7):
        r0 = _NA_GROUP_ROWS * g
        ks = int(np.clip(g - 1, 0, GRID_H // _NA_GROUP_ROWS - 3)) * _NA_GROUP_ROWS
        qi = np.arange(RB)
        r = r0 + qi // GRID_W
        c = qi % GRID_W
        ki = np.arange(3 * RB)
        kr = ks + ki // GRID_W
        kc = ki % GRID_W
        rs = np.clip(r - NA_WIN_H // 2, 0, GRID_H - NA_WIN_H)
        cs = np.clip(c - NA_WIN_W // 2, 0, GRID_W - NA_WIN_W)
        valid = ((kr[None, :] >= rs[:, None]) & (kr[None, :] < rs[:, None] + NA_WIN_H)
                 & (kc[None, :] >= cs[:, None]) & (kc[None, :] < cs[:, None] + NA_WIN_W))
        ro = np.clip(kr[None, :] - r[:, None] + NA_WIN_H - 1, 0, 2 * NA_WIN_H - 2)
        co = np.clip(kc[None, :] - c[:, None] + NA_WIN_W - 1, 0, 2 * NA_WIN_W - 2)
        tabs.append(jnp.where(valid[None], rpb.astype(F32)[:, ro, co], NEG))
    tabs.append(jnp.full((NH, RB, 3 * RB), NEG, F32))
    return jnp.stack(tabs, axis=1)


def _na_kernel(q_ref, k0_ref, k1_ref, k2_ref, v0_ref, v1_ref, v2_ref, kc_ref, vc_ref, b_ref, o_ref):
    q = (q_ref[...] * HEAD_DIM ** -0.5).astype(BF16)
    s = [_dot_nt(q, kr[...].astype(BF16)) + b_ref[0, 0, :, t * RB:(t + 1) * RB]
         for t, kr in enumerate((k0_ref, k1_ref, k2_ref))]
    s.append(_dot_nt(q, kc_ref[...].astype(BF16)))
    m = functools.reduce(jnp.maximum, [jnp.max(t, axis=-1, keepdims=True) for t in s])
    p = [jnp.exp(t - m) for t in s]
    l = functools.reduce(jnp.add, [jnp.sum(t, axis=-1, keepdims=True) for t in p])
    vs = (v0_ref, v1_ref, v2_ref, vc_ref)
    o = functools.reduce(jnp.add, [_dot(pt.astype(BF16), vr[...].astype(BF16)) for pt, vr in zip(p, vs)])
    o_ref[...] = (o / l).astype(o_ref.dtype)


def _na_attention(proj, bias):
    def qrow(h, g, b):
        return b * BLOCKS_PER_BATCH + jnp.where(g < 8, 1 + g, 0)

    def krow(t):
        return lambda h, g, b: (b * BLOCKS_PER_BATCH + 1 + _na_key_block(jnp.minimum(g, 7)) + t, NH + h)

    def vrow(t):
        return lambda h, g, b: (b * BLOCKS_PER_BATCH + 1 + _na_key_block(jnp.minimum(g, 7)) + t, 2 * NH + h)

    blk = (RB, HEAD_DIM)
    pattern = jnp.asarray(_NA_PATTERN, jnp.int32)

    def bias_map(h, g, b, pat_ref):
        return (h, pat_ref[g], 0, 0)

    def wrap(f):
        return lambda h, g, b, pat_ref: f(h, g, b)

    in_specs = [pl.BlockSpec(blk, wrap(lambda h, g, b: (qrow(h, g, b), h)))]
    in_specs += [pl.BlockSpec(blk, wrap(krow(t))) for t in range(3)]
    in_specs += [pl.BlockSpec(blk, wrap(vrow(t))) for t in range(3)]
    in_specs += [pl.BlockSpec(blk, wrap(lambda h, g, b: (b * BLOCKS_PER_BATCH, NH + h))),
                 pl.BlockSpec(blk, wrap(lambda h, g, b: (b * BLOCKS_PER_BATCH, 2 * NH + h))),
                 pl.BlockSpec((1, 1, RB, 3 * RB), bias_map)]

    def body(pat_ref, *refs):
        _na_kernel(*refs)

    return pl.pallas_call(
        body,
        grid_spec=pltpu.PrefetchScalarGridSpec(
            num_scalar_prefetch=1,
            grid=(NH, BLOCKS_PER_BATCH, BATCH),
            in_specs=in_specs,
            out_specs=pl.BlockSpec(blk, wrap(lambda h, g, b: (qrow(h, g, b), h))),
        ),
        out_shape=jax.ShapeDtypeStruct((M, NH * HEAD_DIM), BF16),
        compiler_params=_cparams(("arbitrary", "arbitrary", "arbitrary"), 32),
        name="na_attention",
    )(pattern, *([proj] * 9), bias)


_RET_NC = TB // RET_CHUNK
_RET_CTX_NC = CTX // RET_CHUNK


def _ret_kernel(lg_ref, q_ref, k_ref, v_ref, g_ref, cos_ref, sin_ref, o_ref, qr_ref, kr_ref, sf_ref, sb_ref):
    h = pl.program_id(1)
    lgf = lg_ref[0, h]
    lgb = lg_ref[1, h]
    c = RET_CHUNK
    qr_ref[...] = _rope(q_ref[...], cos_ref[...], sin_ref[...], HEAD_DIM // 4)
    kr_ref[...] = _rope(k_ref[...], cos_ref[...], sin_ref[...], HEAD_DIM // 4) * HEAD_DIM ** -0.5

    pos = lax.broadcasted_iota(jnp.int32, (c, 1), 0).astype(F32)
    kdec_f = jnp.exp(lgf * (c - 1.0 - pos))
    kdec_b = jnp.exp(lgb * pos)
    qdec_f = jnp.exp(lgf * (pos + 1.0))
    qdec_b = jnp.exp(lgb * (c - pos))
    gc_f = jnp.exp(lgf * c)
    gc_b = jnp.exp(lgb * c)
    rel = (lax.broadcasted_iota(jnp.int32, (c, c), 0) - lax.broadcasted_iota(jnp.int32, (c, c), 1)).astype(F32)
    dmask = (jnp.where(rel >= 0, jnp.exp(jnp.maximum(rel, 0.0) * lgf), 0.0)
             + jnp.where(rel <= 0, jnp.exp(jnp.maximum(-rel, 0.0) * lgb), 0.0))

    def state_step(n, s, s_ref, kdec, gc):
        rows = pl.ds(pl.multiple_of(n * c, c), c)
        s_ref[n] = s
        kn = (kr_ref[rows, :] * kdec).astype(BF16)
        return s * gc + _dot_tn(kn, v_ref[rows, :].astype(BF16))

    zero = jnp.zeros((HEAD_DIM, RET_DV), F32)
    lax.fori_loop(0, _RET_NC, lambda n, s: state_step(n, s, sf_ref, kdec_f, gc_f), zero)
    s = lax.fori_loop(0, _RET_CTX_NC,
                      lambda i, s: state_step(_RET_CTX_NC - 1 - i, s, sb_ref, kdec_b, gc_b), zero)
    lax.fori_loop(0, _RET_NC - _RET_CTX_NC,
                  lambda i, s: state_step(_RET_NC - 1 - i, s, sb_ref, kdec_b, gc_b), s)

    def out_step(n, carry):
        rows = pl.ds(pl.multiple_of(n * c, c), c)
        qn = qr_ref[rows, :]
        vn = v_ref[rows, :].astype(BF16)
        scores = _dot_nt(qn.astype(BF16), kr_ref[rows, :].astype(BF16)) * dmask
        intra = _dot(scores.astype(BF16), vn)
        qcat = jnp.concatenate([qn * qdec_f, qn * qdec_b], axis=1).astype(BF16)
        scat = jnp.concatenate([sf_ref[n], sb_ref[n]], axis=0).astype(BF16)
        o = intra + _dot(qcat, scat)
        o = o * lax.rsqrt(jnp.mean(o * o, axis=-1, keepdims=True) + EPS)
        o_ref[rows, :] = (o * _silu(g_ref[rows, :])).astype(o_ref.dtype)
        return carry

    lax.fori_loop(0, _RET_NC, out_step, 0)


def _retention(proj, log_g, cos, sin):
    seq128 = lambda blk0: pl.BlockSpec((TB, HEAD_DIM), lambda b, h, lg: (b, blk0 + h))
    seq256 = lambda blk0: pl.BlockSpec((TB, RET_DV), lambda b, h, lg: (b, blk0 + h))
    tab = pl.BlockSpec((TB, HEAD_DIM), lambda b, h, lg: (0, 0))
    return pl.pallas_call(
        _ret_kernel,
        grid_spec=pltpu.PrefetchScalarGridSpec(
            num_scalar_prefetch=1,
            grid=(BATCH, NH),
            in_specs=[seq128(3072 // HEAD_DIM), seq128(4096 // HEAD_DIM),
                      seq256(5120 // RET_DV), seq256(7168 // RET_DV), tab, tab],
            out_specs=pl.BlockSpec((TB, RET_DV), lambda b, h, lg: (b, h)),
            scratch_shapes=[pltpu.VMEM((TB, HEAD_DIM), F32), pltpu.VMEM((TB, HEAD_DIM), F32),
                            pltpu.VMEM((_RET_NC, HEAD_DIM, RET_DV), F32),
                            pltpu.VMEM((_RET_NC, HEAD_DIM, RET_DV), F32)],
        ),
        out_shape=jax.ShapeDtypeStruct((M, NH * RET_DV), BF16),
        compiler_params=_cparams(("arbitrary", "arbitrary"), 48),
        name="retention",
    )(log_g, proj, proj, proj, proj, cos, sin)


def _diff_kernel(lam_ref, q_ref, k_ref, v_ref, cq_ref, sq_ref, ck_ref, sk_ref, o_ref, kb_ref, vb_ref, *, post_scale):
    qi = pl.program_id(2)

    @pl.when(qi == 0)
    def _():
        kb_ref[...] = _rope(k_ref[...], ck_ref[...], sk_ref[...], DIFF_DK // 4).astype(BF16)
        vb_ref[...] = v_ref[...].astype(BF16)
        o_ref[...] = jnp.zeros_like(o_ref)

    @pl.when(qi > 0)
    def _():
        lam = lam_ref[0]
        q = _rope(q_ref[...], cq_ref[...], sq_ref[...], DIFF_DK // 4) * DIFF_DK ** -0.5
        lane = lax.broadcasted_iota(jnp.int32, q.shape, 1)
        kb = kb_ref[...]
        p, inv = [], []
        for t in range(2):
            qt = jnp.where((lane >= DIFF_DK) == (t == 1), q, 0.0).astype(BF16)
            s = _dot_nt(qt, kb)
            e = jnp.exp(s - jnp.max(s, axis=-1, keepdims=True))
            p.append(e)
            inv.append(1.0 / jnp.sum(e, axis=-1, keepdims=True))
        a = (p[0] * inv[0] - p[1] * (lam * inv[1])).astype(BF16)
        o = _dot(a, vb_ref[...])
        o = o * lax.rsqrt(jnp.mean(o * o, axis=-1, keepdims=True) + EPS)
        o_ref[...] = (o * post_scale).astype(o_ref.dtype)


def _diff_attention(proj, lam, cos, sin, post_scale):
    blk = (RB, HEAD_DIM)
    seq = (TB, HEAD_DIM)
    qmap = lambda b, h, qi, lam_ref: (b * BLOCKS_PER_BATCH + qi, h)
    return pl.pallas_call(
        functools.partial(_diff_kernel, post_scale=post_scale),
        grid_spec=pltpu.PrefetchScalarGridSpec(
            num_scalar_prefetch=1,
            grid=(BATCH, NH, BLOCKS_PER_BATCH),
            in_specs=[pl.BlockSpec(blk, qmap),
                      pl.BlockSpec(seq, lambda b, h, qi, lam_ref: (b, NH + h)),
                      pl.BlockSpec(seq, lambda b, h, qi, lam_ref: (b, 2 * NH + h)),
                      pl.BlockSpec(blk, lambda b, h, qi, lam_ref: (qi, 0)),
                      pl.BlockSpec(blk, lambda b, h, qi, lam_ref: (qi, 0)),
                      pl.BlockSpec(seq, lambda b, h, qi, lam_ref: (0, 0)),
                      pl.BlockSpec(seq, lambda b, h, qi, lam_ref: (0, 0))],
            out_specs=pl.BlockSpec(blk, qmap),
            scratch_shapes=[pltpu.VMEM(seq, BF16), pltpu.VMEM(seq, BF16)],
        ),
        out_shape=jax.ShapeDtypeStruct((M, NH * HEAD_DIM), BF16),
        compiler_params=_cparams(("arbitrary", "arbitrary", "arbitrary"), 48),
        name="diff_attention",
    )(lam, proj, proj, proj, cos, sin, cos, sin)


_HG_NC = TB // HG_CHUNK
_HG_CTX_NC = CTX // HG_CHUNK


def _chunk_cumsum(x, reverse):
    c = x.shape[0]
    row = lax.broadcasted_iota(jnp.int32, x.shape, 0)
    s = 1
    while s < c:
        if reverse:
            x = x + jnp.where(row < c - s, pltpu.roll(x, c - s, axis=0), 0.0)
        else:
            x = x + jnp.where(row >= s, pltpu.roll(x, s, axis=0), 0.0)
        s *= 2
    return x


def _hg_kernel(q_ref, ff_ref, fb_ref, v_ref, g_ref, lb_ref, o_ref,
               qs_ref, kf_ref, kb_ref, cf_ref, cb_ref, sf_ref, sb_ref, od_ref):
    c = HG_CHUNK
    lb = lb_ref[0]
    qs_ref[...] = _silu(q_ref[...])

    def gate(raw_ref, k_ref, lf_ref):
        f = lb + (1.0 - lb) * (1.0 / (1.0 + jnp.exp(-raw_ref[...])))
        k_ref[...] = 1.0 - f
        lf_ref[...] = jnp.log(f)

    gate(ff_ref, kf_ref, cf_ref)
    gate(fb_ref, kb_ref, cb_ref)

    def state_step(n, st, s_ref, k_ref, cum_ref, reverse):
        rows = pl.ds(pl.multiple_of(n * c, c), c)
        s_ref[n] = st
        cum = _chunk_cumsum(cum_ref[rows, :], reverse)
        cum_ref[rows, :] = cum
        last = cum[0:1, :] if reverse else cum[c - 1:c, :]
        kt = (k_ref[rows, :] * jnp.exp(last - cum)).astype(BF16)
        return st * jnp.exp(last) + _dot_tn(v_ref[rows, :].astype(BF16), kt)

    zero = jnp.zeros((HEAD_DIM, HEAD_DIM), F32)
    lax.fori_loop(0, _HG_NC, lambda n, s: state_step(n, s, sf_ref, kf_ref, cf_ref, False), zero)
    s = lax.fori_loop(0, _HG_CTX_NC,
                      lambda i, s: state_step(_HG_CTX_NC - 1 - i, s, sb_ref, kb_ref, cb_ref, True), zero)
    lax.fori_loop(0, _HG_NC - _HG_CTX_NC,
                  lambda i, s: state_step(_HG_NC - 1 - i, s, sb_ref, kb_ref, cb_ref, True), s)

    o_ref[0:CTX, :] = jnp.zeros((CTX, HEAD_DIM), o_ref.dtype)

    row = lax.broadcasted_iota(jnp.int32, (c, 1), 0)
    ri = lax.broadcasted_iota(jnp.int32, (c, c), 0)
    ci = lax.broadcasted_iota(jnp.int32, (c, c), 1)
    sub_row = lax.broadcasted_iota(jnp.int32, (HG_SUB, 1), 0)

    def level(q, kf, kb, cf, cb, half):
        grp = row // (2 * half)
        upper = (row % (2 * half)) >= half
        ref_f = jnp.zeros_like(cf)
        ref_b = jnp.zeros_like(cb)
        for gi in range(c // (2 * half)):
            base = gi * 2 * half
            ref_f = jnp.where(grp == gi, cf[base + half - 1:base + half, :], ref_f)
            ref_b = jnp.where(grp == gi, cb[base + half:base + half + 1, :], ref_b)
        zf = jnp.minimum(jnp.where(upper, cf - ref_f, ref_f - cf), 0.0)
        zb = jnp.minimum(jnp.where(upper, ref_b - cb, cb - ref_b), 0.0)
        ef = jnp.exp(zf)
        eb = jnp.exp(zb)
        qq = jnp.concatenate([jnp.where(upper, q * ef, 0.0), jnp.where(upper, 0.0, q * eb)], axis=1)
        kk = jnp.concatenate([jnp.where(upper, 0.0, kf * ef), jnp.where(upper, kb * eb, 0.0)], axis=1)
        a = _dot_nt(qq.astype(BF16), kk.astype(BF16))
        return jnp.where(ri // (2 * half) == ci // (2 * half), a, 0.0)

    def out_step(n, carry):
        rows = pl.ds(pl.multiple_of(n * c, c), c)
        q = qs_ref[rows, :]
        kf = kf_ref[rows, :]
        kb = kb_ref[rows, :]
        cf = cf_ref[rows, :]
        cb = cb_ref[rows, :]
        v = v_ref[rows, :]
        qcat = jnp.concatenate([q * jnp.exp(cf), q * jnp.exp(cb)], axis=1).astype(BF16)
        scat = jnp.concatenate([sf_ref[n], sb_ref[n]], axis=1).astype(BF16)
        o = _dot_nt(qcat, scat)
        att = None
        half = c // 2
        while half >= HG_SUB:
            a = level(q, kf, kb, cf, cb, half)
            att = a if att is None else att + a
            half //= 2
        o = o + _dot(att.astype(BF16), v.astype(BF16))
        for blk in range(c // HG_SUB):
            lo = blk * HG_SUB
            qb, kfb, kbb = q[lo:lo + HG_SUB], kf[lo:lo + HG_SUB], kb[lo:lo + HG_SUB]
            cfb, cbb, vb = cf[lo:lo + HG_SUB], cb[lo:lo + HG_SUB], v[lo:lo + HG_SUB]
            for i in range(HG_SUB):
                ef = jnp.where(sub_row <= i, jnp.exp(jnp.minimum(cfb[i:i + 1] - cfb, 0.0)), 0.0)
                eb = jnp.where(sub_row >= i, jnp.exp(jnp.minimum(cbb[i:i + 1] - cbb, 0.0)), 0.0)
                w = jnp.sum(qb[i:i + 1] * (ef * kfb + eb * kbb), axis=-1, keepdims=True)
                od_ref[lo + i:lo + i + 1, :] = jnp.sum(w * vb, axis=0, keepdims=True)
        o = o + od_ref[...]
        o = o * lax.rsqrt(jnp.mean(o * o, axis=-1, keepdims=True) + EPS)
        o_ref[rows, :] = (o * _silu(g_ref[rows, :])).astype(o_ref.dtype)
        return carry

    lax.fori_loop(_HG_CTX_NC, _HG_NC, out_step, 0)


def _hgrn(proj, lb):
    seq = lambda col0: pl.BlockSpec((TB, HEAD_DIM), lambda b, h: (b, col0 // HEAD_DIM + h))
    full = pltpu.VMEM((TB, HEAD_DIM), F32)
    states = pltpu.VMEM((_HG_NC, HEAD_DIM, HEAD_DIM), F32)
    return pl.pallas_call(
        _hg_kernel,
        grid=(BATCH, NH),
        in_specs=[seq(3072), seq(4096), seq(5120), seq(6144), seq(7168),
                  pl.BlockSpec((1, 1, HEAD_DIM), lambda b, h: (h, 0, 0))],
        out_specs=pl.BlockSpec((TB, HEAD_DIM), lambda b, h: (b, h)),
        out_shape=jax.ShapeDtypeStruct((M, NH * HEAD_DIM), BF16),
        scratch_shapes=[full, full, full, full, full, states, states, pltpu.VMEM((HG_CHUNK, HEAD_DIM), F32)],
        compiler_params=_cparams(("arbitrary", "arbitrary"), 48),
        name="hgrn",
    )(proj, proj, proj, proj, proj, lb.reshape(NH, 1, HEAD_DIM))


def _moe_up_kernel(be_ref, nu_ref, x_ref, w1_ref, w3_ref, o_ref, w1b_ref, w3b_ref):
    i = pl.program_id(0)
    changed = jnp.logical_or(i == 0, be_ref[i] != be_ref[jnp.maximum(i - 1, 0)])

    @pl.when(changed)
    def _():
        w1b_ref[...] = w1_ref[0].astype(BF16)
        w3b_ref[...] = w3_ref[0].astype(BF16)

    @pl.when(i < nu_ref[0])
    def _():
        x = x_ref[...]
        a = _dot(x, w1b_ref[...])
        o_ref[...] = (_silu(a) * _dot(x, w3b_ref[...])).astype(o_ref.dtype)

    @pl.when(i >= nu_ref[0])
    def _():
        o_ref[...] = jnp.zeros_like(o_ref)


def _moe_up(blk_e, n_used, xg, w1, w3):
    wspec = pl.BlockSpec((1, D, EXPERT_FF), lambda i, be, nu: (be[i], 0, 0))
    return pl.pallas_call(
        _moe_up_kernel,
        grid_spec=pltpu.PrefetchScalarGridSpec(
            num_scalar_prefetch=2,
            grid=(MOE_BLOCKS,),
            in_specs=[pl.BlockSpec((MOE_TM, D), lambda i, be, nu: (i, 0)), wspec, wspec],
            out_specs=pl.BlockSpec((MOE_TM, EXPERT_FF), lambda i, be, nu: (i, 0)),
            scratch_shapes=[pltpu.VMEM((D, EXPERT_FF), BF16), pltpu.VMEM((D, EXPERT_FF), BF16)],
        ),
        out_shape=jax.ShapeDtypeStruct((MOE_ROWS, EXPERT_FF), BF16),
        compiler_params=_cparams(("arbitrary",), 52),
        name="moe_up",
    )(blk_e, n_used, xg, w1, w3)


def _moe_down_kernel(be_ref, nu_ref, h_ref, w2_ref, o_ref, w2b_ref):
    i = pl.program_id(0)
    changed = jnp.logical_or(i == 0, be_ref[i] != be_ref[jnp.maximum(i - 1, 0)])

    @pl.when(changed)
    def _():
        w2b_ref[...] = w2_ref[0].astype(BF16)

    @pl.when(i < nu_ref[0])
    def _():
        o_ref[...] = _dot(h_ref[...], w2b_ref[...])

    @pl.when(i >= nu_ref[0])
    def _():
        o_ref[...] = jnp.zeros_like(o_ref)


def _moe_down(blk_e, n_used, hmid, w2):
    return pl.pallas_call(
        _moe_down_kernel,
        grid_spec=pltpu.PrefetchScalarGridSpec(
            num_scalar_prefetch=2,
            grid=(MOE_BLOCKS,),
            in_specs=[pl.BlockSpec((MOE_TM, EXPERT_FF), lambda i, be, nu: (i, 0)),
                      pl.BlockSpec((1, EXPERT_FF, D), lambda i, be, nu: (be[i], 0, 0))],
            out_specs=pl.BlockSpec((MOE_TM, D), lambda i, be, nu: (i, 0)),
            scratch_shapes=[pltpu.VMEM((EXPERT_FF, D), BF16)],
        ),
        out_shape=jax.ShapeDtypeStruct((MOE_ROWS, D), F32),
        compiler_params=_cparams(("arbitrary",), 40),
        name="moe_down",
    )(blk_e, n_used, hmid, w2)


def _combine_kernel(x_ref, y0_ref, y1_ref, w_ref, g_ref, o_ref):
    w = w_ref[...]
    y = w[:, 0:1] * y0_ref[...] + w[:, 1:2] * y1_ref[...]
    o_ref[...] = x_ref[...] + g_ref[0] * y


def _moe_combine(tok, y0, y1, gates, gate_mod):
    row = pl.BlockSpec((RB, D), lambda i: (i, 0))
    return pl.pallas_call(
        _combine_kernel,
        grid=(M // RB,),
        in_specs=[row, row, row, pl.BlockSpec((RB, TOP_K), lambda i: (i, 0)), pl.BlockSpec((1, 1, D), _mod_index)],
        out_specs=row,
        out_shape=jax.ShapeDtypeStruct((M, D), F32),
        compiler_params=_cparams(("arbitrary",), 32),
        name="moe_combine",
    )(tok, y0, y1, gates, gate_mod)


def _route(logits, valid):
    lg_g = logits[:, :N_GROUPS]
    p_g = jax.nn.softmax(lg_g, axis=-1)
    g_top = jnp.argmax(lg_g, axis=-1)
    w_grp = jnp.take_along_axis(p_g, g_top[:, None], axis=1)
    lg_e = logits[:, N_GROUPS:N_GROUPS + N_EXPERTS].reshape(M, N_GROUPS, EXPERTS_PER_GROUP)
    lg_in = jnp.take_along_axis(lg_e, g_top[:, None, None], axis=1)[:, 0]
    top_v, top_i = lax.top_k(lg_in, TOP_K)
    gates = w_grp * jax.nn.softmax(top_v, axis=-1)
    gates = jnp.where(valid[:, None], gates, 0.0)
    eid = g_top[:, None] * EXPERTS_PER_GROUP + top_i
    eid = jnp.where(valid[:, None], eid, N_EXPERTS).reshape(-1).astype(jnp.int32)
    n_slots = M * TOP_K
    order = jnp.argsort(eid)
    s_e = eid[order]
    s_t = (jnp.arange(n_slots, dtype=jnp.int32) // TOP_K)[order]
    counts = jnp.zeros((N_EXPERTS + 1,), jnp.int32).at[eid].add(1)
    padded = (counts + MOE_TM - 1) // MOE_TM * MOE_TM
    p_end = jnp.cumsum(padded)
    p_start = p_end - padded
    u_start = jnp.cumsum(counts) - counts
    dest = p_start[s_e] + jnp.arange(n_slots, dtype=jnp.int32) - u_start[s_e]
    dest = jnp.where(s_e < N_EXPERTS, dest, MOE_ROWS)
    row_tok = jnp.zeros((MOE_ROWS,), jnp.int32).at[dest].set(s_t, mode="drop")
    pos = jnp.zeros((n_slots,), jnp.int32).at[order].set(jnp.minimum(dest, MOE_ROWS - 1)).reshape(M, TOP_K)
    pos = jnp.where(valid[:, None], pos, 0)
    real_end = p_end[N_EXPERTS - 1]
    blk_e = jnp.searchsorted(p_end[:N_EXPERTS], jnp.arange(MOE_BLOCKS, dtype=jnp.int32) * MOE_TM, side="right")
    blk_e = jnp.minimum(blk_e, N_EXPERTS - 1).astype(jnp.int32)
    n_used = (real_end // MOE_TM).astype(jnp.int32).reshape(1)
    return gates, row_tok, pos, blk_e, n_used


def _moe(tok, mul, add, gate_mod, wg, bg, we, be, w1, w3, w2, valid):
    wr = jnp.zeros((D, ROUTER_PAD), F32).at[:, :N_GROUPS].set(wg).at[:, N_GROUPS:N_GROUPS + N_EXPERTS].set(we)
    w_hi = wr.astype(BF16)
    w_lo = (wr - w_hi.astype(F32)).astype(BF16)
    r_bias = jnp.zeros((1, ROUTER_PAD), F32).at[0, :N_GROUPS].set(bg).at[0, N_GROUPS:N_GROUPS + N_EXPERTS].set(be)
    h2, logits = _norm_router(tok, mul, add, w_hi, w_lo, r_bias)
    gates, row_tok, pos, blk_e, n_used = _route(logits, valid)
    xg = jnp.take(h2, row_tok, axis=0)
    hmid = _moe_up(blk_e, n_used, xg, w1, w3)
    y = _moe_down(blk_e, n_used, hmid, w2)
    y0 = jnp.take(y, pos[:, 0], axis=0)
    y1 = jnp.take(y, pos[:, 1], axis=0)
    return _moe_combine(tok, y0, y1, gates, gate_mod)


def _final_kernel(x_ref, w_ref, o_ref):
    x = x_ref[...]
    o_ref[0] = x * lax.rsqrt(jnp.mean(x * x, axis=-1, keepdims=True) + EPS) * w_ref[...]


def _final_norm(tok, w):
    nb = SEQ // RB
    return pl.pallas_call(
        _final_kernel,
        grid=(BATCH, nb),
        in_specs=[pl.BlockSpec((RB, D), lambda b, g: (b * BLOCKS_PER_BATCH + 1 + g, 0)),
                  pl.BlockSpec((1, D), lambda b, g: (0, 0))],
        out_specs=pl.BlockSpec((1, RB, D), lambda b, g: (b, g, 0)),
        out_shape=jax.ShapeDtypeStruct((BATCH, SEQ, D), F32),
        compiler_params=_cparams(("arbitrary", "arbitrary"), 32),
        name="final_norm",
    )(tok, w.reshape(1, D))


def _mod9(t):
    return t[:BATCH + 1].reshape(BATCH + 1, 1, D)


def kernel(x, c, ctx, c_ctx, ada_w, ada_b, norm1_w, norm2_w, w_in_even, w_out_even, na_rpb, ret_decay,
           w_in_odd, w_out_odd, diff_lambda, hg_lb_logits, router_g_w, router_g_b, router_e_w, router_e_b,
           moe_w1, moe_w3, moe_w2, norm_f_w):
    depth = ada_w.shape[0]
    tok = jnp.concatenate([ctx, x], axis=1).reshape(M, D)
    c16 = jnp.concatenate([c, c_ctx[None], jnp.zeros((16 - BATCH - 1, D), F32)], axis=0)
    mod = _ada_mod(c16, ada_w, ada_b)
    lb_all = jnp.cumsum(jax.nn.softmax(hg_lb_logits.astype(F32), axis=0), axis=0)
    lb_all = lb_all - lb_all[0]
    is_latent = (jnp.arange(M) % TB) >= CTX
    all_rows = jnp.ones((M,), bool)

    for l in range(depth):
        with_ctx = l < depth - 1
        sh1, s1, g1, sh2, s2, g2 = [_mod9(t) for t in jnp.split(mod[l], 6, axis=-1)]
        h = _norm_mod(tok, norm1_w[l] * (1.0 + s1), sh1)
        j = l // 2
        if l % 2 == 0:
            proj = _matmul(h, w_in_even[j].astype(BF16))
            a = _na_attention(proj, _na_bias_tables(na_rpb[j]))
            cos, sin = _rope_tables(HEAD_DIM, 1)
            r = _retention(proj, -jnp.exp(ret_decay[j].astype(F32)), cos, sin)
            w_out = w_out_even[j].astype(BF16)
            tok = _out_proj(a, r, w_out[:NH * HEAD_DIM], w_out[NH * HEAD_DIM:], tok, g1)
        else:
            proj = _matmul(h, w_in_odd[j].astype(BF16))
            lp = diff_lambda[j].astype(F32)
            lam_init = 0.8 - 0.6 * math.exp(-0.3 * l)
            lam = jnp.exp(jnp.sum(lp[0] * lp[1])) - jnp.exp(jnp.sum(lp[2] * lp[3])) + lam_init
            cos, sin = _rope_tables(DIFF_DK, 2)
            d_l = _diff_attention(proj, lam.reshape(1), cos, sin, 1.0 - lam_init)
            g_l = _hgrn(proj, lb_all[l])
            w_out = w_out_odd[j].astype(BF16)
            tok = _out_proj(d_l, g_l, w_out[:NH * HEAD_DIM], w_out[NH * HEAD_DIM:], tok, g1)
        tok = _moe(tok, norm2_w[l] * (1.0 + s2), sh2, g2,
                   router_g_w[l], router_g_b[l], router_e_w[l], router_e_b[l],
                   moe_w1[l], moe_w3[l], moe_w2[l], all_rows if with_ctx else is_latent)
    return _final_norm(tok, norm_f_w)
```

```python
import functools
import math

import numpy as np
import jax
import jax.numpy as jnp
from jax import lax
from jax.experimental import pallas as pl
from jax.experimental.pallas import tpu as pltpu

F32 = jnp.float32
BF16 = jnp.bfloat16

D = 2048
BATCH = 8
SEQ = 2048
CTX = 256
TB = SEQ + CTX
M = BATCH * TB
GRID_W = 64
GRID_H = SEQ // GRID_W
HEAD_DIM = 128
NH = 8
NA_WIN_H = 8
NA_WIN_W = 16
RET_DV = 256
RET_CHUNK = 128
DIFF_DK = 64
HG_CHUNK = 64
HG_SUB = 16
N_GROUPS = 4
EXPERTS_PER_GROUP = 8
N_EXPERTS = 32
TOP_K = 2
EXPERT_FF = 1024
ROPE_BASE = 10000.0
EPS = 1e-6
NEG = -1e30

RB = 256
BLOCKS_PER_BATCH = TB // RB
MOE_TM = 256
MOE_ROWS = M * TOP_K + N_EXPERTS * MOE_TM
MOE_BLOCKS = MOE_ROWS // MOE_TM
ROUTER_PAD = 128
GATHER_UNROLL = 8


def _cparams(sem, vmem_mb):
    return pltpu.CompilerParams(dimension_semantics=sem, vmem_limit_bytes=vmem_mb * 1024 * 1024)


def _silu(x):
    return x * (1.0 / (1.0 + jnp.exp(-x)))


_NT = (((1,), (1,)), ((), ()))
_TN = (((0,), (0,)), ((), ()))


def _dot(a, b):
    return jnp.dot(a, b, preferred_element_type=F32)


def _dot_nt(a, b):
    return lax.dot_general(a, b, _NT, preferred_element_type=F32)


def _dot_tn(a, b):
    return lax.dot_general(a, b, _TN, preferred_element_type=F32)


def _ada_kernel(c_ref, w_ref, b_ref, o_ref):
    a = _silu(c_ref[...]).astype(BF16)
    o_ref[0] = _dot(a, w_ref[0].astype(BF16)) + b_ref[0]


def _ada_mod(c16, ada_w, ada_b):
    depth, _, n = ada_w.shape
    tn = 1024
    return pl.pallas_call(
        _ada_kernel,
        grid=(depth, n // tn),
        in_specs=[
            pl.BlockSpec((16, D), lambda l, j: (0, 0)),
            pl.BlockSpec((1, D, tn), lambda l, j: (l, 0, j)),
            pl.BlockSpec((1, 1, tn), lambda l, j: (l, 0, j)),
        ],
        out_specs=pl.BlockSpec((1, 16, tn), lambda l, j: (l, 0, j)),
        out_shape=jax.ShapeDtypeStruct((depth, 16, n), F32),
        compiler_params=_cparams(("arbitrary", "arbitrary"), 40),
        name="ada_mod",
    )(c16, ada_w, ada_b.reshape(depth, 1, n))


def _mod_index(i):
    return (jnp.where(i % BLOCKS_PER_BATCH == 0, BATCH, i // BLOCKS_PER_BATCH), 0, 0)


def _norm_body(x_ref, mul_ref, add_ref):
    x = x_ref[...]
    y = x * lax.rsqrt(jnp.mean(x * x, axis=-1, keepdims=True) + EPS)
    return y * mul_ref[0] + add_ref[0]


def _norm_kernel(x_ref, mul_ref, add_ref, o_ref):
    o_ref[...] = _norm_body(x_ref, mul_ref, add_ref).astype(BF16)


def _norm_router_kernel(x_ref, mul_ref, add_ref, whi_ref, wlo_ref, rb_ref, o_ref, lg_ref):
    h = _norm_body(x_ref, mul_ref, add_ref)
    hi = h.astype(BF16)
    lo = (h - hi.astype(F32)).astype(BF16)
    o_ref[...] = h
    lg_ref[...] = (_dot(hi, whi_ref[...]) + _dot(hi, wlo_ref[...]) + _dot(lo, whi_ref[...])) + rb_ref[...]


def _norm_mod(tok, mul, add):
    spec_mod = pl.BlockSpec((1, 1, D), _mod_index)
    return pl.pallas_call(
        _norm_kernel,
        grid=(M // RB,),
        in_specs=[pl.BlockSpec((RB, D), lambda i: (i, 0)), spec_mod, spec_mod],
        out_specs=pl.BlockSpec((RB, D), lambda i: (i, 0)),
        out_shape=jax.ShapeDtypeStruct((M, D), BF16),
        compiler_params=_cparams(("arbitrary",), 32),
        name="norm_mod",
    )(tok, mul, add)


def _norm_router(tok, mul, add, w_hi, w_lo, r_bias):
    spec_mod = pl.BlockSpec((1, 1, D), _mod_index)
    spec_w = pl.BlockSpec((D, ROUTER_PAD), lambda i: (0, 0))
    return pl.pallas_call(
        _norm_router_kernel,
        grid=(M // RB,),
        in_specs=[pl.BlockSpec((RB, D), lambda i: (i, 0)), spec_mod, spec_mod, spec_w, spec_w,
                  pl.BlockSpec((1, ROUTER_PAD), lambda i: (0, 0))],
        out_specs=[pl.BlockSpec((RB, D), lambda i: (i, 0)), pl.BlockSpec((RB, ROUTER_PAD), lambda i: (i, 0))],
        out_shape=[jax.ShapeDtypeStruct((M, D), F32), jax.ShapeDtypeStruct((M, ROUTER_PAD), F32)],
        compiler_params=_cparams(("arbitrary",), 32),
        name="norm_router",
    )(tok, mul, add, w_hi, w_lo, r_bias)


def _matmul_kernel(a_ref, w_ref, o_ref):
    o_ref[...] = _dot(a_ref[...], w_ref[...]).astype(o_ref.dtype)


def _matmul(a, w, out_dtype):
    m, k = a.shape
    n = w.shape[1]
    tm, tn = TB, 512
    return pl.pallas_call(
        _matmul_kernel,
        grid=(m // tm, n // tn),
        in_specs=[pl.BlockSpec((tm, k), lambda i, j: (i, 0)), pl.BlockSpec((k, tn), lambda i, j: (0, j))],
        out_specs=pl.BlockSpec((tm, tn), lambda i, j: (i, j)),
        out_shape=jax.ShapeDtypeStruct((m, n), out_dtype),
        compiler_params=_cparams(("arbitrary", "arbitrary"), 48),
        name="proj_in",
    )(a, w)


def _out_proj_kernel(a1_ref, a2_ref, w1_ref, w2_ref, res_ref, gl_ref, gc_ref, o_ref, *, tm):
    acc = _dot(a1_ref[...], w1_ref[...]) + _dot(a2_ref[...], w2_ref[...])
    row = pl.program_id(0) * tm + lax.broadcasted_iota(jnp.int32, (tm, 1), 0)
    gate = jnp.where(row % TB < CTX, gc_ref[0], gl_ref[0])
    o_ref[...] = res_ref[...] + gate * acc


def _out_proj(a1, a2, w1, w2, tok, gate):
    tm, tn = TB // 2, 512
    k1, k2 = a1.shape[1], a2.shape[1]
    return pl.pallas_call(
        functools.partial(_out_proj_kernel, tm=tm),
        grid=(M // tm, D // tn),
        in_specs=[
            pl.BlockSpec((tm, k1), lambda i, j: (i, 0)),
            pl.BlockSpec((tm, k2), lambda i, j: (i, 0)),
            pl.BlockSpec((k1, tn), lambda i, j: (0, j)),
            pl.BlockSpec((k2, tn), lambda i, j: (0, j)),
            pl.BlockSpec((tm, tn), lambda i, j: (i, j)),
            pl.BlockSpec((1, 1, tn), lambda i, j: (i // 2, 0, j)),
            pl.BlockSpec((1, 1, tn), lambda i, j: (BATCH, 0, j)),
        ],
        out_specs=pl.BlockSpec((tm, tn), lambda i, j: (i, j)),
        out_shape=jax.ShapeDtypeStruct((M, D), F32),
        compiler_params=_cparams(("arbitrary", "arbitrary"), 48),
        name="proj_out",
    )(a1, a2, w1, w2, tok, gate, gate)


def _rope_tables(dim, reps):
    n_f = dim // 4
    inv = ROPE_BASE ** (-jnp.arange(n_f, dtype=F32) / n_f)
    t = jnp.arange(SEQ)
    pos = jnp.stack([t // GRID_W, t % GRID_W], axis=-1).astype(F32)
    ang = pos[:, :, None] * inv
    cos = jnp.cos(ang)[:, :, None, :]
    sin = jnp.sin(ang)[:, :, None, :]
    cos = jnp.broadcast_to(cos, (SEQ, 2, 2, n_f)).reshape(SEQ, dim)
    sin = jnp.concatenate([-sin, sin], axis=2).reshape(SEQ, dim)
    cos = jnp.tile(cos, (1, reps))
    sin = jnp.tile(sin, (1, reps))
    cos = jnp.concatenate([jnp.ones((CTX, dim * reps), F32), cos], axis=0)
    sin = jnp.concatenate([jnp.zeros((CTX, dim * reps), F32), sin], axis=0)
    return cos, sin


def _rope(x, cos, sin_signed, n_f):
    lane = lax.broadcasted_iota(jnp.int32, x.shape, x.ndim - 1)
    partner = jnp.where(lane % (2 * n_f) < n_f,
                        pltpu.roll(x, HEAD_DIM - n_f, axis=x.ndim - 1),
                        pltpu.roll(x, n_f, axis=x.ndim - 1))
    return x * cos + partner * sin_signed


_NA_GROUP_ROWS = RB // GRID_W
_NA_KEY_ROWS = 3 * _NA_GROUP_ROWS
_NA_PATTERN = (0, 1, 2, 2, 2, 2, 2, 3, 4)


def _na_key_block(g):
    return jnp.clip(g - 1, 0, GRID_H // _NA_GROUP_ROWS - 3)


def _na_bias_tables(rpb):
    c = np.arange(GRID_W)
    cs = np.clip(c - NA_WIN_W // 2, 0, GRID_W - NA_WIN_W)
    col_valid = (c[None, :] >= cs[:, None]) & (c[None, :] < cs[:, None] + NA_WIN_W)
    co = np.clip(c[None, :] - c[:, None] + NA_WIN_W - 1, 0, 2 * NA_WIN_W - 2)
    onehot = jnp.asarray((co[None] == np.arange(2 * NA_WIN_W - 1)[:, None, None]) & col_valid[None], F32)
    tcol = jnp.einsum("hro,oqk->hrqk", rpb.astype(F32), onehot, precision=lax.Precision.HIGHEST)
    tcol = jnp.where(jnp.asarray(col_valid)[None, None], tcol, NEG)
    neg_blk = jnp.full((NH, GRID_W, GRID_W), NEG, F32)
    tabs = []
    for g in (0, 1, 2, 7):
        r0 = _NA_GROUP_ROWS * g
        ks = int(np.clip(g - 1, 0, GRID_H // _NA_GROUP_ROWS - 3)) * _NA_GROUP_ROWS
        rows = []
        for dr in range(_NA_GROUP_ROWS):
            r = r0 + dr
            rs = int(np.clip(r - NA_WIN_H // 2, 0, GRID_H - NA_WIN_H))
            blks = []
            for krel in range(_NA_KEY_ROWS):
                kr = ks + krel
                blks.append(tcol[:, kr - r + NA_WIN_H - 1] if rs <= kr < rs + NA_WIN_H else neg_blk)
            rows.append(jnp.concatenate(blks, axis=-1))
        tabs.append(jnp.concatenate(rows, axis=-2))
    tabs.append(jnp.full((NH, RB, 3 * RB), NEG, F32))
    return jnp.stack(tabs, axis=1)


def _na_kernel(q_ref, k0_ref, k1_ref, k2_ref, v0_ref, v1_ref, v2_ref, kc_ref, vc_ref, b_ref, o_ref):
    q = (q_ref[...].astype(F32) * HEAD_DIM ** -0.5).astype(BF16)
    s = [_dot_nt(q, kr[...].astype(BF16)) + b_ref[0, 0, :, t * RB:(t + 1) * RB]
         for t, kr in enumerate((k0_ref, k1_ref, k2_ref))]
    s.append(_dot_nt(q, kc_ref[...].astype(BF16)))
    m = functools.reduce(jnp.maximum, [jnp.max(t, axis=-1, keepdims=True) for t in s])
    p = [jnp.exp(t - m) for t in s]
    l = functools.reduce(jnp.add, [jnp.sum(t, axis=-1, keepdims=True) for t in p])
    vs = (v0_ref, v1_ref, v2_ref, vc_ref)
    o = functools.reduce(jnp.add, [_dot(pt.astype(BF16), vr[...].astype(BF16)) for pt, vr in zip(p, vs)])
    o_ref[...] = (o / l).astype(o_ref.dtype)


def _na_attention(proj, bias):
    def qrow(h, g, b):
        return b * BLOCKS_PER_BATCH + jnp.where(g < 8, 1 + g, 0)

    def krow(t):
        return lambda h, g, b: (b * BLOCKS_PER_BATCH + 1 + _na_key_block(jnp.minimum(g, 7)) + t, NH + h)

    def vrow(t):
        return lambda h, g, b: (b * BLOCKS_PER_BATCH + 1 + _na_key_block(jnp.minimum(g, 7)) + t, 2 * NH + h)

    blk = (RB, HEAD_DIM)
    pattern = jnp.asarray(_NA_PATTERN, jnp.int32)

    def bias_map(h, g, b, pat_ref):
        return (h, pat_ref[g], 0, 0)

    def wrap(f):
        return lambda h, g, b, pat_ref: f(h, g, b)

    in_specs = [pl.BlockSpec(blk, wrap(lambda h, g, b: (qrow(h, g, b), h)))]
    in_specs += [pl.BlockSpec(blk, wrap(krow(t))) for t in range(3)]
    in_specs += [pl.BlockSpec(blk, wrap(vrow(t))) for t in range(3)]
    in_specs += [pl.BlockSpec(blk, wrap(lambda h, g, b: (b * BLOCKS_PER_BATCH, NH + h))),
                 pl.BlockSpec(blk, wrap(lambda h, g, b: (b * BLOCKS_PER_BATCH, 2 * NH + h))),
                 pl.BlockSpec((1, 1, RB, 3 * RB), bias_map)]

    def body(pat_ref, *refs):
        _na_kernel(*refs)

    return pl.pallas_call(
        body,
        grid_spec=pltpu.PrefetchScalarGridSpec(
            num_scalar_prefetch=1,
            grid=(NH, BLOCKS_PER_BATCH, BATCH),
            in_specs=in_specs,
            out_specs=pl.BlockSpec(blk, wrap(lambda h, g, b: (qrow(h, g, b), h))),
        ),
        out_shape=jax.ShapeDtypeStruct((M, NH * HEAD_DIM), BF16),
        compiler_params=_cparams(("arbitrary", "arbitrary", "arbitrary"), 32),
        name="na_attention",
    )(pattern, *([proj] * 9), bias)


_RET_NC = TB // RET_CHUNK
_RET_CTX_NC = CTX // RET_CHUNK


def _ret_kernel(lg_ref, q_ref, k_ref, v_ref, g_ref, cos_ref, sin_ref, o_ref, qr_ref, kr_ref, sf_ref, sb_ref):
    h = pl.program_id(1)
    lgf = lg_ref[0, h]
    lgb = lg_ref[1, h]
    c = RET_CHUNK
    qr_ref[...] = _rope(q_ref[...].astype(F32), cos_ref[...], sin_ref[...], HEAD_DIM // 4)
    kr_ref[...] = _rope(k_ref[...].astype(F32), cos_ref[...], sin_ref[...], HEAD_DIM // 4) * HEAD_DIM ** -0.5

    pos = lax.broadcasted_iota(jnp.int32, (c, 1), 0).astype(F32)
    kdec_f = jnp.exp(lgf * (c - 1.0 - pos))
    kdec_b = jnp.exp(lgb * pos)
    qdec_f = jnp.exp(lgf * (pos + 1.0))
    qdec_b = jnp.exp(lgb * (c - pos))
    gc_f = jnp.exp(lgf * c)
    gc_b = jnp.exp(lgb * c)
    rel = (lax.broadcasted_iota(jnp.int32, (c, c), 0) - lax.broadcasted_iota(jnp.int32, (c, c), 1)).astype(F32)
    dmask = (jnp.where(rel >= 0, jnp.exp(jnp.maximum(rel, 0.0) * lgf), 0.0)
             + jnp.where(rel <= 0, jnp.exp(jnp.maximum(-rel, 0.0) * lgb), 0.0))

    def state_step(n, s, s_ref, kdec, gc):
        rows = pl.ds(pl.multiple_of(n * c, c), c)
        s_ref[n] = s
        kn = (kr_ref[rows, :] * kdec).astype(BF16)
        return s * gc + _dot_tn(kn, v_ref[rows, :].astype(BF16))

    zero = jnp.zeros((HEAD_DIM, RET_DV), F32)
    lax.fori_loop(0, _RET_NC, lambda n, s: state_step(n, s, sf_ref, kdec_f, gc_f), zero)
    s = lax.fori_loop(0, _RET_CTX_NC,
                      lambda i, s: state_step(_RET_CTX_NC - 1 - i, s, sb_ref, kdec_b, gc_b), zero)
    lax.fori_loop(0, _RET_NC - _RET_CTX_NC,
                  lambda i, s: state_step(_RET_NC - 1 - i, s, sb_ref, kdec_b, gc_b), s)

    def out_step(n, carry):
        rows = pl.ds(pl.multiple_of(n * c, c), c)
        qn = qr_ref[rows, :]
        vn = v_ref[rows, :].astype(BF16)
        scores = _dot_nt(qn.astype(BF16), kr_ref[rows, :].astype(BF16)) * dmask
        intra = _dot(scores.astype(BF16), vn)
        qcat = jnp.concatenate([qn * qdec_f, qn * qdec_b], axis=1).astype(BF16)
        scat = jnp.concatenate([sf_ref[n], sb_ref[n]], axis=0).astype(BF16)
        o = intra + _dot(qcat, scat)
        o = o * lax.rsqrt(jnp.mean(o * o, axis=-1, keepdims=True) + EPS)
        o_ref[rows, :] = (o * _silu(g_ref[rows, :].astype(F32))).astype(o_ref.dtype)
        return carry

    lax.fori_loop(0, _RET_NC, out_step, 0)


def _retention(proj, log_g, cos, sin):
    seq128 = lambda blk0: pl.BlockSpec((TB, HEAD_DIM), lambda b, h, lg: (b, blk0 + h))
    seq256 = lambda blk0: pl.BlockSpec((TB, RET_DV), lambda b, h, lg: (b, blk0 + h))
    tab = pl.BlockSpec((TB, HEAD_DIM), lambda b, h, lg: (0, 0))
    return pl.pallas_call(
        _ret_kernel,
        grid_spec=pltpu.PrefetchScalarGridSpec(
            num_scalar_prefetch=1,
            grid=(BATCH, NH),
            in_specs=[seq128(3072 // HEAD_DIM), seq128(4096 // HEAD_DIM),
                      seq256(5120 // RET_DV), seq256(7168 // RET_DV), tab, tab],
            out_specs=pl.BlockSpec((TB, RET_DV), lambda b, h, lg: (b, h)),
            scratch_shapes=[pltpu.VMEM((TB, HEAD_DIM), F32), pltpu.VMEM((TB, HEAD_DIM), F32),
                            pltpu.VMEM((_RET_NC, HEAD_DIM, RET_DV), F32),
                            pltpu.VMEM((_RET_NC, HEAD_DIM, RET_DV), F32)],
        ),
        out_shape=jax.ShapeDtypeStruct((M, NH * RET_DV), BF16),
        compiler_params=_cparams(("arbitrary", "arbitrary"), 48),
        name="retention",
    )(log_g, proj, proj, proj, proj, cos, sin)


def _diff_kernel(lam_ref, q_ref, k_ref, v_ref, cq_ref, sq_ref, ck_ref, sk_ref, o_ref, kb_ref, vb_ref, *, post_scale):
    qi = pl.program_id(2)

    @pl.when(qi == 0)
    def _():
        kb_ref[...] = _rope(k_ref[...].astype(F32), ck_ref[...], sk_ref[...], DIFF_DK // 4).astype(BF16)
        vb_ref[...] = v_ref[...].astype(BF16)
        o_ref[...] = jnp.zeros_like(o_ref)

    @pl.when(qi > 0)
    def _():
        lam = lam_ref[0]
        q = _rope(q_ref[...].astype(F32), cq_ref[...], sq_ref[...], DIFF_DK // 4) * DIFF_DK ** -0.5
        lane = lax.broadcasted_iota(jnp.int32, q.shape, 1)
        kb = kb_ref[...]
        p, inv = [], []
        for t in range(2):
            qt = jnp.where((lane >= DIFF_DK) == (t == 1), q, 0.0).astype(BF16)
            s = _dot_nt(qt, kb)
            e = jnp.exp(s - jnp.max(s, axis=-1, keepdims=True))
            p.append(e)
            inv.append(1.0 / jnp.sum(e, axis=-1, keepdims=True))
        a = (p[0] * inv[0] - p[1] * (lam * inv[1])).astype(BF16)
        o = _dot(a, vb_ref[...])
        o = o * lax.rsqrt(jnp.mean(o * o, axis=-1, keepdims=True) + EPS)
        o_ref[...] = (o * post_scale).astype(o_ref.dtype)


def _diff_attention(proj, lam, cos, sin, post_scale):
    blk = (RB, HEAD_DIM)
    seq = (TB, HEAD_DIM)
    qmap = lambda b, h, qi, lam_ref: (b * BLOCKS_PER_BATCH + qi, h)
    return pl.pallas_call(
        functools.partial(_diff_kernel, post_scale=post_scale),
        grid_spec=pltpu.PrefetchScalarGridSpec(
            num_scalar_prefetch=1,
            grid=(BATCH, NH, BLOCKS_PER_BATCH),
            in_specs=[pl.BlockSpec(blk, qmap),
                      pl.BlockSpec(seq, lambda b, h, qi, lam_ref: (b, NH + h)),
                      pl.BlockSpec(seq, lambda b, h, qi, lam_ref: (b, 2 * NH + h)),
                      pl.BlockSpec(blk, lambda b, h, qi, lam_ref: (qi, 0)),
                      pl.BlockSpec(blk, lambda b, h, qi, lam_ref: (qi, 0)),
                      pl.BlockSpec(seq, lambda b, h, qi, lam_ref: (0, 0)),
                      pl.BlockSpec(seq, lambda b, h, qi, lam_ref: (0, 0))],
            out_specs=pl.BlockSpec(blk, qmap),
            scratch_shapes=[pltpu.VMEM(seq, BF16), pltpu.VMEM(seq, BF16)],
        ),
        out_shape=jax.ShapeDtypeStruct((M, NH * HEAD_DIM), BF16),
        compiler_params=_cparams(("arbitrary", "arbitrary", "arbitrary"), 48),
        name="diff_attention",
    )(lam, proj, proj, proj, cos, sin, cos, sin)


_HG_NC = TB // HG_CHUNK
_HG_CTX_NC = CTX // HG_CHUNK


def _chunk_cumsum(x, reverse):
    c = x.shape[0]
    row = lax.broadcasted_iota(jnp.int32, x.shape, 0)
    s = 1
    while s < c:
        if reverse:
            x = x + jnp.where(row < c - s, pltpu.roll(x, c - s, axis=0), 0.0)
        else:
            x = x + jnp.where(row >= s, pltpu.roll(x, s, axis=0), 0.0)
        s *= 2
    return x


def _hg_kernel(q_ref, ff_ref, fb_ref, v_ref, g_ref, lb_ref, o_ref,
               qs_ref, kf_ref, kb_ref, cf_ref, cb_ref, sf_ref, sb_ref, od_ref):
    c = HG_CHUNK
    lb = lb_ref[0]
    qs_ref[...] = _silu(q_ref[...].astype(F32))

    def gate(raw_ref, k_ref, lf_ref):
        f = lb + (1.0 - lb) * (1.0 / (1.0 + jnp.exp(-raw_ref[...].astype(F32))))
        k_ref[...] = 1.0 - f
        lf_ref[...] = jnp.log(f)

    gate(ff_ref, kf_ref, cf_ref)
    gate(fb_ref, kb_ref, cb_ref)

    def state_step(n, st, s_ref, k_ref, cum_ref, reverse):
        rows = pl.ds(pl.multiple_of(n * c, c), c)
        s_ref[n] = st
        cum = _chunk_cumsum(cum_ref[rows, :], reverse)
        cum_ref[rows, :] = cum
        last = cum[0:1, :] if reverse else cum[c - 1:c, :]
        kt = (k_ref[rows, :] * jnp.exp(last - cum)).astype(BF16)
        return st * jnp.exp(last) + _dot_tn(v_ref[rows, :].astype(BF16), kt)

    zero = jnp.zeros((HEAD_DIM, HEAD_DIM), F32)
    lax.fori_loop(0, _HG_NC, lambda n, s: state_step(n, s, sf_ref, kf_ref, cf_ref, False), zero)
    s = lax.fori_loop(0, _HG_CTX_NC,
                      lambda i, s: state_step(_HG_CTX_NC - 1 - i, s, sb_ref, kb_ref, cb_ref, True), zero)
    lax.fori_loop(0, _HG_NC - _HG_CTX_NC,
                  lambda i, s: state_step(_HG_NC - 1 - i, s, sb_ref, kb_ref, cb_ref, True), s)

    o_ref[0:CTX, :] = jnp.zeros((CTX, HEAD_DIM), o_ref.dtype)

    row = lax.broadcasted_iota(jnp.int32, (c, 1), 0)
    ri = lax.broadcasted_iota(jnp.int32, (c, c), 0)
    ci = lax.broadcasted_iota(jnp.int32, (c, c), 1)
    sub_row = lax.broadcasted_iota(jnp.int32, (HG_SUB, 1), 0)

    def level(q, kf, kb, cf, cb, half):
        grp = row // (2 * half)
        upper = (row % (2 * half)) >= half
        ref_f = jnp.zeros_like(cf)
        ref_b = jnp.zeros_like(cb)
        for gi in range(c // (2 * half)):
            base = gi * 2 * half
            ref_f = jnp.where(grp == gi, cf[base + half - 1:base + half, :], ref_f)
            ref_b = jnp.where(grp == gi, cb[base + half:base + half + 1, :], ref_b)
        zf = jnp.minimum(jnp.where(upper, cf - ref_f, ref_f - cf), 0.0)
        zb = jnp.minimum(jnp.where(upper, ref_b - cb, cb - ref_b), 0.0)
        ef = jnp.exp(zf)
        eb = jnp.exp(zb)
        qq = jnp.concatenate([jnp.where(upper, q * ef, 0.0), jnp.where(upper, 0.0, q * eb)], axis=1)
        kk = jnp.concatenate([jnp.where(upper, 0.0, kf * ef), jnp.where(upper, kb * eb, 0.0)], axis=1)
        a = _dot_nt(qq.astype(BF16), kk.astype(BF16))
        return jnp.where(ri // (2 * half) == ci // (2 * half), a, 0.0)

    def out_step(n, carry):
        rows = pl.ds(pl.multiple_of(n * c, c), c)
        q = qs_ref[rows, :]
        kf = kf_ref[rows, :]
        kb = kb_ref[rows, :]
        cf = cf_ref[rows, :]
        cb = cb_ref[rows, :]
        v = v_ref[rows, :].astype(F32)
        qcat = jnp.concatenate([q * jnp.exp(cf), q * jnp.exp(cb)], axis=1).astype(BF16)
        scat = jnp.concatenate([sf_ref[n], sb_ref[n]], axis=1).astype(BF16)
        o = _dot_nt(qcat, scat)
        att = None
        half = c // 2
        while half >= HG_SUB:
            a = level(q, kf, kb, cf, cb, half)
            att = a if att is None else att + a
            half //= 2
        o = o + _dot(att.astype(BF16), v.astype(BF16))
        for blk in range(c // HG_SUB):
            lo = blk * HG_SUB
            qb, kfb, kbb = q[lo:lo + HG_SUB], kf[lo:lo + HG_SUB], kb[lo:lo + HG_SUB]
            cfb, cbb, vb = cf[lo:lo + HG_SUB], cb[lo:lo + HG_SUB], v[lo:lo + HG_SUB]
            for i in range(HG_SUB):
                ef = jnp.where(sub_row <= i, jnp.exp(jnp.minimum(cfb[i:i + 1] - cfb, 0.0)), 0.0)
                eb = jnp.where(sub_row >= i, jnp.exp(jnp.minimum(cbb[i:i + 1] - cbb, 0.0)), 0.0)
                w = jnp.sum(qb[i:i + 1] * (ef * kfb + eb * kbb), axis=-1, keepdims=True)
                od_ref[lo + i:lo + i + 1, :] = jnp.sum(w * vb, axis=0, keepdims=True)
        o = o + od_ref[...]
        o = o * lax.rsqrt(jnp.mean(o * o, axis=-1, keepdims=True) + EPS)
        o_ref[rows, :] = (o * _silu(g_ref[rows, :].astype(F32))).astype(o_ref.dtype)
        return carry

    lax.fori_loop(_HG_CTX_NC, _HG_NC, out_step, 0)


def _hgrn(proj, lb):
    seq = lambda col0: pl.BlockSpec((TB, HEAD_DIM), lambda b, h: (b, col0 // HEAD_DIM + h))
    full = pltpu.VMEM((TB, HEAD_DIM), F32)
    states = pltpu.VMEM((_HG_NC, HEAD_DIM, HEAD_DIM), F32)
    return pl.pallas_call(
        _hg_kernel,
        grid=(BATCH, NH),
        in_specs=[seq(3072), seq(4096), seq(5120), seq(6144), seq(7168),
                  pl.BlockSpec((1, 1, HEAD_DIM), lambda b, h: (h, 0, 0))],
        out_specs=pl.BlockSpec((TB, HEAD_DIM), lambda b, h: (b, h)),
        out_shape=jax.ShapeDtypeStruct((M, NH * HEAD_DIM), BF16),
        scratch_shapes=[full, full, full, full, full, states, states, pltpu.VMEM((HG_CHUNK, HEAD_DIM), F32)],
        compiler_params=_cparams(("arbitrary", "arbitrary"), 48),
        name="hgrn",
    )(proj, proj, proj, proj, proj, lb.reshape(NH, 1, HEAD_DIM))


def _row_gather_start(idx_ref, k, src_hbm, dst, sem, n):
    def body(r, carry):
        pltpu.make_async_copy(src_hbm.at[pl.ds(idx_ref[0, k, r], 1)], dst.at[pl.ds(r, 1)], sem).start()
        return carry
    lax.fori_loop(0, n, body, 0, unroll=GATHER_UNROLL)


def _row_gather_wait(src_hbm, dst, sem, n):
    pltpu.make_async_copy(src_hbm.at[pl.ds(0, n)], dst, sem).wait()


def _moe_up_kernel(be_ref, nu_ref, rt0_ref, rt1_ref, x_hbm, w1_ref, w3_ref, o_ref, xbuf, sem, w1b_ref, w3b_ref):
    i = pl.program_id(0)
    nu = nu_ref[0]

    @pl.when(jnp.logical_and(i == 0, nu > 0))
    def _():
        _row_gather_start(rt0_ref, 0, x_hbm, xbuf.at[0], sem.at[0], MOE_TM)

    @pl.when(i + 1 < nu)
    def _():
        slot = (i + 1) % 2
        _row_gather_start(rt1_ref, 0, x_hbm, xbuf.at[slot], sem.at[slot], MOE_TM)

    changed = jnp.logical_or(i == 0, be_ref[i] != be_ref[jnp.maximum(i - 1, 0)])

    @pl.when(changed)
    def _():
        w1b_ref[...] = w1_ref[0, 0].astype(BF16)
        w3b_ref[...] = w3_ref[0, 0].astype(BF16)

    @pl.when(i < nu)
    def _():
        slot = i % 2
        _row_gather_wait(x_hbm, xbuf.at[slot], sem.at[slot], MOE_TM)
        x = xbuf[slot].astype(BF16)
        a = _dot(x, w1b_ref[...])
        o_ref[...] = (_silu(a) * _dot(x, w3b_ref[...])).astype(o_ref.dtype)

    @pl.when(i >= nu)
    def _():
        o_ref[...] = jnp.zeros_like(o_ref)


def _moe_up(layer, blk_e, n_used, row_tok, h2, w1, w3):
    wspec = pl.BlockSpec((1, 1, D, EXPERT_FF), lambda i, be, nu: (layer, be[i], 0, 0))
    rt = row_tok.reshape(MOE_BLOCKS, 1, MOE_TM)
    smem_blk = lambda f: pl.BlockSpec((1, 1, MOE_TM), f, memory_space=pltpu.SMEM)
    return pl.pallas_call(
        _moe_up_kernel,
        grid_spec=pltpu.PrefetchScalarGridSpec(
            num_scalar_prefetch=2,
            grid=(MOE_BLOCKS,),
            in_specs=[smem_blk(lambda i, be, nu: (i, 0, 0)),
                      smem_blk(lambda i, be, nu: (jnp.minimum(i + 1, MOE_BLOCKS - 1), 0, 0)),
                      pl.BlockSpec(memory_space=pl.ANY), wspec, wspec],
            out_specs=pl.BlockSpec((MOE_TM, EXPERT_FF), lambda i, be, nu: (i, 0)),
            scratch_shapes=[pltpu.VMEM((2, MOE_TM, D), F32), pltpu.SemaphoreType.DMA((2,)),
                            pltpu.VMEM((D, EXPERT_FF), BF16), pltpu.VMEM((D, EXPERT_FF), BF16)],
        ),
        out_shape=jax.ShapeDtypeStruct((MOE_ROWS, EXPERT_FF), BF16),
        compiler_params=_cparams(("arbitrary",), 52),
        name="moe_up",
    )(blk_e, n_used, rt, rt, h2, w1, w3)


def _moe_down_kernel(be_ref, nu_ref, h_ref, w2_ref, o_ref, w2b_ref):
    i = pl.program_id(0)
    changed = jnp.logical_or(i == 0, be_ref[i] != be_ref[jnp.maximum(i - 1, 0)])

    @pl.when(changed)
    def _():
        w2b_ref[...] = w2_ref[0, 0].astype(BF16)

    @pl.when(i < nu_ref[0])
    def _():
        o_ref[...] = _dot(h_ref[...], w2b_ref[...])

    @pl.when(i >= nu_ref[0])
    def _():
        o_ref[...] = jnp.zeros_like(o_ref)


def _moe_down(layer, blk_e, n_used, hmid, w2):
    return pl.pallas_call(
        _moe_down_kernel,
        grid_spec=pltpu.PrefetchScalarGridSpec(
            num_scalar_prefetch=2,
            grid=(MOE_BLOCKS,),
            in_specs=[pl.BlockSpec((MOE_TM, EXPERT_FF), lambda i, be, nu: (i, 0)),
                      pl.BlockSpec((1, 1, EXPERT_FF, D), lambda i, be, nu: (layer, be[i], 0, 0))],
            out_specs=pl.BlockSpec((MOE_TM, D), lambda i, be, nu: (i, 0)),
            scratch_shapes=[pltpu.VMEM((EXPERT_FF, D), BF16)],
        ),
        out_shape=jax.ShapeDtypeStruct((MOE_ROWS, D), F32),
        compiler_params=_cparams(("arbitrary",), 40),
        name="moe_down",
    )(blk_e, n_used, hmid, w2)


def _combine_kernel(p0_ref, p1_ref, y_hbm, x_ref, w_ref, g_ref, o_ref, ybuf, sem):
    i = pl.program_id(0)
    n = pl.num_programs(0)

    def start(p_ref, slot):
        for k in range(TOP_K):
            _row_gather_start(p_ref, k, y_hbm, ybuf.at[slot, k], sem.at[slot], RB)

    @pl.when(i == 0)
    def _():
        start(p0_ref, 0)

    @pl.when(i + 1 < n)
    def _():
        start(p1_ref, (i + 1) % 2)

    slot = i % 2
    for k in range(TOP_K):
        _row_gather_wait(y_hbm, ybuf.at[slot, k], sem.at[slot], RB)
    w = w_ref[...]
    y = w[:, 0:1] * ybuf[slot, 0] + w[:, 1:2] * ybuf[slot, 1]
    o_ref[...] = x_ref[...] + g_ref[0] * y


def _moe_combine(tok, y, pos, gates, gate_mod):
    nb = M // RB
    pblk = pos.reshape(nb, RB, TOP_K).transpose(0, 2, 1)
    row = pl.BlockSpec((RB, D), lambda i: (i, 0))
    smem_blk = lambda f: pl.BlockSpec((1, TOP_K, RB), f, memory_space=pltpu.SMEM)
    return pl.pallas_call(
        _combine_kernel,
        grid=(nb,),
        in_specs=[smem_blk(lambda i: (i, 0, 0)), smem_blk(lambda i: (jnp.minimum(i + 1, nb - 1), 0, 0)),
                  pl.BlockSpec(memory_space=pl.ANY), row,
                  pl.BlockSpec((RB, TOP_K), lambda i: (i, 0)), pl.BlockSpec((1, 1, D), _mod_index)],
        out_specs=row,
        out_shape=jax.ShapeDtypeStruct((M, D), F32),
        scratch_shapes=[pltpu.VMEM((2, TOP_K, RB, D), F32), pltpu.SemaphoreType.DMA((2,))],
        compiler_params=_cparams(("arbitrary",), 40),
        name="moe_combine",
    )(pblk, pblk, y, tok, gates, gate_mod)


def _route(logits, valid):
    lg_g = logits[:, :N_GROUPS]
    p_g = jax.nn.softmax(lg_g, axis=-1)
    g_top = jnp.argmax(lg_g, axis=-1)
    w_grp = jnp.take_along_axis(p_g, g_top[:, None], axis=1)
    lg_e = logits[:, N_GROUPS:N_GROUPS + N_EXPERTS].reshape(M, N_GROUPS, EXPERTS_PER_GROUP)
    lg_in = jnp.take_along_axis(lg_e, g_top[:, None, None], axis=1)[:, 0]
    top_v, top_i = lax.top_k(lg_in, TOP_K)
    gates = w_grp * jax.nn.softmax(top_v, axis=-1)
    gates = jnp.where(valid[:, None], gates, 0.0)
    eid = g_top[:, None] * EXPERTS_PER_GROUP + top_i
    eid = jnp.where(valid[:, None], eid, N_EXPERTS).reshape(-1).astype(jnp.int32)
    n_slots = M * TOP_K
    slot = jnp.arange(n_slots, dtype=jnp.int32)
    experts = jnp.arange(N_EXPERTS + 1, dtype=jnp.int32)
    counts = jnp.sum((eid[:, None] == experts[None, :]).astype(jnp.int32), axis=0)
    padded = (counts + MOE_TM - 1) // MOE_TM * MOE_TM
    p_end = jnp.cumsum(padded)
    p_start = p_end - padded
    u_start = jnp.cumsum(counts) - counts
    shift = p_start - u_start
    s_e, order = lax.sort((eid, slot), num_keys=1, is_stable=True)
    dest = slot + jnp.sum(jnp.where(s_e[:, None] == experts[None, :], shift[None, :], 0), axis=1)
    dest = jnp.where(s_e < N_EXPERTS, dest, 0)
    _, pos = lax.sort((order, dest), num_keys=1)
    pos = pos.reshape(M, TOP_K)
    blk_start = jnp.arange(MOE_BLOCKS, dtype=jnp.int32) * MOE_TM
    blk_e = jnp.sum((p_end[None, :N_EXPERTS] <= blk_start[:, None]).astype(jnp.int32), axis=1)
    blk_e = jnp.minimum(blk_e, N_EXPERTS - 1)
    onehot_be = blk_e[:, None] == experts[None, :N_EXPERTS]
    blk_shift = jnp.sum(jnp.where(onehot_be, shift[None, :N_EXPERTS], 0), axis=1)
    blk_lim = jnp.sum(jnp.where(onehot_be, (p_start + counts)[None, :N_EXPERTS], 0), axis=1)
    rows = jnp.arange(MOE_ROWS, dtype=jnp.int32).reshape(MOE_BLOCKS, MOE_TM)
    src = jnp.clip(rows - blk_shift[:, None], 0, n_slots - 1)
    row_tok = jnp.where(rows < blk_lim[:, None], jnp.take(order // TOP_K, src), 0)
    n_used = (p_end[N_EXPERTS - 1] // MOE_TM).astype(jnp.int32).reshape(1)
    return gates, row_tok, pos, blk_e, n_used


def _moe(layer, tok, mul, add, gate_mod, wg, bg, we, be, w1, w3, w2, valid):
    wr = jnp.zeros((D, ROUTER_PAD), F32).at[:, :N_GROUPS].set(wg).at[:, N_GROUPS:N_GROUPS + N_EXPERTS].set(we)
    w_hi = wr.astype(BF16)
    w_lo = (wr - w_hi.astype(F32)).astype(BF16)
    r_bias = jnp.zeros((1, ROUTER_PAD), F32).at[0, :N_GROUPS].set(bg).at[0, N_GROUPS:N_GROUPS + N_EXPERTS].set(be)
    h2, logits = _norm_router(tok, mul, add, w_hi, w_lo, r_bias)
    gates, row_tok, pos, blk_e, n_used = _route(logits, valid)
    hmid = _moe_up(layer, blk_e, n_used, row_tok, h2, w1, w3)
    y = _moe_down(layer, blk_e, n_used, hmid, w2)
    return _moe_combine(tok, y, pos, gates, gate_mod)


def _final_kernel(x_ref, w_ref, o_ref):
    x = x_ref[...]
    o_ref[0] = x * lax.rsqrt(jnp.mean(x * x, axis=-1, keepdims=True) + EPS) * w_ref[...]


def _final_norm(tok, w):
    nb = SEQ // RB
    return pl.pallas_call(
        _final_kernel,
        grid=(BATCH, nb),
        in_specs=[pl.BlockSpec((RB, D), lambda b, g: (b * BLOCKS_PER_BATCH + 1 + g, 0)),
                  pl.BlockSpec((1, D), lambda b, g: (0, 0))],
        out_specs=pl.BlockSpec((1, RB, D), lambda b, g: (b, g, 0)),
        out_shape=jax.ShapeDtypeStruct((BATCH, SEQ, D), F32),
        compiler_params=_cparams(("arbitrary", "arbitrary"), 32),
        name="final_norm",
    )(tok, w.reshape(1, D))


def _mod9(t):
    return t[:BATCH + 1].reshape(BATCH + 1, 1, D)


def kernel(x, c, ctx, c_ctx, ada_w, ada_b, norm1_w, norm2_w, w_in_even, w_out_even, na_rpb, ret_decay,
           w_in_odd, w_out_odd, diff_lambda, hg_lb_logits, router_g_w, router_g_b, router_e_w, router_e_b,
           moe_w1, moe_w3, moe_w2, norm_f_w):
    depth = ada_w.shape[0]
    tok = jnp.concatenate([ctx, x], axis=1).reshape(M, D)
    c16 = jnp.concatenate([c, c_ctx[None], jnp.zeros((16 - BATCH - 1, D), F32)], axis=0)
    mod = _ada_mod(c16, ada_w, ada_b)
    lb_all = jnp.cumsum(jax.nn.softmax(hg_lb_logits.astype(F32), axis=0), axis=0)
    lb_all = lb_all - lb_all[0]
    is_latent = (jnp.arange(M) % TB) >= CTX
    all_rows = jnp.ones((M,), bool)

    for l in range(depth):
        with_ctx = l < depth - 1
        sh1, s1, g1, sh2, s2, g2 = [_mod9(t) for t in jnp.split(mod[l], 6, axis=-1)]
        h = _norm_mod(tok, norm1_w[l] * (1.0 + s1), sh1)
        j = l // 2
        if l % 2 == 0:
            proj = _matmul(h, w_in_even[j].astype(BF16), BF16)
            a = _na_attention(proj, _na_bias_tables(na_rpb[j]))
            cos, sin = _rope_tables(HEAD_DIM, 1)
            r = _retention(proj, -jnp.exp(ret_decay[j].astype(F32)), cos, sin)
            w_out = w_out_even[j].astype(BF16)
            tok = _out_proj(a, r, w_out[:NH * HEAD_DIM], w_out[NH * HEAD_DIM:], tok, g1)
        else:
            proj = _matmul(h, w_in_odd[j].astype(BF16), BF16)
            lp = diff_lambda[j].astype(F32)
            lam_init = 0.8 - 0.6 * math.exp(-0.3 * l)
            lam = jnp.exp(jnp.sum(lp[0] * lp[1])) - jnp.exp(jnp.sum(lp[2] * lp[3])) + lam_init
            cos, sin = _rope_tables(DIFF_DK, 2)
            d_l = _diff_attention(proj, lam.reshape(1), cos, sin, 1.0 - lam_init)
            g_l = _hgrn(proj, lb_all[l])
            w_out = w_out_odd[j].astype(BF16)
            tok = _out_proj(d_l, g_l, w_out[:NH * HEAD_DIM], w_out[NH * HEAD_DIM:], tok, g1)
        tok = _moe(l, tok, norm2_w[l] * (1.0 + s2), sh2, g2,
                   router_g_w[l], router_g_b[l], router_e_w[l], router_e_b[l],
                   moe_w1, moe_w3, moe_w2, all_rows if with_ctx else is_latent)
    return _final_norm(tok, norm_f_w)
```

```python
import functools
import math

import numpy as np
import jax
import jax.numpy as jnp
from jax import lax
from jax.experimental import pallas as pl
from jax.experimental.pallas import tpu as pltpu

F32 = jnp.float32
BF16 = jnp.bfloat16

D = 2048
BATCH = 8
SEQ = 2048
CTX = 256
TB = SEQ + CTX
M = BATCH * TB
GRID_W = 64
GRID_H = SEQ // GRID_W
HEAD_DIM = 128
NH = 8
NA_WIN_H = 8
NA_WIN_W = 16
RET_DV = 256
RET_CHUNK = 128
DIFF_DK = 64
HG_CHUNK = 64
HG_SUB = 8
N_GROUPS = 4
EXPERTS_PER_GROUP = 8
N_EXPERTS = 32
TOP_K = 2
EXPERT_FF = 1024
ROPE_BASE = 10000.0
EPS = 1e-6
NEG = -1e30
LOG2E = math.log2(math.e)

RB = 256
BLOCKS_PER_BATCH = TB // RB
MOE_TM = 256
MOE_ROWS = M * TOP_K + N_EXPERTS * MOE_TM
MOE_BLOCKS = MOE_ROWS // MOE_TM
ROUTER_PAD = 128
GATHER_UNROLL = 8


def _cparams(sem, vmem_mb):
    return pltpu.CompilerParams(dimension_semantics=sem, vmem_limit_bytes=vmem_mb * 1024 * 1024)


def _sigmoid(x):
    return 0.5 * jnp.tanh(0.5 * x) + 0.5


def _silu(x):
    return x * _sigmoid(x)


_NT = (((1,), (1,)), ((), ()))
_TN = (((0,), (0,)), ((), ()))


def _dot(a, b):
    return jnp.dot(a, b, preferred_element_type=F32)


def _dot_nt(a, b):
    return lax.dot_general(a, b, _NT, preferred_element_type=F32)


def _dot_tn(a, b):
    return lax.dot_general(a, b, _TN, preferred_element_type=F32)


def _ada_kernel(c_ref, w_ref, b_ref, o_ref):
    a = _silu(c_ref[...]).astype(BF16)
    o_ref[0] = _dot(a, w_ref[0].astype(BF16)) + b_ref[0]


def _ada_mod(c16, ada_w, ada_b):
    depth, _, n = ada_w.shape
    tn = 1024
    return pl.pallas_call(
        _ada_kernel,
        grid=(depth, n // tn),
        in_specs=[
            pl.BlockSpec((16, D), lambda l, j: (0, 0)),
            pl.BlockSpec((1, D, tn), lambda l, j: (l, 0, j)),
            pl.BlockSpec((1, 1, tn), lambda l, j: (l, 0, j)),
        ],
        out_specs=pl.BlockSpec((1, 16, tn), lambda l, j: (l, 0, j)),
        out_shape=jax.ShapeDtypeStruct((depth, 16, n), F32),
        compiler_params=_cparams(("arbitrary", "arbitrary"), 40),
        name="ada_mod",
    )(c16, ada_w, ada_b.reshape(depth, 1, n))


def _mod_index(i):
    return (jnp.where(i % BLOCKS_PER_BATCH == 0, BATCH, i // BLOCKS_PER_BATCH), 0, 0)


def _norm_body(x_ref, mul_ref, add_ref):
    x = x_ref[...]
    y = x * lax.rsqrt(jnp.mean(x * x, axis=-1, keepdims=True) + EPS)
    return y * mul_ref[0] + add_ref[0]


def _norm_kernel(x_ref, mul_ref, add_ref, o_ref):
    o_ref[...] = _norm_body(x_ref, mul_ref, add_ref).astype(BF16)


def _norm_router_kernel(x_ref, mul_ref, add_ref, whi_ref, wlo_ref, rb_ref, o_ref, lg_ref):
    h = _norm_body(x_ref, mul_ref, add_ref)
    hi = h.astype(BF16)
    lo = (h - hi.astype(F32)).astype(BF16)
    o_ref[...] = h
    lg_ref[...] = (_dot(hi, whi_ref[...]) + _dot(hi, wlo_ref[...]) + _dot(lo, whi_ref[...])) + rb_ref[...]


def _norm_mod(tok, mul, add):
    spec_mod = pl.BlockSpec((1, 1, D), _mod_index)
    return pl.pallas_call(
        _norm_kernel,
        grid=(M // RB,),
        in_specs=[pl.BlockSpec((RB, D), lambda i: (i, 0)), spec_mod, spec_mod],
        out_specs=pl.BlockSpec((RB, D), lambda i: (i, 0)),
        out_shape=jax.ShapeDtypeStruct((M, D), BF16),
        compiler_params=_cparams(("arbitrary",), 32),
        name="norm_mod",
    )(tok, mul, add)


def _norm_router(tok, mul, add, w_hi, w_lo, r_bias):
    spec_mod = pl.BlockSpec((1, 1, D), _mod_index)
    spec_w = pl.BlockSpec((D, ROUTER_PAD), lambda i: (0, 0))
    return pl.pallas_call(
        _norm_router_kernel,
        grid=(M // RB,),
        in_specs=[pl.BlockSpec((RB, D), lambda i: (i, 0)), spec_mod, spec_mod, spec_w, spec_w,
                  pl.BlockSpec((1, ROUTER_PAD), lambda i: (0, 0))],
        out_specs=[pl.BlockSpec((RB, D), lambda i: (i, 0)), pl.BlockSpec((RB, ROUTER_PAD), lambda i: (i, 0))],
        out_shape=[jax.ShapeDtypeStruct((M, D), F32), jax.ShapeDtypeStruct((M, ROUTER_PAD), F32)],
        compiler_params=_cparams(("arbitrary",), 32),
        name="norm_router",
    )(tok, mul, add, w_hi, w_lo, r_bias)


def _matmul_kernel(a_ref, w_ref, o_ref):
    o_ref[...] = _dot(a_ref[...], w_ref[...]).astype(o_ref.dtype)


def _matmul(a, w, out_dtype):
    m, k = a.shape
    n = w.shape[1]
    tm, tn = TB, 512
    return pl.pallas_call(
        _matmul_kernel,
        grid=(m // tm, n // tn),
        in_specs=[pl.BlockSpec((tm, k), lambda i, j: (i, 0)), pl.BlockSpec((k, tn), lambda i, j: (0, j))],
        out_specs=pl.BlockSpec((tm, tn), lambda i, j: (i, j)),
        out_shape=jax.ShapeDtypeStruct((m, n), out_dtype),
        compiler_params=_cparams(("arbitrary", "arbitrary"), 48),
        name="proj_in",
    )(a, w)


def _out_proj_kernel(a1_ref, a2_ref, w1_ref, w2_ref, res_ref, gl_ref, gc_ref, o_ref, *, tm):
    acc = _dot(a1_ref[...], w1_ref[...]) + _dot(a2_ref[...], w2_ref[...])
    row = pl.program_id(0) * tm + lax.broadcasted_iota(jnp.int32, (tm, 1), 0)
    gate = jnp.where(row % TB < CTX, gc_ref[0], gl_ref[0])
    o_ref[...] = res_ref[...] + gate * acc


def _out_proj(a1, a2, w1, w2, tok, gate):
    tm, tn = TB // 2, 512
    k1, k2 = a1.shape[1], a2.shape[1]
    return pl.pallas_call(
        functools.partial(_out_proj_kernel, tm=tm),
        grid=(M // tm, D // tn),
        in_specs=[
            pl.BlockSpec((tm, k1), lambda i, j: (i, 0)),
            pl.BlockSpec((tm, k2), lambda i, j: (i, 0)),
            pl.BlockSpec((k1, tn), lambda i, j: (0, j)),
            pl.BlockSpec((k2, tn), lambda i, j: (0, j)),
            pl.BlockSpec((tm, tn), lambda i, j: (i, j)),
            pl.BlockSpec((1, 1, tn), lambda i, j: (i // 2, 0, j)),
            pl.BlockSpec((1, 1, tn), lambda i, j: (BATCH, 0, j)),
        ],
        out_specs=pl.BlockSpec((tm, tn), lambda i, j: (i, j)),
        out_shape=jax.ShapeDtypeStruct((M, D), F32),
        compiler_params=_cparams(("arbitrary", "arbitrary"), 48),
        name="proj_out",
    )(a1, a2, w1, w2, tok, gate, gate)


def _rope_tables(dim, reps):
    n_f = dim // 4
    inv = ROPE_BASE ** (-jnp.arange(n_f, dtype=F32) / n_f)
    t = jnp.arange(SEQ)
    pos = jnp.stack([t // GRID_W, t % GRID_W], axis=-1).astype(F32)
    ang = pos[:, :, None] * inv
    cos = jnp.cos(ang)[:, :, None, :]
    sin = jnp.sin(ang)[:, :, None, :]
    cos = jnp.broadcast_to(cos, (SEQ, 2, 2, n_f)).reshape(SEQ, dim)
    sin = jnp.concatenate([-sin, sin], axis=2).reshape(SEQ, dim)
    cos = jnp.tile(cos, (1, reps))
    sin = jnp.tile(sin, (1, reps))
    cos = jnp.concatenate([jnp.ones((CTX, dim * reps), F32), cos], axis=0)
    sin = jnp.concatenate([jnp.zeros((CTX, dim * reps), F32), sin], axis=0)
    return cos, sin


def _rope(x, cos, sin_signed, n_f):
    lane = lax.broadcasted_iota(jnp.int32, x.shape, x.ndim - 1)
    partner = jnp.where(lane % (2 * n_f) < n_f,
                        pltpu.roll(x, HEAD_DIM - n_f, axis=x.ndim - 1),
                        pltpu.roll(x, n_f, axis=x.ndim - 1))
    return x * cos + partner * sin_signed


_NA_GROUP_ROWS = RB // GRID_W
_NA_KEY_ROWS = 3 * _NA_GROUP_ROWS
_NA_PATTERN = (0, 1, 2, 2, 2, 2, 2, 3, 4)


def _na_key_block(g):
    return jnp.clip(g - 1, 0, GRID_H // _NA_GROUP_ROWS - 3)


def _na_bias_tables(rpb):
    c = np.arange(GRID_W)
    cs = np.clip(c - NA_WIN_W // 2, 0, GRID_W - NA_WIN_W)
    col_valid = (c[None, :] >= cs[:, None]) & (c[None, :] < cs[:, None] + NA_WIN_W)
    co = np.clip(c[None, :] - c[:, None] + NA_WIN_W - 1, 0, 2 * NA_WIN_W - 2)
    onehot = jnp.asarray((co[None] == np.arange(2 * NA_WIN_W - 1)[:, None, None]) & col_valid[None], F32)
    tcol = jnp.einsum("hro,oqk->hrqk", rpb.astype(F32), onehot, precision=lax.Precision.HIGHEST)
    tcol = jnp.where(jnp.asarray(col_valid)[None, None], tcol, NEG)
    neg_blk = jnp.full((NH, GRID_W, GRID_W), NEG, F32)
    tabs = []
    for g in (0, 1, 2, 7):
        r0 = _NA_GROUP_ROWS * g
        ks = int(np.clip(g - 1, 0, GRID_H // _NA_GROUP_ROWS - 3)) * _NA_GROUP_ROWS
        rows = []
        for dr in range(_NA_GROUP_ROWS):
            r = r0 + dr
            rs = int(np.clip(r - NA_WIN_H // 2, 0, GRID_H - NA_WIN_H))
            blks = []
            for krel in range(_NA_KEY_ROWS):
                kr = ks + krel
                blks.append(tcol[:, kr - r + NA_WIN_H - 1] if rs <= kr < rs + NA_WIN_H else neg_blk)
            rows.append(jnp.concatenate(blks, axis=-1))
        tabs.append(jnp.concatenate(rows, axis=-2))
    tabs.append(jnp.full((NH, RB, 3 * RB), NEG, F32))
    return jnp.stack(tabs, axis=1)


_NA_HPB = 2


def _na_kernel(q_ref, k0_ref, k1_ref, k2_ref, v0_ref, v1_ref, v2_ref, kc_ref, vc_ref, b_ref, o_ref):
    for hh in range(_NA_HPB):
        ln = slice(hh * HEAD_DIM, (hh + 1) * HEAD_DIM)
        q = (q_ref[:, ln].astype(F32) * HEAD_DIM ** -0.5).astype(BF16)
        s = [_dot_nt(q, kr[:, ln]) + b_ref[hh, 0, :, t * RB:(t + 1) * RB]
             for t, kr in enumerate((k0_ref, k1_ref, k2_ref))]
        s.append(_dot_nt(q, kc_ref[:, ln]))
        m = functools.reduce(jnp.maximum, [jnp.max(t, axis=-1, keepdims=True) for t in s])
        p = [jnp.exp(t - m) for t in s]
        l = functools.reduce(jnp.add, [jnp.sum(t, axis=-1, keepdims=True) for t in p])
        vs = (v0_ref, v1_ref, v2_ref, vc_ref)
        o = functools.reduce(jnp.add, [_dot(pt.astype(BF16), vr[:, ln]) for pt, vr in zip(p, vs)])
        o_ref[:, ln] = (o / l).astype(o_ref.dtype)


def _na_attention(proj, bias):
    hb = NH // _NA_HPB

    def qrow(h, g, b):
        return b * BLOCKS_PER_BATCH + jnp.where(g < 8, 1 + g, 0)

    def krow(t):
        return lambda h, g, b: (b * BLOCKS_PER_BATCH + 1 + _na_key_block(jnp.minimum(g, 7)) + t, hb + h)

    def vrow(t):
        return lambda h, g, b: (b * BLOCKS_PER_BATCH + 1 + _na_key_block(jnp.minimum(g, 7)) + t, 2 * hb + h)

    blk = (RB, _NA_HPB * HEAD_DIM)
    pattern = jnp.asarray(_NA_PATTERN, jnp.int32)

    def bias_map(h, g, b, pat_ref):
        return (h, pat_ref[g], 0, 0)

    def wrap(f):
        return lambda h, g, b, pat_ref: f(h, g, b)

    in_specs = [pl.BlockSpec(blk, wrap(lambda h, g, b: (qrow(h, g, b), h)))]
    in_specs += [pl.BlockSpec(blk, wrap(krow(t))) for t in range(3)]
    in_specs += [pl.BlockSpec(blk, wrap(vrow(t))) for t in range(3)]
    in_specs += [pl.BlockSpec(blk, wrap(lambda h, g, b: (b * BLOCKS_PER_BATCH, hb + h))),
                 pl.BlockSpec(blk, wrap(lambda h, g, b: (b * BLOCKS_PER_BATCH, 2 * hb + h))),
                 pl.BlockSpec((_NA_HPB, 1, RB, 3 * RB), bias_map)]

    def body(pat_ref, *refs):
        _na_kernel(*refs)

    return pl.pallas_call(
        body,
        grid_spec=pltpu.PrefetchScalarGridSpec(
            num_scalar_prefetch=1,
            grid=(hb, BLOCKS_PER_BATCH, BATCH),
            in_specs=in_specs,
            out_specs=pl.BlockSpec(blk, wrap(lambda h, g, b: (qrow(h, g, b), h))),
        ),
        out_shape=jax.ShapeDtypeStruct((M, NH * HEAD_DIM), BF16),
        compiler_params=_cparams(("arbitrary", "arbitrary", "arbitrary"), 32),
        name="na_attention",
    )(pattern, *([proj] * 9), bias)


_RET_NC = TB // RET_CHUNK
_RET_CTX_NC = CTX // RET_CHUNK


def _ret_kernel(lg_ref, q_ref, k_ref, v_ref, g_ref, cos_ref, sin_ref, o_ref, qr_ref, kr_ref, sf_ref, sb_ref):
    h = pl.program_id(1)
    lgf = lg_ref[0, h]
    lgb = lg_ref[1, h]
    c = RET_CHUNK
    qr_ref[...] = _rope(q_ref[...].astype(F32), cos_ref[...], sin_ref[...], HEAD_DIM // 4)
    kr_ref[...] = _rope(k_ref[...].astype(F32), cos_ref[...], sin_ref[...], HEAD_DIM // 4) * HEAD_DIM ** -0.5

    pos = lax.broadcasted_iota(jnp.int32, (c, 1), 0).astype(F32)
    kdec_f = jnp.exp(lgf * (c - 1.0 - pos))
    kdec_b = jnp.exp(lgb * pos)
    qdec_f = jnp.exp(lgf * (pos + 1.0))
    qdec_b = jnp.exp(lgb * (c - pos))
    gc_f = jnp.exp(lgf * c)
    gc_b = jnp.exp(lgb * c)
    rel = (lax.broadcasted_iota(jnp.int32, (c, c), 0) - lax.broadcasted_iota(jnp.int32, (c, c), 1)).astype(F32)
    dmask = (jnp.where(rel >= 0, jnp.exp(jnp.maximum(rel, 0.0) * lgf), 0.0)
             + jnp.where(rel <= 0, jnp.exp(jnp.maximum(-rel, 0.0) * lgb), 0.0))

    def kv_step(n, carry):
        rows = pl.ds(pl.multiple_of(n * c, c), c)
        kn = kr_ref[rows, :]
        vn = v_ref[rows, :]
        sf_ref[n] = _dot_tn((kn * kdec_f).astype(BF16), vn)
        sb_ref[n] = _dot_tn((kn * kdec_b).astype(BF16), vn)
        return carry

    lax.fori_loop(0, _RET_NC, kv_step, 0, unroll=2)

    def state_step(n, s, s_ref, gc):
        u = s_ref[n]
        s_ref[n] = s
        return s * gc + u

    zero = jnp.zeros((HEAD_DIM, RET_DV), F32)
    lax.fori_loop(0, _RET_NC, lambda n, s: state_step(n, s, sf_ref, gc_f), zero)
    s = lax.fori_loop(0, _RET_CTX_NC, lambda i, s: state_step(_RET_CTX_NC - 1 - i, s, sb_ref, gc_b), zero)
    lax.fori_loop(0, _RET_NC - _RET_CTX_NC, lambda i, s: state_step(_RET_NC - 1 - i, s, sb_ref, gc_b), s)

    def out_step(n, carry):
        rows = pl.ds(pl.multiple_of(n * c, c), c)
        qn = qr_ref[rows, :]
        vn = v_ref[rows, :]
        scores = _dot_nt(qn.astype(BF16), kr_ref[rows, :].astype(BF16)) * dmask
        intra = _dot(scores.astype(BF16), vn)
        qcat = jnp.concatenate([qn * qdec_f, qn * qdec_b], axis=1).astype(BF16)
        scat = jnp.concatenate([sf_ref[n], sb_ref[n]], axis=0).astype(BF16)
        o = intra + _dot(qcat, scat)
        o = o * lax.rsqrt(jnp.mean(o * o, axis=-1, keepdims=True) + EPS)
        o_ref[rows, :] = (o * _silu(g_ref[rows, :].astype(F32))).astype(o_ref.dtype)
        return carry

    lax.fori_loop(0, _RET_NC, out_step, 0, unroll=2)


def _retention(proj, log_g, cos, sin):
    seq128 = lambda blk0: pl.BlockSpec((TB, HEAD_DIM), lambda b, h, lg: (b, blk0 + h))
    seq256 = lambda blk0: pl.BlockSpec((TB, RET_DV), lambda b, h, lg: (b, blk0 + h))
    tab = pl.BlockSpec((TB, HEAD_DIM), lambda b, h, lg: (0, 0))
    return pl.pallas_call(
        _ret_kernel,
        grid_spec=pltpu.PrefetchScalarGridSpec(
            num_scalar_prefetch=1,
            grid=(BATCH, NH),
            in_specs=[seq128(3072 // HEAD_DIM), seq128(4096 // HEAD_DIM),
                      seq256(5120 // RET_DV), seq256(7168 // RET_DV), tab, tab],
            out_specs=pl.BlockSpec((TB, RET_DV), lambda b, h, lg: (b, h)),
            scratch_shapes=[pltpu.VMEM((TB, HEAD_DIM), F32), pltpu.VMEM((TB, HEAD_DIM), F32),
                            pltpu.VMEM((_RET_NC, HEAD_DIM, RET_DV), F32),
                            pltpu.VMEM((_RET_NC, HEAD_DIM, RET_DV), F32)],
        ),
        out_shape=jax.ShapeDtypeStruct((M, NH * RET_DV), BF16),
        compiler_params=_cparams(("arbitrary", "arbitrary"), 48),
        name="retention",
    )(log_g, proj, proj, proj, proj, cos, sin)


def _diff_kernel(lam_ref, q_ref, k_ref, v_ref, cq_ref, sq_ref, ck_ref, sk_ref, o_ref, kb_ref, vb_ref, *, post_scale):
    qi = pl.program_id(2)

    @pl.when(qi == 0)
    def _():
        kb_ref[...] = _rope(k_ref[...].astype(F32), ck_ref[...], sk_ref[...], DIFF_DK // 4).astype(BF16)
        vb_ref[...] = v_ref[...].astype(BF16)
        o_ref[...] = jnp.zeros_like(o_ref)

    @pl.when(qi > 0)
    def _():
        lam = lam_ref[0]
        q = _rope(q_ref[...].astype(F32), cq_ref[...], sq_ref[...], DIFF_DK // 4) * (DIFF_DK ** -0.5 * LOG2E)
        lane = lax.broadcasted_iota(jnp.int32, q.shape, 1)
        kb = kb_ref[...]
        p, inv = [], []
        for t in range(2):
            qt = jnp.where((lane >= DIFF_DK) == (t == 1), q, 0.0).astype(BF16)
            s = _dot_nt(qt, kb)
            e = jnp.exp2(s - jnp.max(s, axis=-1, keepdims=True))
            p.append(e)
            inv.append(1.0 / jnp.sum(e, axis=-1, keepdims=True))
        a = (p[0] * inv[0] - p[1] * (lam * inv[1])).astype(BF16)
        o = _dot(a, vb_ref[...])
        o = o * lax.rsqrt(jnp.mean(o * o, axis=-1, keepdims=True) + EPS)
        o_ref[...] = (o * post_scale).astype(o_ref.dtype)


def _diff_attention(proj, lam, cos, sin, post_scale):
    blk = (RB, HEAD_DIM)
    seq = (TB, HEAD_DIM)
    qmap = lambda b, h, qi, lam_ref: (b * BLOCKS_PER_BATCH + qi, h)
    return pl.pallas_call(
        functools.partial(_diff_kernel, post_scale=post_scale),
        grid_spec=pltpu.PrefetchScalarGridSpec(
            num_scalar_prefetch=1,
            grid=(BATCH, NH, BLOCKS_PER_BATCH),
            in_specs=[pl.BlockSpec(blk, qmap),
                      pl.BlockSpec(seq, lambda b, h, qi, lam_ref: (b, NH + h)),
                      pl.BlockSpec(seq, lambda b, h, qi, lam_ref: (b, 2 * NH + h)),
                      pl.BlockSpec(blk, lambda b, h, qi, lam_ref: (qi, 0)),
                      pl.BlockSpec(blk, lambda b, h, qi, lam_ref: (qi, 0)),
                      pl.BlockSpec(seq, lambda b, h, qi, lam_ref: (0, 0)),
                      pl.BlockSpec(seq, lambda b, h, qi, lam_ref: (0, 0))],
            out_specs=pl.BlockSpec(blk, qmap),
            scratch_shapes=[pltpu.VMEM(seq, BF16), pltpu.VMEM(seq, BF16)],
        ),
        out_shape=jax.ShapeDtypeStruct((M, NH * HEAD_DIM), BF16),
        compiler_params=_cparams(("arbitrary", "arbitrary", "arbitrary"), 48),
        name="diff_attention",
    )(lam, proj, proj, proj, cos, sin, cos, sin)


_HG_NC = TB // HG_CHUNK
_HG_CTX_NC = CTX // HG_CHUNK


def _chunk_cumsum(x, reverse):
    c = x.shape[0]
    row = lax.broadcasted_iota(jnp.int32, x.shape, 0)
    s = 1
    while s < c:
        if reverse:
            x = x + jnp.where(row < c - s, pltpu.roll(x, c - s, axis=0), 0.0)
        else:
            x = x + jnp.where(row >= s, pltpu.roll(x, s, axis=0), 0.0)
        s *= 2
    return x


def _hg_kernel(q_ref, ff_ref, fb_ref, v_ref, g_ref, lb_ref, o_ref,
               qs_ref, kf_ref, kb_ref, cf_ref, cb_ref, sf_ref, sb_ref, df_ref, db_ref):
    c = HG_CHUNK
    lb = lb_ref[0]
    qraw = q_ref[...].astype(F32)
    qs_ref[...] = qraw * _sigmoid(qraw)

    def gate(raw_ref, k_ref, lf_ref):
        f = lb + (1.0 - lb) * _sigmoid(raw_ref[...].astype(F32))
        k_ref[...] = 1.0 - f
        lf_ref[...] = jnp.log(f)

    gate(ff_ref, kf_ref, cf_ref)
    gate(fb_ref, kb_ref, cb_ref)

    def chunk_prep(n, carry):
        rows = pl.ds(pl.multiple_of(n * c, c), c)
        vn = v_ref[rows, :]
        for cum_ref, k_ref, u_ref, d_ref, reverse in ((cf_ref, kf_ref, sf_ref, df_ref, False),
                                                      (cb_ref, kb_ref, sb_ref, db_ref, True)):
            cum = _chunk_cumsum(cum_ref[rows, :], reverse)
            cum_ref[rows, :] = cum
            last = cum[0:1, :] if reverse else cum[c - 1:c, :]
            u_ref[n] = _dot_tn(vn, (k_ref[rows, :] * jnp.exp(last - cum)).astype(BF16))
            d_ref[n] = jnp.exp(last)
        return carry

    lax.fori_loop(0, _HG_NC, chunk_prep, 0, unroll=2)

    def state_step(n, st, s_ref, d_ref):
        u = s_ref[n]
        s_ref[n] = st
        return st * d_ref[n] + u

    zero = jnp.zeros((HEAD_DIM, HEAD_DIM), F32)
    lax.fori_loop(0, _HG_NC, lambda n, s: state_step(n, s, sf_ref, df_ref), zero)
    s = lax.fori_loop(0, _HG_CTX_NC, lambda i, s: state_step(_HG_CTX_NC - 1 - i, s, sb_ref, db_ref), zero)
    lax.fori_loop(0, _HG_NC - _HG_CTX_NC, lambda i, s: state_step(_HG_NC - 1 - i, s, sb_ref, db_ref), s)

    o_ref[0:CTX, :] = jnp.zeros((CTX, HEAD_DIM), o_ref.dtype)

    def level(q, kf, kb, cf, cb, half):
        span = 2 * half
        row = lax.broadcasted_iota(jnp.int32, (c, 1), 0)
        upper = (row % span) >= half
        ref_f = jnp.concatenate([jnp.broadcast_to(cf[b0 + half - 1:b0 + half, :], (span, HEAD_DIM))
                                 for b0 in range(0, c, span)], axis=0)
        ref_b = jnp.concatenate([jnp.broadcast_to(cb[b0 + half:b0 + half + 1, :], (span, HEAD_DIM))
                                 for b0 in range(0, c, span)], axis=0)
        ef = jnp.exp(jnp.minimum(jnp.where(upper, cf - ref_f, ref_f - cf), 0.0))
        eb = jnp.exp(jnp.minimum(jnp.where(upper, ref_b - cb, cb - ref_b), 0.0))
        qq = jnp.concatenate([jnp.where(upper, q * ef, 0.0), jnp.where(upper, 0.0, q * eb)], axis=1)
        kk = jnp.concatenate([jnp.where(upper, 0.0, kf * ef), jnp.where(upper, kb * eb, 0.0)], axis=1)
        a = _dot_nt(qq.astype(BF16), kk.astype(BF16))
        if span == c:
            return a
        ri = lax.broadcasted_iota(jnp.int32, (c, c), 0)
        ci = lax.broadcasted_iota(jnp.int32, (c, c), 1)
        return jnp.where(ri // span == ci // span, a, 0.0)

    def sub_block(q, kf, kb, lo):
        sub = HG_SUB
        row = lax.broadcasted_iota(jnp.int32, (sub, 1), 0)
        lane = lax.broadcasted_iota(jnp.int32, (sub, c), 1)
        qb, kfb, kbb = q[lo:lo + sub], kf[lo:lo + sub], kb[lo:lo + sub]
        ffb, fbb = 1.0 - kfb, 1.0 - kbb
        pf = [None] * sub
        p = jnp.where(row == sub - 1, 1.0, jnp.zeros((sub, HEAD_DIM), F32))
        pf[sub - 1] = p
        for j in range(sub - 2, -1, -1):
            p = jnp.where(row == j, 1.0, p * ffb[j + 1:j + 2])
            pf[j] = p
        att = jnp.zeros((sub, c), F32)
        p = None
        for j in range(sub):
            p = (jnp.where(row == 0, 1.0, jnp.zeros((sub, HEAD_DIM), F32)) if j == 0
                 else jnp.where(row == j, 1.0, p * fbb[j - 1:j]))
            col = jnp.sum(qb * (pf[j] * kfb[j:j + 1] + p * kbb[j:j + 1]), axis=-1, keepdims=True)
            att = jnp.where(lane == lo + j, col, att)
        return att

    def out_step(n, carry):
        rows = pl.ds(pl.multiple_of(n * c, c), c)
        q = qs_ref[rows, :]
        kf = kf_ref[rows, :]
        kb = kb_ref[rows, :]
        cf = cf_ref[rows, :]
        cb = cb_ref[rows, :]
        qcat = jnp.concatenate([q * jnp.exp(cf), q * jnp.exp(cb)], axis=1).astype(BF16)
        scat = jnp.concatenate([sf_ref[n], sb_ref[n]], axis=1).astype(BF16)
        o = _dot_nt(qcat, scat)
        att = jnp.concatenate([sub_block(q, kf, kb, lo) for lo in range(0, c, HG_SUB)], axis=0)
        half = c // 2
        while half >= HG_SUB:
            att = att + level(q, kf, kb, cf, cb, half)
            half //= 2
        o = o + _dot(att.astype(BF16), v_ref[rows, :])
        o = o * lax.rsqrt(jnp.mean(o * o, axis=-1, keepdims=True) + EPS)
        graw = g_ref[rows, :].astype(F32)
        o_ref[rows, :] = (o * (graw * _sigmoid(graw))).astype(o_ref.dtype)
        return carry

    lax.fori_loop(_HG_CTX_NC, _HG_NC, out_step, 0, unroll=2)


def _hgrn(proj, lb):
    seq = lambda col0: pl.BlockSpec((TB, HEAD_DIM), lambda b, h: (b, col0 // HEAD_DIM + h))
    full = pltpu.VMEM((TB, HEAD_DIM), F32)
    states = pltpu.VMEM((_HG_NC, HEAD_DIM, HEAD_DIM), F32)
    decays = pltpu.VMEM((_HG_NC, 1, HEAD_DIM), F32)
    return pl.pallas_call(
        _hg_kernel,
        grid=(BATCH, NH),
        in_specs=[seq(3072), seq(4096), seq(5120), seq(6144), seq(7168),
                  pl.BlockSpec((1, 1, HEAD_DIM), lambda b, h: (h, 0, 0))],
        out_specs=pl.BlockSpec((TB, HEAD_DIM), lambda b, h: (b, h)),
        out_shape=jax.ShapeDtypeStruct((M, NH * HEAD_DIM), BF16),
        scratch_shapes=[full, full, full, full, full, states, states, decays, decays],
        compiler_params=_cparams(("arbitrary", "arbitrary"), 48),
        name="hgrn",
    )(proj, proj, proj, proj, proj, lb.reshape(NH, 1, HEAD_DIM))


def _row_gather_start(idx_ref, k, src_hbm, dst, sem, n):
    def body(r, carry):
        pltpu.make_async_copy(src_hbm.at[pl.ds(idx_ref[0, k, r], 1)], dst.at[pl.ds(r, 1)], sem).start()
        return carry
    lax.fori_loop(0, n, body, 0, unroll=GATHER_UNROLL)


def _row_gather_wait(src_hbm, dst, sem, n):
    pltpu.make_async_copy(src_hbm.at[pl.ds(0, n)], dst, sem).wait()


def _moe_up_kernel(be_ref, nu_ref, rt0_ref, rt1_ref, x_hbm, w1_ref, w3_ref, o_ref, xbuf, sem, w1b_ref, w3b_ref):
    i = pl.program_id(0)
    nu = nu_ref[0]

    @pl.when(jnp.logical_and(i == 0, nu > 0))
    def _():
        _row_gather_start(rt0_ref, 0, x_hbm, xbuf.at[0], sem.at[0], MOE_TM)

    @pl.when(i + 1 < nu)
    def _():
        slot = (i + 1) % 2
        _row_gather_start(rt1_ref, 0, x_hbm, xbuf.at[slot], sem.at[slot], MOE_TM)

    changed = jnp.logical_or(i == 0, be_ref[i] != be_ref[jnp.maximum(i - 1, 0)])

    @pl.when(changed)
    def _():
        w1b_ref[...] = w1_ref[0, 0].astype(BF16)
        w3b_ref[...] = w3_ref[0, 0].astype(BF16)

    @pl.when(i < nu)
    def _():
        slot = i % 2
        _row_gather_wait(x_hbm, xbuf.at[slot], sem.at[slot], MOE_TM)
        x = xbuf[slot].astype(BF16)
        a = _dot(x, w1b_ref[...])
        o_ref[...] = (_silu(a) * _dot(x, w3b_ref[...])).astype(o_ref.dtype)

    @pl.when(i >= nu)
    def _():
        o_ref[...] = jnp.zeros_like(o_ref)


def _moe_up(layer, blk_e, n_used, row_tok, h2, w1, w3):
    wspec = pl.BlockSpec((1, 1, D, EXPERT_FF), lambda i, be, nu: (layer, be[i], 0, 0))
    rt = row_tok.reshape(MOE_BLOCKS, 1, MOE_TM)
    smem_blk = lambda f: pl.BlockSpec((1, 1, MOE_TM), f, memory_space=pltpu.SMEM)
    return pl.pallas_call(
        _moe_up_kernel,
        grid_spec=pltpu.PrefetchScalarGridSpec(
            num_scalar_prefetch=2,
            grid=(MOE_BLOCKS,),
            in_specs=[smem_blk(lambda i, be, nu: (i, 0, 0)),
                      smem_blk(lambda i, be, nu: (jnp.minimum(i + 1, MOE_BLOCKS - 1), 0, 0)),
                      pl.BlockSpec(memory_space=pl.ANY), wspec, wspec],
            out_specs=pl.BlockSpec((MOE_TM, EXPERT_FF), lambda i, be, nu: (i, 0)),
            scratch_shapes=[pltpu.VMEM((2, MOE_TM, D), F32), pltpu.SemaphoreType.DMA((2,)),
                            pltpu.VMEM((D, EXPERT_FF), BF16), pltpu.VMEM((D, EXPERT_FF), BF16)],
        ),
        out_shape=jax.ShapeDtypeStruct((MOE_ROWS, EXPERT_FF), BF16),
        compiler_params=_cparams(("arbitrary",), 52),
        name="moe_up",
    )(blk_e, n_used, rt, rt, h2, w1, w3)


def _moe_down_kernel(be_ref, nu_ref, h_ref, w2_ref, o_ref, w2b_ref):
    i = pl.program_id(0)
    changed = jnp.logical_or(i == 0, be_ref[i] != be_ref[jnp.maximum(i - 1, 0)])

    @pl.when(changed)
    def _():
        w2b_ref[...] = w2_ref[0, 0].astype(BF16)

    @pl.when(i < nu_ref[0])
    def _():
        o_ref[...] = _dot(h_ref[...], w2b_ref[...])

    @pl.when(i >= nu_ref[0])
    def _():
        o_ref[...] = jnp.zeros_like(o_ref)


def _moe_down(layer, blk_e, n_used, hmid, w2):
    return pl.pallas_call(
        _moe_down_kernel,
        grid_spec=pltpu.PrefetchScalarGridSpec(
            num_scalar_prefetch=2,
            grid=(MOE_BLOCKS,),
            in_specs=[pl.BlockSpec((MOE_TM, EXPERT_FF), lambda i, be, nu: (i, 0)),
                      pl.BlockSpec((1, 1, EXPERT_FF, D), lambda i, be, nu: (layer, be[i], 0, 0))],
            out_specs=pl.BlockSpec((MOE_TM, D), lambda i, be, nu: (i, 0)),
            scratch_shapes=[pltpu.VMEM((EXPERT_FF, D), BF16)],
        ),
        out_shape=jax.ShapeDtypeStruct((MOE_ROWS, D), F32),
        compiler_params=_cparams(("arbitrary",), 40),
        name="moe_down",
    )(blk_e, n_used, hmid, w2)


def _combine_kernel(p0_ref, p1_ref, y_hbm, x_ref, w_ref, g_ref, o_ref, ybuf, sem):
    i = pl.program_id(0)
    n = pl.num_programs(0)

    def start(p_ref, slot):
        for k in range(TOP_K):
            _row_gather_start(p_ref, k, y_hbm, ybuf.at[slot, k], sem.at[slot], RB)

    @pl.when(i == 0)
    def _():
        start(p0_ref, 0)

    @pl.when(i + 1 < n)
    def _():
        start(p1_ref, (i + 1) % 2)

    slot = i % 2
    for k in range(TOP_K):
        _row_gather_wait(y_hbm, ybuf.at[slot, k], sem.at[slot], RB)
    w = w_ref[...]
    y = w[:, 0:1] * ybuf[slot, 0] + w[:, 1:2] * ybuf[slot, 1]
    o_ref[...] = x_ref[...] + g_ref[0] * y


def _moe_combine(tok, y, pos, gates, gate_mod):
    nb = M // RB
    pblk = pos.reshape(nb, RB, TOP_K).transpose(0, 2, 1)
    row = pl.BlockSpec((RB, D), lambda i: (i, 0))
    smem_blk = lambda f: pl.BlockSpec((1, TOP_K, RB), f, memory_space=pltpu.SMEM)
    return pl.pallas_call(
        _combine_kernel,
        grid=(nb,),
        in_specs=[smem_blk(lambda i: (i, 0, 0)), smem_blk(lambda i: (jnp.minimum(i + 1, nb - 1), 0, 0)),
                  pl.BlockSpec(memory_space=pl.ANY), row,
                  pl.BlockSpec((RB, TOP_K), lambda i: (i, 0)), pl.BlockSpec((1, 1, D), _mod_index)],
        out_specs=row,
        out_shape=jax.ShapeDtypeStruct((M, D), F32),
        scratch_shapes=[pltpu.VMEM((2, TOP_K, RB, D), F32), pltpu.SemaphoreType.DMA((2,))],
        compiler_params=_cparams(("arbitrary",), 40),
        name="moe_combine",
    )(pblk, pblk, y, tok, gates, gate_mod)


def _route(logits, valid):
    lg_g = logits[:, :N_GROUPS]
    p_g = jax.nn.softmax(lg_g, axis=-1)
    g_top = jnp.argmax(lg_g, axis=-1)
    w_grp = jnp.take_along_axis(p_g, g_top[:, None], axis=1)
    lg_e = logits[:, N_GROUPS:N_GROUPS + N_EXPERTS].reshape(M, N_GROUPS, EXPERTS_PER_GROUP)
    lg_in = jnp.take_along_axis(lg_e, g_top[:, None, None], axis=1)[:, 0]
    top_v, top_i = lax.top_k(lg_in, TOP_K)
    gates = w_grp * jax.nn.softmax(top_v, axis=-1)
    gates = jnp.where(valid[:, None], gates, 0.0)
    eid = g_top[:, None] * EXPERTS_PER_GROUP + top_i
    eid = jnp.where(valid[:, None], eid, N_EXPERTS).reshape(-1).astype(jnp.int32)
    n_slots = M * TOP_K
    slot = jnp.arange(n_slots, dtype=jnp.int32)
    experts = jnp.arange(N_EXPERTS + 1, dtype=jnp.int32)
    counts = jnp.sum((eid[:, None] == experts[None, :]).astype(jnp.int32), axis=0)
    padded = (counts + MOE_TM - 1) // MOE_TM * MOE_TM
    p_end = jnp.cumsum(padded)
    p_start = p_end - padded
    u_start = jnp.cumsum(counts) - counts
    shift = p_start - u_start
    s_e, order = lax.sort((eid, slot), num_keys=1, is_stable=True)
    dest = slot + jnp.sum(jnp.where(s_e[:, None] == experts[None, :], shift[None, :], 0), axis=1)
    dest = jnp.where(s_e < N_EXPERTS, dest, 0)
    _, pos = lax.sort((order, dest), num_keys=1)
    pos = pos.reshape(M, TOP_K)
    blk_start = jnp.arange(MOE_BLOCKS, dtype=jnp.int32) * MOE_TM
    blk_e = jnp.sum((p_end[None, :N_EXPERTS] <= blk_start[:, None]).astype(jnp.int32), axis=1)
    blk_e = jnp.minimum(blk_e, N_EXPERTS - 1)
    onehot_be = blk_e[:, None] == experts[None, :N_EXPERTS]
    blk_shift = jnp.sum(jnp.where(onehot_be, shift[None, :N_EXPERTS], 0), axis=1)
    blk_lim = jnp.sum(jnp.where(onehot_be, (p_start + counts)[None, :N_EXPERTS], 0), axis=1)
    rows = jnp.arange(MOE_ROWS, dtype=jnp.int32).reshape(MOE_BLOCKS, MOE_TM)
    src = jnp.clip(rows - blk_shift[:, None], 0, n_slots - 1)
    row_tok = jnp.where(rows < blk_lim[:, None], jnp.take(order // TOP_K, src), 0)
    n_used = (p_end[N_EXPERTS - 1] // MOE_TM).astype(jnp.int32).reshape(1)
    return gates, row_tok, pos, blk_e, n_used


def _moe(layer, tok, mul, add, gate_mod, wg, bg, we, be, w1, w3, w2, valid):
    wr = jnp.zeros((D, ROUTER_PAD), F32).at[:, :N_GROUPS].set(wg).at[:, N_GROUPS:N_GROUPS + N_EXPERTS].set(we)
    w_hi = wr.astype(BF16)
    w_lo = (wr - w_hi.astype(F32)).astype(BF16)
    r_bias = jnp.zeros((1, ROUTER_PAD), F32).at[0, :N_GROUPS].set(bg).at[0, N_GROUPS:N_GROUPS + N_EXPERTS].set(be)
    h2, logits = _norm_router(tok, mul, add, w_hi, w_lo, r_bias)
    gates, row_tok, pos, blk_e, n_used = _route(logits, valid)
    hmid = _moe_up(layer, blk_e, n_used, row_tok, h2, w1, w3)
    y = _moe_down(layer, blk_e, n_used, hmid, w2)
    return _moe_combine(tok, y, pos, gates, gate_mod)


def _final_kernel(x_ref, w_ref, o_ref):
    x = x_ref[...]
    o_ref[0] = x * lax.rsqrt(jnp.mean(x * x, axis=-1, keepdims=True) + EPS) * w_ref[...]


def _final_norm(tok, w):
    nb = SEQ // RB
    return pl.pallas_call(
        _final_kernel,
        grid=(BATCH, nb),
        in_specs=[pl.BlockSpec((RB, D), lambda b, g: (b * BLOCKS_PER_BATCH + 1 + g, 0)),
                  pl.BlockSpec((1, D), lambda b, g: (0, 0))],
        out_specs=pl.BlockSpec((1, RB, D), lambda b, g: (b, g, 0)),
        out_shape=jax.ShapeDtypeStruct((BATCH, SEQ, D), F32),
        compiler_params=_cparams(("arbitrary", "arbitrary"), 32),
        name="final_norm",
    )(tok, w.reshape(1, D))


def _mod9(t):
    return t[:BATCH + 1].reshape(BATCH + 1, 1, D)


def kernel(x, c, ctx, c_ctx, ada_w, ada_b, norm1_w, norm2_w, w_in_even, w_out_even, na_rpb, ret_decay,
           w_in_odd, w_out_odd, diff_lambda, hg_lb_logits, router_g_w, router_g_b, router_e_w, router_e_b,
           moe_w1, moe_w3, moe_w2, norm_f_w):
    depth = ada_w.shape[0]
    tok = jnp.concatenate([ctx, x], axis=1).reshape(M, D)
    c16 = jnp.concatenate([c, c_ctx[None], jnp.zeros((16 - BATCH - 1, D), F32)], axis=0)
    mod = _ada_mod(c16, ada_w, ada_b)
    lb_all = jnp.cumsum(jax.nn.softmax(hg_lb_logits.astype(F32), axis=0), axis=0)
    lb_all = lb_all - lb_all[0]
    is_latent = (jnp.arange(M) % TB) >= CTX
    all_rows = jnp.ones((M,), bool)

    for l in range(depth):
        with_ctx = l < depth - 1
        sh1, s1, g1, sh2, s2, g2 = [_mod9(t) for t in jnp.split(mod[l], 6, axis=-1)]
        h = _norm_mod(tok, norm1_w[l] * (1.0 + s1), sh1)
        j = l // 2
        if l % 2 == 0:
            proj = _matmul(h, w_in_even[j].astype(BF16), BF16)
            a = _na_attention(proj, _na_bias_tables(na_rpb[j]))
            cos, sin = _rope_tables(HEAD_DIM, 1)
            r = _retention(proj, -jnp.exp(ret_decay[j].astype(F32)), cos, sin)
            w_out = w_out_even[j].astype(BF16)
            tok = _out_proj(a, r, w_out[:NH * HEAD_DIM], w_out[NH * HEAD_DIM:], tok, g1)
        else:
            proj = _matmul(h, w_in_odd[j].astype(BF16), BF16)
            lp = diff_lambda[j].astype(F32)
            lam_init = 0.8 - 0.6 * math.exp(-0.3 * l)
            lam = jnp.exp(jnp.sum(lp[0] * lp[1])) - jnp.exp(jnp.sum(lp[2] * lp[3])) + lam_init
            cos, sin = _rope_tables(DIFF_DK, 2)
            d_l = _diff_attention(proj, lam.reshape(1), cos, sin, 1.0 - lam_init)
            g_l = _hgrn(proj, lb_all[l])
            w_out = w_out_odd[j].astype(BF16)
            tok = _out_proj(d_l, g_l, w_out[:NH * HEAD_DIM], w_out[NH * HEAD_DIM:], tok, g1)
        tok = _moe(l, tok, norm2_w[l] * (1.0 + s2), sh2, g2,
                   router_g_w[l], router_g_b[l], router_e_w[l], router_e_b[l],
                   moe_w1, moe_w3, moe_w2, all_rows if with_ctx else is_latent)
    return _final_norm(tok, norm_f_w)
```

```python
import functools
import math

import numpy as np
import jax
import jax.numpy as jnp
from jax import lax
from jax.experimental import pallas as pl
from jax.experimental.pallas import tpu as pltpu

F32 = jnp.float32
BF16 = jnp.bfloat16

D = 2048
BATCH = 8
SEQ = 2048
CTX = 256
TB = SEQ + CTX
M = BATCH * TB
GRID_W = 64
GRID_H = SEQ // GRID_W
HEAD_DIM = 128
NH = 8
NA_WIN_H = 8
NA_WIN_W = 16
RET_DV = 256
RET_CHUNK = 128
DIFF_DK = 64
HG_CHUNK = 64
HG_SUB = 8
N_GROUPS = 4
EXPERTS_PER_GROUP = 8
N_EXPERTS = 32
TOP_K = 2
EXPERT_FF = 1024
ROPE_BASE = 10000.0
EPS = 1e-6
NEG = -1e30
LOG2E = math.log2(math.e)

RB = 256
BLOCKS_PER_BATCH = TB // RB
MOE_TM = 256
MOE_ROWS = M * TOP_K + N_EXPERTS * MOE_TM
MOE_BLOCKS = MOE_ROWS // MOE_TM
ROUTER_PAD = 128


def _cparams(sem, vmem_mb):
    return pltpu.CompilerParams(dimension_semantics=sem, vmem_limit_bytes=vmem_mb * 1024 * 1024)


def _sigmoid(x):
    return 0.5 * jnp.tanh(0.5 * x) + 0.5


def _silu(x):
    return x * _sigmoid(x)


_NT = (((1,), (1,)), ((), ()))
_TN = (((0,), (0,)), ((), ()))


def _dot(a, b):
    return jnp.dot(a, b, preferred_element_type=F32)


def _dot_nt(a, b):
    return lax.dot_general(a, b, _NT, preferred_element_type=F32)


def _dot_tn(a, b):
    return lax.dot_general(a, b, _TN, preferred_element_type=F32)


def _ada_kernel(c_ref, w_ref, b_ref, o_ref):
    a = _silu(c_ref[...]).astype(BF16)
    o_ref[0] = _dot(a, w_ref[0].astype(BF16)) + b_ref[0]


def _ada_mod(c16, ada_w, ada_b):
    depth, _, n = ada_w.shape
    tn = 1024
    return pl.pallas_call(
        _ada_kernel,
        grid=(depth, n // tn),
        in_specs=[
            pl.BlockSpec((16, D), lambda l, j: (0, 0)),
            pl.BlockSpec((1, D, tn), lambda l, j: (l, 0, j)),
            pl.BlockSpec((1, 1, tn), lambda l, j: (l, 0, j)),
        ],
        out_specs=pl.BlockSpec((1, 16, tn), lambda l, j: (l, 0, j)),
        out_shape=jax.ShapeDtypeStruct((depth, 16, n), F32),
        compiler_params=_cparams(("arbitrary", "arbitrary"), 40),
        name="ada_mod",
    )(c16, ada_w, ada_b.reshape(depth, 1, n))


def _mod_index(i):
    return (jnp.where(i % BLOCKS_PER_BATCH == 0, BATCH, i // BLOCKS_PER_BATCH), 0, 0)


def _norm_body(x_ref, mul_ref, add_ref):
    x = x_ref[...]
    y = x * lax.rsqrt(jnp.mean(x * x, axis=-1, keepdims=True) + EPS)
    return y * mul_ref[0] + add_ref[0]


def _norm_kernel(x_ref, mul_ref, add_ref, o_ref):
    o_ref[...] = _norm_body(x_ref, mul_ref, add_ref).astype(BF16)


def _norm_router_kernel(x_ref, mul_ref, add_ref, whi_ref, wlo_ref, rb_ref, o_ref, lg_ref):
    h = _norm_body(x_ref, mul_ref, add_ref)
    hi = h.astype(BF16)
    lo = (h - hi.astype(F32)).astype(BF16)
    o_ref[...] = h
    lg_ref[...] = (_dot(hi, whi_ref[...]) + _dot(hi, wlo_ref[...]) + _dot(lo, whi_ref[...])) + rb_ref[...]


def _norm_mod(tok, mul, add):
    spec_mod = pl.BlockSpec((1, 1, D), _mod_index)
    return pl.pallas_call(
        _norm_kernel,
        grid=(M // RB,),
        in_specs=[pl.BlockSpec((RB, D), lambda i: (i, 0)), spec_mod, spec_mod],
        out_specs=pl.BlockSpec((RB, D), lambda i: (i, 0)),
        out_shape=jax.ShapeDtypeStruct((M, D), BF16),
        compiler_params=_cparams(("arbitrary",), 32),
        name="norm_mod",
    )(tok, mul, add)


def _norm_router(tok, mul, add, w_hi, w_lo, r_bias):
    spec_mod = pl.BlockSpec((1, 1, D), _mod_index)
    spec_w = pl.BlockSpec((D, ROUTER_PAD), lambda i: (0, 0))
    return pl.pallas_call(
        _norm_router_kernel,
        grid=(M // RB,),
        in_specs=[pl.BlockSpec((RB, D), lambda i: (i, 0)), spec_mod, spec_mod, spec_w, spec_w,
                  pl.BlockSpec((1, ROUTER_PAD), lambda i: (0, 0))],
        out_specs=[pl.BlockSpec((RB, D), lambda i: (i, 0)), pl.BlockSpec((RB, ROUTER_PAD), lambda i: (i, 0))],
        out_shape=[jax.ShapeDtypeStruct((M, D), F32), jax.ShapeDtypeStruct((M, ROUTER_PAD), F32)],
        compiler_params=_cparams(("arbitrary",), 32),
        name="norm_router",
    )(tok, mul, add, w_hi, w_lo, r_bias)


def _matmul_kernel(a_ref, w_ref, o_ref):
    o_ref[...] = _dot(a_ref[...], w_ref[...]).astype(o_ref.dtype)


def _matmul(a, w, out_dtype):
    m, k = a.shape
    n = w.shape[1]
    tm, tn = TB, 512
    return pl.pallas_call(
        _matmul_kernel,
        grid=(m // tm, n // tn),
        in_specs=[pl.BlockSpec((tm, k), lambda i, j: (i, 0)), pl.BlockSpec((k, tn), lambda i, j: (0, j))],
        out_specs=pl.BlockSpec((tm, tn), lambda i, j: (i, j)),
        out_shape=jax.ShapeDtypeStruct((m, n), out_dtype),
        compiler_params=_cparams(("arbitrary", "arbitrary"), 48),
        name="proj_in",
    )(a, w)


def _out_proj_kernel(a1_ref, a2_ref, w1_ref, w2_ref, res_ref, gl_ref, gc_ref, o_ref, *, tm):
    acc = _dot(a1_ref[...], w1_ref[...]) + _dot(a2_ref[...], w2_ref[...])
    row = pl.program_id(0) * tm + lax.broadcasted_iota(jnp.int32, (tm, 1), 0)
    gate = jnp.where(row % TB < CTX, gc_ref[0], gl_ref[0])
    o_ref[...] = res_ref[...] + gate * acc


def _out_proj(a1, a2, w1, w2, tok, gate):
    tm, tn = TB // 2, 512
    k1, k2 = a1.shape[1], a2.shape[1]
    return pl.pallas_call(
        functools.partial(_out_proj_kernel, tm=tm),
        grid=(M // tm, D // tn),
        in_specs=[
            pl.BlockSpec((tm, k1), lambda i, j: (i, 0)),
            pl.BlockSpec((tm, k2), lambda i, j: (i, 0)),
            pl.BlockSpec((k1, tn), lambda i, j: (0, j)),
            pl.BlockSpec((k2, tn), lambda i, j: (0, j)),
            pl.BlockSpec((tm, tn), lambda i, j: (i, j)),
            pl.BlockSpec((1, 1, tn), lambda i, j: (i // 2, 0, j)),
            pl.BlockSpec((1, 1, tn), lambda i, j: (BATCH, 0, j)),
        ],
        out_specs=pl.BlockSpec((tm, tn), lambda i, j: (i, j)),
        out_shape=jax.ShapeDtypeStruct((M, D), F32),
        compiler_params=_cparams(("arbitrary", "arbitrary"), 48),
        name="proj_out",
    )(a1, a2, w1, w2, tok, gate, gate)


def _rope_tables(dim, reps):
    n_f = dim // 4
    inv = ROPE_BASE ** (-jnp.arange(n_f, dtype=F32) / n_f)
    t = jnp.arange(SEQ)
    pos = jnp.stack([t // GRID_W, t % GRID_W], axis=-1).astype(F32)
    ang = pos[:, :, None] * inv
    cos = jnp.cos(ang)[:, :, None, :]
    sin = jnp.sin(ang)[:, :, None, :]
    cos = jnp.broadcast_to(cos, (SEQ, 2, 2, n_f)).reshape(SEQ, dim)
    sin = jnp.concatenate([-sin, sin], axis=2).reshape(SEQ, dim)
    cos = jnp.tile(cos, (1, reps))
    sin = jnp.tile(sin, (1, reps))
    cos = jnp.concatenate([jnp.ones((CTX, dim * reps), F32), cos], axis=0)
    sin = jnp.concatenate([jnp.zeros((CTX, dim * reps), F32), sin], axis=0)
    return cos, sin


def _rope(x, cos, sin_signed, n_f):
    lane = lax.broadcasted_iota(jnp.int32, x.shape, x.ndim - 1)
    partner = jnp.where(lane % (2 * n_f) < n_f,
                        pltpu.roll(x, HEAD_DIM - n_f, axis=x.ndim - 1),
                        pltpu.roll(x, n_f, axis=x.ndim - 1))
    return x * cos + partner * sin_signed


_NA_GROUP_ROWS = RB // GRID_W
_NA_KEY_ROWS = 3 * _NA_GROUP_ROWS
_NA_PATTERN = (0, 1, 2, 2, 2, 2, 2, 3, 4)


def _na_key_block(g):
    return jnp.clip(g - 1, 0, GRID_H // _NA_GROUP_ROWS - 3)


def _na_bias_tables(rpb):
    c = np.arange(GRID_W)
    cs = np.clip(c - NA_WIN_W // 2, 0, GRID_W - NA_WIN_W)
    col_valid = (c[None, :] >= cs[:, None]) & (c[None, :] < cs[:, None] + NA_WIN_W)
    co = np.clip(c[None, :] - c[:, None] + NA_WIN_W - 1, 0, 2 * NA_WIN_W - 2)
    onehot = jnp.asarray((co[None] == np.arange(2 * NA_WIN_W - 1)[:, None, None]) & col_valid[None], F32)
    tcol = jnp.einsum("hro,oqk->hrqk", rpb.astype(F32), onehot, precision=lax.Precision.HIGHEST)
    tcol = jnp.where(jnp.asarray(col_valid)[None, None], tcol, NEG)
    neg_blk = jnp.full((NH, GRID_W, GRID_W), NEG, F32)
    tabs = []
    for g in (0, 1, 2, 7):
        r0 = _NA_GROUP_ROWS * g
        ks = int(np.clip(g - 1, 0, GRID_H // _NA_GROUP_ROWS - 3)) * _NA_GROUP_ROWS
        rows = []
        for dr in range(_NA_GROUP_ROWS):
            r = r0 + dr
            rs = int(np.clip(r - NA_WIN_H // 2, 0, GRID_H - NA_WIN_H))
            blks = []
            for krel in range(_NA_KEY_ROWS):
                kr = ks + krel
                blks.append(tcol[:, kr - r + NA_WIN_H - 1] if rs <= kr < rs + NA_WIN_H else neg_blk)
            rows.append(jnp.concatenate(blks, axis=-1))
        tabs.append(jnp.concatenate(rows, axis=-2))
    tabs.append(jnp.full((NH, RB, 3 * RB), NEG, F32))
    return jnp.stack(tabs, axis=1)


_NA_HPB = 2


def _na_kernel(q_ref, k0_ref, k1_ref, k2_ref, v0_ref, v1_ref, v2_ref, kc_ref, vc_ref, b_ref, o_ref):
    for hh in range(_NA_HPB):
        ln = slice(hh * HEAD_DIM, (hh + 1) * HEAD_DIM)
        q = (q_ref[:, ln].astype(F32) * HEAD_DIM ** -0.5).astype(BF16)
        s = [_dot_nt(q, kr[:, ln]) + b_ref[hh, 0, :, t * RB:(t + 1) * RB]
             for t, kr in enumerate((k0_ref, k1_ref, k2_ref))]
        s.append(_dot_nt(q, kc_ref[:, ln]))
        m = functools.reduce(jnp.maximum, [jnp.max(t, axis=-1, keepdims=True) for t in s])
        p = [jnp.exp(t - m) for t in s]
        l = functools.reduce(jnp.add, [jnp.sum(t, axis=-1, keepdims=True) for t in p])
        vs = (v0_ref, v1_ref, v2_ref, vc_ref)
        o = functools.reduce(jnp.add, [_dot(pt.astype(BF16), vr[:, ln]) for pt, vr in zip(p, vs)])
        o_ref[:, ln] = (o / l).astype(o_ref.dtype)


def _na_attention(proj, bias):
    hb = NH // _NA_HPB

    def qrow(h, g, b):
        return b * BLOCKS_PER_BATCH + jnp.where(g < 8, 1 + g, 0)

    def krow(t):
        return lambda h, g, b: (b * BLOCKS_PER_BATCH + 1 + _na_key_block(jnp.minimum(g, 7)) + t, hb + h)

    def vrow(t):
        return lambda h, g, b: (b * BLOCKS_PER_BATCH + 1 + _na_key_block(jnp.minimum(g, 7)) + t, 2 * hb + h)

    blk = (RB, _NA_HPB * HEAD_DIM)
    pattern = jnp.asarray(_NA_PATTERN, jnp.int32)

    def bias_map(h, g, b, pat_ref):
        return (h, pat_ref[g], 0, 0)

    def wrap(f):
        return lambda h, g, b, pat_ref: f(h, g, b)

    in_specs = [pl.BlockSpec(blk, wrap(lambda h, g, b: (qrow(h, g, b), h)))]
    in_specs += [pl.BlockSpec(blk, wrap(krow(t))) for t in range(3)]
    in_specs += [pl.BlockSpec(blk, wrap(vrow(t))) for t in range(3)]
    in_specs += [pl.BlockSpec(blk, wrap(lambda h, g, b: (b * BLOCKS_PER_BATCH, hb + h))),
                 pl.BlockSpec(blk, wrap(lambda h, g, b: (b * BLOCKS_PER_BATCH, 2 * hb + h))),
                 pl.BlockSpec((_NA_HPB, 1, RB, 3 * RB), bias_map)]

    def body(pat_ref, *refs):
        _na_kernel(*refs)

    return pl.pallas_call(
        body,
        grid_spec=pltpu.PrefetchScalarGridSpec(
            num_scalar_prefetch=1,
            grid=(hb, BLOCKS_PER_BATCH, BATCH),
            in_specs=in_specs,
            out_specs=pl.BlockSpec(blk, wrap(lambda h, g, b: (qrow(h, g, b), h))),
        ),
        out_shape=jax.ShapeDtypeStruct((M, NH * HEAD_DIM), BF16),
        compiler_params=_cparams(("arbitrary", "arbitrary", "arbitrary"), 32),
        name="na_attention",
    )(pattern, *([proj] * 9), bias)


_RET_NC = TB // RET_CHUNK
_RET_CTX_NC = CTX // RET_CHUNK


def _ret_kernel(lg_ref, q_ref, k_ref, v_ref, g_ref, cos_ref, sin_ref, o_ref, qr_ref, kr_ref, sf_ref, sb_ref):
    h = pl.program_id(1)
    lgf = lg_ref[0, h]
    lgb = lg_ref[1, h]
    c = RET_CHUNK
    qr_ref[...] = _rope(q_ref[...].astype(F32), cos_ref[...], sin_ref[...], HEAD_DIM // 4)
    kr_ref[...] = _rope(k_ref[...].astype(F32), cos_ref[...], sin_ref[...], HEAD_DIM // 4) * HEAD_DIM ** -0.5

    pos = lax.broadcasted_iota(jnp.int32, (c, 1), 0).astype(F32)
    kdec_f = jnp.exp(lgf * (c - 1.0 - pos))
    kdec_b = jnp.exp(lgb * pos)
    qdec_f = jnp.exp(lgf * (pos + 1.0))
    qdec_b = jnp.exp(lgb * (c - pos))
    gc_f = jnp.exp(lgf * c)
    gc_b = jnp.exp(lgb * c)
    rel = (lax.broadcasted_iota(jnp.int32, (c, c), 0) - lax.broadcasted_iota(jnp.int32, (c, c), 1)).astype(F32)
    dmask = (jnp.where(rel >= 0, jnp.exp(jnp.maximum(rel, 0.0) * lgf), 0.0)
             + jnp.where(rel <= 0, jnp.exp(jnp.maximum(-rel, 0.0) * lgb), 0.0))

    def kv_step(n, carry):
        rows = pl.ds(pl.multiple_of(n * c, c), c)
        kn = kr_ref[rows, :]
        vn = v_ref[rows, :]
        sf_ref[n] = _dot_tn((kn * kdec_f).astype(BF16), vn)
        sb_ref[n] = _dot_tn((kn * kdec_b).astype(BF16), vn)
        return carry

    lax.fori_loop(0, _RET_NC, kv_step, 0, unroll=3)

    def state_step(n, s, s_ref, gc):
        u = s_ref[n]
        s_ref[n] = s
        return s * gc + u

    zero = jnp.zeros((HEAD_DIM, RET_DV), F32)
    lax.fori_loop(0, _RET_NC, lambda n, s: state_step(n, s, sf_ref, gc_f), zero)
    s = lax.fori_loop(0, _RET_CTX_NC, lambda i, s: state_step(_RET_CTX_NC - 1 - i, s, sb_ref, gc_b), zero)
    lax.fori_loop(0, _RET_NC - _RET_CTX_NC, lambda i, s: state_step(_RET_NC - 1 - i, s, sb_ref, gc_b), s)

    def out_step(n, carry):
        rows = pl.ds(pl.multiple_of(n * c, c), c)
        qn = qr_ref[rows, :]
        vn = v_ref[rows, :]
        scores = _dot_nt(qn.astype(BF16), kr_ref[rows, :].astype(BF16)) * dmask
        intra = _dot(scores.astype(BF16), vn)
        qcat = jnp.concatenate([qn * qdec_f, qn * qdec_b], axis=1).astype(BF16)
        scat = jnp.concatenate([sf_ref[n], sb_ref[n]], axis=0).astype(BF16)
        o = intra + _dot(qcat, scat)
        o = o * lax.rsqrt(jnp.mean(o * o, axis=-1, keepdims=True) + EPS)
        o_ref[rows, :] = (o * _silu(g_ref[rows, :].astype(F32))).astype(o_ref.dtype)
        return carry

    lax.fori_loop(0, _RET_NC, out_step, 0, unroll=2)


def _retention(proj, log_g, cos, sin):
    seq128 = lambda blk0: pl.BlockSpec((TB, HEAD_DIM), lambda b, h, lg: (b, blk0 + h))
    seq256 = lambda blk0: pl.BlockSpec((TB, RET_DV), lambda b, h, lg: (b, blk0 + h))
    tab = pl.BlockSpec((TB, HEAD_DIM), lambda b, h, lg: (0, 0))
    return pl.pallas_call(
        _ret_kernel,
        grid_spec=pltpu.PrefetchScalarGridSpec(
            num_scalar_prefetch=1,
            grid=(BATCH, NH),
            in_specs=[seq128(3072 // HEAD_DIM), seq128(4096 // HEAD_DIM),
                      seq256(5120 // RET_DV), seq256(7168 // RET_DV), tab, tab],
            out_specs=pl.BlockSpec((TB, RET_DV), lambda b, h, lg: (b, h)),
            scratch_shapes=[pltpu.VMEM((TB, HEAD_DIM), F32), pltpu.VMEM((TB, HEAD_DIM), F32),
                            pltpu.VMEM((_RET_NC, HEAD_DIM, RET_DV), F32),
                            pltpu.VMEM((_RET_NC, HEAD_DIM, RET_DV), F32)],
        ),
        out_shape=jax.ShapeDtypeStruct((M, NH * RET_DV), BF16),
        compiler_params=_cparams(("arbitrary", "arbitrary"), 48),
        name="retention",
    )(log_g, proj, proj, proj, proj, cos, sin)


def _diff_kernel(lam_ref, q_ref, k_ref, v_ref, cq_ref, sq_ref, ck_ref, sk_ref, o_ref, kb_ref, vb_ref, *, post_scale):
    qi = pl.program_id(2)

    @pl.when(qi == 0)
    def _():
        kb_ref[...] = _rope(k_ref[...].astype(F32), ck_ref[...], sk_ref[...], DIFF_DK // 4).astype(BF16)
        vb_ref[...] = v_ref[...].astype(BF16)
        o_ref[...] = jnp.zeros_like(o_ref)

    @pl.when(qi > 0)
    def _():
        lam = lam_ref[0]
        q = _rope(q_ref[...].astype(F32), cq_ref[...], sq_ref[...], DIFF_DK // 4) * (DIFF_DK ** -0.5 * LOG2E)
        lane = lax.broadcasted_iota(jnp.int32, q.shape, 1)
        kb = kb_ref[...]
        p, inv = [], []
        for t in range(2):
            qt = jnp.where((lane >= DIFF_DK) == (t == 1), q, 0.0).astype(BF16)
            s = _dot_nt(qt, kb)
            e = jnp.exp2(s - jnp.max(s, axis=-1, keepdims=True))
            p.append(e)
            inv.append(1.0 / jnp.sum(e, axis=-1, keepdims=True))
        a = (p[0] * inv[0] - p[1] * (lam * inv[1])).astype(BF16)
        o = _dot(a, vb_ref[...])
        o = o * lax.rsqrt(jnp.mean(o * o, axis=-1, keepdims=True) + EPS)
        o_ref[...] = (o * post_scale).astype(o_ref.dtype)


def _diff_attention(proj, lam, cos, sin, post_scale):
    blk = (RB, HEAD_DIM)
    seq = (TB, HEAD_DIM)
    qmap = lambda b, h, qi, lam_ref: (b * BLOCKS_PER_BATCH + qi, h)
    return pl.pallas_call(
        functools.partial(_diff_kernel, post_scale=post_scale),
        grid_spec=pltpu.PrefetchScalarGridSpec(
            num_scalar_prefetch=1,
            grid=(BATCH, NH, BLOCKS_PER_BATCH),
            in_specs=[pl.BlockSpec(blk, qmap),
                      pl.BlockSpec(seq, lambda b, h, qi, lam_ref: (b, NH + h)),
                      pl.BlockSpec(seq, lambda b, h, qi, lam_ref: (b, 2 * NH + h)),
                      pl.BlockSpec(blk, lambda b, h, qi, lam_ref: (qi, 0)),
                      pl.BlockSpec(blk, lambda b, h, qi, lam_ref: (qi, 0)),
                      pl.BlockSpec(seq, lambda b, h, qi, lam_ref: (0, 0)),
                      pl.BlockSpec(seq, lambda b, h, qi, lam_ref: (0, 0))],
            out_specs=pl.BlockSpec(blk, qmap),
            scratch_shapes=[pltpu.VMEM(seq, BF16), pltpu.VMEM(seq, BF16)],
        ),
        out_shape=jax.ShapeDtypeStruct((M, NH * HEAD_DIM), BF16),
        compiler_params=_cparams(("arbitrary", "arbitrary", "arbitrary"), 48),
        name="diff_attention",
    )(lam, proj, proj, proj, cos, sin, cos, sin)


_HG_NC = TB // HG_CHUNK
_HG_CTX_NC = CTX // HG_CHUNK


def _chunk_cumsum(x, reverse):
    c = x.shape[0]
    row = lax.broadcasted_iota(jnp.int32, x.shape, 0)
    s = 1
    while s < c:
        if reverse:
            x = x + jnp.where(row < c - s, pltpu.roll(x, c - s, axis=0), 0.0)
        else:
            x = x + jnp.where(row >= s, pltpu.roll(x, s, axis=0), 0.0)
        s *= 2
    return x


def _hg_kernel(q_ref, ff_ref, fb_ref, v_ref, g_ref, lb_ref, o_ref,
               qs_ref, kf_ref, kb_ref, cf_ref, cb_ref, sf_ref, sb_ref, df_ref, db_ref):
    c = HG_CHUNK
    lb = lb_ref[0]
    qraw = q_ref[...].astype(F32)
    qs_ref[...] = qraw * _sigmoid(qraw)

    def gate(raw_ref, k_ref, lf_ref):
        f = lb + (1.0 - lb) * _sigmoid(raw_ref[...].astype(F32))
        k_ref[...] = 1.0 - f
        lf_ref[...] = jnp.log(f)

    gate(ff_ref, kf_ref, cf_ref)
    gate(fb_ref, kb_ref, cb_ref)

    def chunk_prep(n, carry):
        rows = pl.ds(pl.multiple_of(n * c, c), c)
        vn = v_ref[rows, :]
        for cum_ref, k_ref, u_ref, d_ref, reverse in ((cf_ref, kf_ref, sf_ref, df_ref, False),
                                                      (cb_ref, kb_ref, sb_ref, db_ref, True)):
            cum = _chunk_cumsum(cum_ref[rows, :], reverse)
            cum_ref[rows, :] = cum
            last = cum[0:1, :] if reverse else cum[c - 1:c, :]
            u_ref[n] = _dot_tn(vn, (k_ref[rows, :] * jnp.exp(last - cum)).astype(BF16))
            d_ref[n] = jnp.exp(last)
        return carry

    lax.fori_loop(0, _HG_NC, chunk_prep, 0, unroll=4)

    def state_step(n, st, s_ref, d_ref):
        u = s_ref[n]
        s_ref[n] = st
        return st * d_ref[n] + u

    zero = jnp.zeros((HEAD_DIM, HEAD_DIM), F32)
    lax.fori_loop(0, _HG_NC, lambda n, s: state_step(n, s, sf_ref, df_ref), zero)
    s = lax.fori_loop(0, _HG_CTX_NC, lambda i, s: state_step(_HG_CTX_NC - 1 - i, s, sb_ref, db_ref), zero)
    lax.fori_loop(0, _HG_NC - _HG_CTX_NC, lambda i, s: state_step(_HG_NC - 1 - i, s, sb_ref, db_ref), s)

    o_ref[0:CTX, :] = jnp.zeros((CTX, HEAD_DIM), o_ref.dtype)

    def level(q, kf, kb, cf, cb, half):
        span = 2 * half
        row = lax.broadcasted_iota(jnp.int32, (c, 1), 0)
        upper = (row % span) >= half
        ref_f = jnp.concatenate([jnp.broadcast_to(cf[b0 + half - 1:b0 + half, :], (span, HEAD_DIM))
                                 for b0 in range(0, c, span)], axis=0)
        ref_b = jnp.concatenate([jnp.broadcast_to(cb[b0 + half:b0 + half + 1, :], (span, HEAD_DIM))
                                 for b0 in range(0, c, span)], axis=0)
        ef = jnp.exp(jnp.minimum(jnp.where(upper, cf - ref_f, ref_f - cf), 0.0))
        eb = jnp.exp(jnp.minimum(jnp.where(upper, ref_b - cb, cb - ref_b), 0.0))
        qq = jnp.concatenate([jnp.where(upper, q * ef, 0.0), jnp.where(upper, 0.0, q * eb)], axis=1)
        kk = jnp.concatenate([jnp.where(upper, 0.0, kf * ef), jnp.where(upper, kb * eb, 0.0)], axis=1)
        a = _dot_nt(qq.astype(BF16), kk.astype(BF16))
        if span == c:
            return a
        ri = lax.broadcasted_iota(jnp.int32, (c, c), 0)
        ci = lax.broadcasted_iota(jnp.int32, (c, c), 1)
        return jnp.where(ri // span == ci // span, a, 0.0)

    def sub_block(q, kf, kb, lo):
        sub = HG_SUB
        row = lax.broadcasted_iota(jnp.int32, (sub, 1), 0)
        lane = lax.broadcasted_iota(jnp.int32, (sub, c), 1)
        qb, kfb, kbb = q[lo:lo + sub], kf[lo:lo + sub], kb[lo:lo + sub]
        ffb, fbb = 1.0 - kfb, 1.0 - kbb
        pf = [None] * sub
        p = jnp.where(row == sub - 1, 1.0, jnp.zeros((sub, HEAD_DIM), F32))
        pf[sub - 1] = p
        for j in range(sub - 2, -1, -1):
            p = jnp.where(row == j, 1.0, p * ffb[j + 1:j + 2])
            pf[j] = p
        att = jnp.zeros((sub, c), F32)
        p = None
        for j in range(sub):
            p = (jnp.where(row == 0, 1.0, jnp.zeros((sub, HEAD_DIM), F32)) if j == 0
                 else jnp.where(row == j, 1.0, p * fbb[j - 1:j]))
            col = jnp.sum(qb * (pf[j] * kfb[j:j + 1] + p * kbb[j:j + 1]), axis=-1, keepdims=True)
            att = jnp.where(lane == lo + j, col, att)
        return att

    def out_step(n, carry):
        rows = pl.ds(pl.multiple_of(n * c, c), c)
        q = qs_ref[rows, :]
        kf = kf_ref[rows, :]
        kb = kb_ref[rows, :]
        cf = cf_ref[rows, :]
        cb = cb_ref[rows, :]
        qcat = jnp.concatenate([q * jnp.exp(cf), q * jnp.exp(cb)], axis=1).astype(BF16)
        scat = jnp.concatenate([sf_ref[n], sb_ref[n]], axis=1).astype(BF16)
        o = _dot_nt(qcat, scat)
        att = jnp.concatenate([sub_block(q, kf, kb, lo) for lo in range(0, c, HG_SUB)], axis=0)
        half = c // 2
        while half >= HG_SUB:
            att = att + level(q, kf, kb, cf, cb, half)
            half //= 2
        o = o + _dot(att.astype(BF16), v_ref[rows, :])
        o = o * lax.rsqrt(jnp.mean(o * o, axis=-1, keepdims=True) + EPS)
        graw = g_ref[rows, :].astype(F32)
        o_ref[rows, :] = (o * (graw * _sigmoid(graw))).astype(o_ref.dtype)
        return carry

    lax.fori_loop(_HG_CTX_NC, _HG_NC, out_step, 0, unroll=2)


def _hgrn(proj, lb):
    seq = lambda col0: pl.BlockSpec((TB, HEAD_DIM), lambda b, h: (b, col0 // HEAD_DIM + h))
    full = pltpu.VMEM((TB, HEAD_DIM), F32)
    states = pltpu.VMEM((_HG_NC, HEAD_DIM, HEAD_DIM), F32)
    decays = pltpu.VMEM((_HG_NC, 1, HEAD_DIM), F32)
    return pl.pallas_call(
        _hg_kernel,
        grid=(BATCH, NH),
        in_specs=[seq(3072), seq(4096), seq(5120), seq(6144), seq(7168),
                  pl.BlockSpec((1, 1, HEAD_DIM), lambda b, h: (h, 0, 0))],
        out_specs=pl.BlockSpec((TB, HEAD_DIM), lambda b, h: (b, h)),
        out_shape=jax.ShapeDtypeStruct((M, NH * HEAD_DIM), BF16),
        scratch_shapes=[full, full, full, full, full, states, states, decays, decays],
        compiler_params=_cparams(("arbitrary", "arbitrary"), 48),
        name="hgrn",
    )(proj, proj, proj, proj, proj, lb.reshape(NH, 1, HEAD_DIM))


def _row_gather_start(idx_ref, k, src_hbm, dst, sem, n):
    for r in range(n):
        pltpu.make_async_copy(src_hbm.at[pl.ds(idx_ref[0, k, r], 1)], dst.at[pl.ds(r, 1)], sem).start()


def _row_gather_wait(src_hbm, dst, sem, n):
    pltpu.make_async_copy(src_hbm.at[pl.ds(0, n)], dst, sem).wait()


def _moe_up_kernel(be_ref, nu_ref, rt0_ref, rt1_ref, x_hbm, w1_ref, w3_ref, o_ref, xbuf, sem, w1b_ref, w3b_ref):
    i = pl.program_id(0)
    nu = nu_ref[0]

    @pl.when(jnp.logical_and(i == 0, nu > 0))
    def _():
        _row_gather_start(rt0_ref, 0, x_hbm, xbuf.at[0], sem.at[0], MOE_TM)

    @pl.when(i + 1 < nu)
    def _():
        slot = (i + 1) % 2
        _row_gather_start(rt1_ref, 0, x_hbm, xbuf.at[slot], sem.at[slot], MOE_TM)

    changed = jnp.logical_or(i == 0, be_ref[i] != be_ref[jnp.maximum(i - 1, 0)])

    @pl.when(changed)
    def _():
        w1b_ref[...] = w1_ref[0, 0].astype(BF16)
        w3b_ref[...] = w3_ref[0, 0].astype(BF16)

    @pl.when(i < nu)
    def _():
        slot = i % 2
        _row_gather_wait(x_hbm, xbuf.at[slot], sem.at[slot], MOE_TM)
        x = xbuf[slot].astype(BF16)
        a = _dot(x, w1b_ref[...])
        o_ref[...] = (_silu(a) * _dot(x, w3b_ref[...])).astype(o_ref.dtype)

    @pl.when(i >= nu)
    def _():
        o_ref[...] = jnp.zeros_like(o_ref)


def _moe_up(layer, blk_e, n_used, row_tok, h2, w1, w3):
    wspec = pl.BlockSpec((1, 1, D, EXPERT_FF), lambda i, be, nu: (layer, be[i], 0, 0))
    rt = row_tok.reshape(MOE_BLOCKS, 1, MOE_TM)
    smem_blk = lambda f: pl.BlockSpec((1, 1, MOE_TM), f, memory_space=pltpu.SMEM)
    return pl.pallas_call(
        _moe_up_kernel,
        grid_spec=pltpu.PrefetchScalarGridSpec(
            num_scalar_prefetch=2,
            grid=(MOE_BLOCKS,),
            in_specs=[smem_blk(lambda i, be, nu: (i, 0, 0)),
                      smem_blk(lambda i, be, nu: (jnp.minimum(i + 1, MOE_BLOCKS - 1), 0, 0)),
                      pl.BlockSpec(memory_space=pl.ANY), wspec, wspec],
            out_specs=pl.BlockSpec((MOE_TM, EXPERT_FF), lambda i, be, nu: (i, 0)),
            scratch_shapes=[pltpu.VMEM((2, MOE_TM, D), F32), pltpu.SemaphoreType.DMA((2,)),
                            pltpu.VMEM((D, EXPERT_FF), BF16), pltpu.VMEM((D, EXPERT_FF), BF16)],
        ),
        out_shape=jax.ShapeDtypeStruct((MOE_ROWS, EXPERT_FF), BF16),
        compiler_params=_cparams(("arbitrary",), 52),
        name="moe_up",
    )(blk_e, n_used, rt, rt, h2, w1, w3)


def _moe_down_kernel(be_ref, nu_ref, h_ref, w2_ref, o_ref, w2b_ref):
    i = pl.program_id(0)
    changed = jnp.logical_or(i == 0, be_ref[i] != be_ref[jnp.maximum(i - 1, 0)])

    @pl.when(changed)
    def _():
        w2b_ref[...] = w2_ref[0, 0].astype(BF16)

    @pl.when(i < nu_ref[0])
    def _():
        o_ref[...] = _dot(h_ref[...], w2b_ref[...])

    @pl.when(i >= nu_ref[0])
    def _():
        o_ref[...] = jnp.zeros_like(o_ref)


def _moe_down(layer, blk_e, n_used, hmid, w2):
    return pl.pallas_call(
        _moe_down_kernel,
        grid_spec=pltpu.PrefetchScalarGridSpec(
            num_scalar_prefetch=2,
            grid=(MOE_BLOCKS,),
            in_specs=[pl.BlockSpec((MOE_TM, EXPERT_FF), lambda i, be, nu: (i, 0)),
                      pl.BlockSpec((1, 1, EXPERT_FF, D), lambda i, be, nu: (layer, be[i], 0, 0))],
            out_specs=pl.BlockSpec((MOE_TM, D), lambda i, be, nu: (i, 0)),
            scratch_shapes=[pltpu.VMEM((EXPERT_FF, D), BF16)],
        ),
        out_shape=jax.ShapeDtypeStruct((MOE_ROWS, D), F32),
        compiler_params=_cparams(("arbitrary",), 40),
        name="moe_down",
    )(blk_e, n_used, hmid, w2)


def _combine_kernel(p0_ref, p1_ref, y_hbm, x_ref, w_ref, g_ref, mul_ref, add_ref, *rest, last):
    out_refs, (ybuf, sem) = rest[:-2], rest[-2:]
    i = pl.program_id(0)
    n = pl.num_programs(0)

    def start(p_ref, slot):
        for k in range(TOP_K):
            _row_gather_start(p_ref, k, y_hbm, ybuf.at[slot, k], sem.at[slot], RB)

    @pl.when(i == 0)
    def _():
        start(p0_ref, 0)

    @pl.when(i + 1 < n)
    def _():
        start(p1_ref, (i + 1) % 2)

    slot = i % 2
    for k in range(TOP_K):
        _row_gather_wait(y_hbm, ybuf.at[slot, k], sem.at[slot], RB)
    w = w_ref[...]
    y = w[:, 0:1] * ybuf[slot, 0] + w[:, 1:2] * ybuf[slot, 1]
    t = x_ref[...] + g_ref[0] * y
    nrm = t * lax.rsqrt(jnp.mean(t * t, axis=-1, keepdims=True) + EPS) * mul_ref[0] + add_ref[0]
    if last:
        out_refs[0][0] = nrm
    else:
        out_refs[0][...] = t
        out_refs[1][...] = nrm.astype(BF16)


def _moe_combine(tok, y, pos, gates, gate_mod, mul, add, last):
    nb = M // RB
    pblk = pos.reshape(nb, RB, TOP_K).transpose(0, 2, 1)
    row = pl.BlockSpec((RB, D), lambda i: (i, 0))
    mod = pl.BlockSpec((1, 1, D), _mod_index)
    smem_blk = lambda f: pl.BlockSpec((1, TOP_K, RB), f, memory_space=pltpu.SMEM)
    if last:
        out_specs = pl.BlockSpec((1, RB, D), lambda i: (i // BLOCKS_PER_BATCH,
                                                        jnp.maximum(i % BLOCKS_PER_BATCH - 1, 0), 0))
        out_shape = jax.ShapeDtypeStruct((BATCH, SEQ, D), F32)
    else:
        out_specs = [row, row]
        out_shape = [jax.ShapeDtypeStruct((M, D), F32), jax.ShapeDtypeStruct((M, D), BF16)]
    return pl.pallas_call(
        functools.partial(_combine_kernel, last=last),
        grid=(nb,),
        in_specs=[smem_blk(lambda i: (i, 0, 0)), smem_blk(lambda i: (jnp.minimum(i + 1, nb - 1), 0, 0)),
                  pl.BlockSpec(memory_space=pl.ANY), row,
                  pl.BlockSpec((RB, TOP_K), lambda i: (i, 0)), mod, mod, mod],
        out_specs=out_specs,
        out_shape=out_shape,
        scratch_shapes=[pltpu.VMEM((2, TOP_K, RB, D), F32), pltpu.SemaphoreType.DMA((2,))],
        compiler_params=_cparams(("arbitrary",), 48),
        name="moe_combine",
    )(pblk, pblk, y, tok, gates, gate_mod, mul, add)


def _route(logits, valid):
    lg_g = logits[:, :N_GROUPS]
    p_g = jax.nn.softmax(lg_g, axis=-1)
    g_top = jnp.argmax(lg_g, axis=-1)
    w_grp = jnp.take_along_axis(p_g, g_top[:, None], axis=1)
    lg_e = logits[:, N_GROUPS:N_GROUPS + N_EXPERTS].reshape(M, N_GROUPS, EXPERTS_PER_GROUP)
    lg_in = jnp.take_along_axis(lg_e, g_top[:, None, None], axis=1)[:, 0]
    top_v, top_i = lax.top_k(lg_in, TOP_K)
    gates = w_grp * jax.nn.softmax(top_v, axis=-1)
    gates = jnp.where(valid[:, None], gates, 0.0)
    eid = g_top[:, None] * EXPERTS_PER_GROUP + top_i
    eid = jnp.where(valid[:, None], eid, N_EXPERTS).reshape(-1).astype(jnp.int32)
    n_slots = M * TOP_K
    slot = jnp.arange(n_slots, dtype=jnp.int32)
    experts = jnp.arange(N_EXPERTS + 1, dtype=jnp.int32)
    counts = jnp.sum((eid[:, None] == experts[None, :]).astype(jnp.int32), axis=0)
    padded = (counts + MOE_TM - 1) // MOE_TM * MOE_TM
    p_end = jnp.cumsum(padded)
    p_start = p_end - padded
    u_start = jnp.cumsum(counts) - counts
    shift = p_start - u_start
    s_e, order = lax.sort((eid, slot), num_keys=1, is_stable=True)
    dest = slot + jnp.sum(jnp.where(s_e[:, None] == experts[None, :], shift[None, :], 0), axis=1)
    dest = jnp.where(s_e < N_EXPERTS, dest, 0)
    _, pos = lax.sort((order, dest), num_keys=1)
    pos = pos.reshape(M, TOP_K)
    blk_start = jnp.arange(MOE_BLOCKS, dtype=jnp.int32) * MOE_TM
    blk_e = jnp.sum((p_end[None, :N_EXPERTS] <= blk_start[:, None]).astype(jnp.int32), axis=1)
    blk_e = jnp.minimum(blk_e, N_EXPERTS - 1)
    onehot_be = blk_e[:, None] == experts[None, :N_EXPERTS]
    blk_shift = jnp.sum(jnp.where(onehot_be, shift[None, :N_EXPERTS], 0), axis=1)
    blk_lim = jnp.sum(jnp.where(onehot_be, (p_start + counts)[None, :N_EXPERTS], 0), axis=1)
    rows = jnp.arange(MOE_ROWS, dtype=jnp.int32).reshape(MOE_BLOCKS, MOE_TM)
    src = jnp.clip(rows - blk_shift[:, None], 0, n_slots - 1)
    row_tok = jnp.where(rows < blk_lim[:, None], jnp.take(order // TOP_K, src), 0)
    n_used = (p_end[N_EXPERTS - 1] // MOE_TM).astype(jnp.int32).reshape(1)
    return gates, row_tok, pos, blk_e, n_used


def _moe(layer, tok, mul, add, gate_mod, wg, bg, we, be, w1, w3, w2, valid, next_mul, next_add, last):
    wr = jnp.zeros((D, ROUTER_PAD), F32).at[:, :N_GROUPS].set(wg).at[:, N_GROUPS:N_GROUPS + N_EXPERTS].set(we)
    w_hi = wr.astype(BF16)
    w_lo = (wr - w_hi.astype(F32)).astype(BF16)
    r_bias = jnp.zeros((1, ROUTER_PAD), F32).at[0, :N_GROUPS].set(bg).at[0, N_GROUPS:N_GROUPS + N_EXPERTS].set(be)
    h2, logits = _norm_router(tok, mul, add, w_hi, w_lo, r_bias)
    gates, row_tok, pos, blk_e, n_used = _route(logits, valid)
    hmid = _moe_up(layer, blk_e, n_used, row_tok, h2, w1, w3)
    y = _moe_down(layer, blk_e, n_used, hmid, w2)
    return _moe_combine(tok, y, pos, gates, gate_mod, next_mul, next_add, last)


def _mod9(t):
    return t[:BATCH + 1].reshape(BATCH + 1, 1, D)


def kernel(x, c, ctx, c_ctx, ada_w, ada_b, norm1_w, norm2_w, w_in_even, w_out_even, na_rpb, ret_decay,
           w_in_odd, w_out_odd, diff_lambda, hg_lb_logits, router_g_w, router_g_b, router_e_w, router_e_b,
           moe_w1, moe_w3, moe_w2, norm_f_w):
    depth = ada_w.shape[0]
    tok = jnp.concatenate([ctx, x], axis=1).reshape(M, D)
    c16 = jnp.concatenate([c, c_ctx[None], jnp.zeros((16 - BATCH - 1, D), F32)], axis=0)
    mod = _ada_mod(c16, ada_w, ada_b)
    lb_all = jnp.cumsum(jax.nn.softmax(hg_lb_logits.astype(F32), axis=0), axis=0)
    lb_all = lb_all - lb_all[0]
    is_latent = (jnp.arange(M) % TB) >= CTX
    all_rows = jnp.ones((M,), bool)

    mods = [[_mod9(t) for t in jnp.split(mod[l], 6, axis=-1)] for l in range(depth)]
    h = _norm_mod(tok, norm1_w[0] * (1.0 + mods[0][1]), mods[0][0])
    for l in range(depth):
        with_ctx = l < depth - 1
        sh1, s1, g1, sh2, s2, g2 = mods[l]
        if with_ctx:
            next_mul, next_add = norm1_w[l + 1] * (1.0 + mods[l + 1][1]), mods[l + 1][0]
        else:
            next_mul = jnp.broadcast_to(norm_f_w.reshape(1, 1, D), (BATCH + 1, 1, D))
            next_add = jnp.zeros((BATCH + 1, 1, D), F32)
        j = l // 2
        if l % 2 == 0:
            proj = _matmul(h, w_in_even[j].astype(BF16), BF16)
            a = _na_attention(proj, _na_bias_tables(na_rpb[j]))
            cos, sin = _rope_tables(HEAD_DIM, 1)
            r = _retention(proj, -jnp.exp(ret_decay[j].astype(F32)), cos, sin)
            w_out = w_out_even[j].astype(BF16)
            tok = _out_proj(a, r, w_out[:NH * HEAD_DIM], w_out[NH * HEAD_DIM:], tok, g1)
        else:
            proj = _matmul(h, w_in_odd[j].astype(BF16), BF16)
            lp = diff_lambda[j].astype(F32)
            lam_init = 0.8 - 0.6 * math.exp(-0.3 * l)
            lam = jnp.exp(jnp.sum(lp[0] * lp[1])) - jnp.exp(jnp.sum(lp[2] * lp[3])) + lam_init
            cos, sin = _rope_tables(DIFF_DK, 2)
            d_l = _diff_attention(proj, lam.reshape(1), cos, sin, 1.0 - lam_init)
            g_l = _hgrn(proj, lb_all[l])
            w_out = w_out_odd[j].astype(BF16)
            tok = _out_proj(d_l, g_l, w_out[:NH * HEAD_DIM], w_out[NH * HEAD_DIM:], tok, g1)
        res = _moe(l, tok, norm2_w[l] * (1.0 + s2), sh2, g2,
                   router_g_w[l], router_g_b[l], router_e_w[l], router_e_b[l],
                   moe_w1, moe_w3, moe_w2, all_rows if with_ctx else is_latent,
                   next_mul, next_add, not with_ctx)
        if with_ctx:
            tok, h = res
    return res
```

```python
import functools
import math

import numpy as np
import jax
import jax.numpy as jnp
from jax import lax
from jax.experimental import pallas as pl
from jax.experimental.pallas import tpu as pltpu

F32 = jnp.float32
BF16 = jnp.bfloat16

D = 2048
BATCH = 8
SEQ = 2048
CTX = 256
TB = SEQ + CTX
M = BATCH * TB
GRID_W = 64
GRID_H = SEQ // GRID_W
HEAD_DIM = 128
NH = 8
NA_WIN_H = 8
NA_WIN_W = 16
RET_DV = 256
RET_CHUNK = 128
DIFF_DK = 64
HG_CHUNK = 64
HG_SUB = 8
N_GROUPS = 4
EXPERTS_PER_GROUP = 8
N_EXPERTS = 32
TOP_K = 2
EXPERT_FF = 1024
ROPE_BASE = 10000.0
EPS = 1e-6
NEG = -1e30
LOG2E = math.log2(math.e)

RB = 256
BLOCKS_PER_BATCH = TB // RB
MOE_TM = 256
MOE_ROWS = M * TOP_K + N_EXPERTS * MOE_TM
MOE_BLOCKS = MOE_ROWS // MOE_TM
ROUTER_PAD = 128


def _cparams(sem, vmem_mb):
    return pltpu.CompilerParams(dimension_semantics=sem, vmem_limit_bytes=vmem_mb * 1024 * 1024)


def _sigmoid(x):
    return 0.5 * jnp.tanh(0.5 * x) + 0.5


def _silu(x):
    return x * _sigmoid(x)


_NT = (((1,), (1,)), ((), ()))
_TN = (((0,), (0,)), ((), ()))


def _dot(a, b):
    return jnp.dot(a, b, preferred_element_type=F32)


def _dot_nt(a, b):
    return lax.dot_general(a, b, _NT, preferred_element_type=F32)


def _dot_tn(a, b):
    return lax.dot_general(a, b, _TN, preferred_element_type=F32)


def _ada_kernel(c_ref, w_ref, b_ref, o_ref):
    a = _silu(c_ref[...]).astype(BF16)
    o_ref[0] = _dot(a, w_ref[0].astype(BF16)) + b_ref[0]


def _ada_mod(c16, ada_w, ada_b):
    depth, _, n = ada_w.shape
    tn = 1024
    return pl.pallas_call(
        _ada_kernel,
        grid=(depth, n // tn),
        in_specs=[
            pl.BlockSpec((16, D), lambda l, j: (0, 0)),
            pl.BlockSpec((1, D, tn), lambda l, j: (l, 0, j)),
            pl.BlockSpec((1, 1, tn), lambda l, j: (l, 0, j)),
        ],
        out_specs=pl.BlockSpec((1, 16, tn), lambda l, j: (l, 0, j)),
        out_shape=jax.ShapeDtypeStruct((depth, 16, n), F32),
        compiler_params=_cparams(("arbitrary", "arbitrary"), 40),
        name="ada_mod",
    )(c16, ada_w, ada_b.reshape(depth, 1, n))


def _mod_index(i):
    return (jnp.where(i % BLOCKS_PER_BATCH == 0, BATCH, i // BLOCKS_PER_BATCH), 0, 0)


def _norm_body(x_ref, mul_ref, add_ref):
    x = x_ref[...]
    y = x * lax.rsqrt(jnp.mean(x * x, axis=-1, keepdims=True) + EPS)
    return y * mul_ref[0] + add_ref[0]


def _norm_kernel(x_ref, mul_ref, add_ref, o_ref):
    o_ref[...] = _norm_body(x_ref, mul_ref, add_ref).astype(BF16)


def _norm_router_kernel(x_ref, mul_ref, add_ref, whi_ref, wlo_ref, rb_ref, o_ref, lg_ref):
    h = _norm_body(x_ref, mul_ref, add_ref)
    hi = h.astype(BF16)
    lo = (h - hi.astype(F32)).astype(BF16)
    o_ref[...] = h
    lg_ref[...] = (_dot(hi, whi_ref[...]) + _dot(hi, wlo_ref[...]) + _dot(lo, whi_ref[...])) + rb_ref[...]


def _norm_mod(tok, mul, add):
    spec_mod = pl.BlockSpec((1, 1, D), _mod_index)
    return pl.pallas_call(
        _norm_kernel,
        grid=(M // RB,),
        in_specs=[pl.BlockSpec((RB, D), lambda i: (i, 0)), spec_mod, spec_mod],
        out_specs=pl.BlockSpec((RB, D), lambda i: (i, 0)),
        out_shape=jax.ShapeDtypeStruct((M, D), BF16),
        compiler_params=_cparams(("arbitrary",), 32),
        name="norm_mod",
    )(tok, mul, add)


def _norm_router(tok, mul, add, w_hi, w_lo, r_bias):
    spec_mod = pl.BlockSpec((1, 1, D), _mod_index)
    spec_w = pl.BlockSpec((D, ROUTER_PAD), lambda i: (0, 0))
    return pl.pallas_call(
        _norm_router_kernel,
        grid=(M // RB,),
        in_specs=[pl.BlockSpec((RB, D), lambda i: (i, 0)), spec_mod, spec_mod, spec_w, spec_w,
                  pl.BlockSpec((1, ROUTER_PAD), lambda i: (0, 0))],
        out_specs=[pl.BlockSpec((RB, D), lambda i: (i, 0)), pl.BlockSpec((RB, ROUTER_PAD), lambda i: (i, 0))],
        out_shape=[jax.ShapeDtypeStruct((M, D), F32), jax.ShapeDtypeStruct((M, ROUTER_PAD), F32)],
        compiler_params=_cparams(("arbitrary",), 32),
        name="norm_router",
    )(tok, mul, add, w_hi, w_lo, r_bias)


def _matmul_kernel(a_ref, w_ref, o_ref):
    o_ref[...] = _dot(a_ref[...], w_ref[...]).astype(o_ref.dtype)


def _matmul(a, w, out_dtype):
    m, k = a.shape
    n = w.shape[1]
    tm, tn = TB, 512
    return pl.pallas_call(
        _matmul_kernel,
        grid=(m // tm, n // tn),
        in_specs=[pl.BlockSpec((tm, k), lambda i, j: (i, 0)), pl.BlockSpec((k, tn), lambda i, j: (0, j))],
        out_specs=pl.BlockSpec((tm, tn), lambda i, j: (i, j)),
        out_shape=jax.ShapeDtypeStruct((m, n), out_dtype),
        compiler_params=_cparams(("arbitrary", "arbitrary"), 48),
        name="proj_in",
    )(a, w)


def _out_proj_kernel(a1_ref, a2_ref, w1_ref, w2_ref, res_ref, gl_ref, gc_ref, o_ref, *, tm):
    acc = _dot(a1_ref[...], w1_ref[...]) + _dot(a2_ref[...], w2_ref[...])
    row = pl.program_id(0) * tm + lax.broadcasted_iota(jnp.int32, (tm, 1), 0)
    gate = jnp.where(row % TB < CTX, gc_ref[0], gl_ref[0])
    o_ref[...] = res_ref[...] + gate * acc


def _out_proj(a1, a2, w1, w2, tok, gate):
    tm, tn = TB // 2, 512
    k1, k2 = a1.shape[1], a2.shape[1]
    return pl.pallas_call(
        functools.partial(_out_proj_kernel, tm=tm),
        grid=(M // tm, D // tn),
        in_specs=[
            pl.BlockSpec((tm, k1), lambda i, j: (i, 0)),
            pl.BlockSpec((tm, k2), lambda i, j: (i, 0)),
            pl.BlockSpec((k1, tn), lambda i, j: (0, j)),
            pl.BlockSpec((k2, tn), lambda i, j: (0, j)),
            pl.BlockSpec((tm, tn), lambda i, j: (i, j)),
            pl.BlockSpec((1, 1, tn), lambda i, j: (i // 2, 0, j)),
            pl.BlockSpec((1, 1, tn), lambda i, j: (BATCH, 0, j)),
        ],
        out_specs=pl.BlockSpec((tm, tn), lambda i, j: (i, j)),
        out_shape=jax.ShapeDtypeStruct((M, D), F32),
        compiler_params=_cparams(("arbitrary", "arbitrary"), 48),
        name="proj_out",
    )(a1, a2, w1, w2, tok, gate, gate)


def _rope_tables(dim, reps):
    n_f = dim // 4
    inv = ROPE_BASE ** (-jnp.arange(n_f, dtype=F32) / n_f)
    t = jnp.arange(SEQ)
    pos = jnp.stack([t // GRID_W, t % GRID_W], axis=-1).astype(F32)
    ang = pos[:, :, None] * inv
    cos = jnp.cos(ang)[:, :, None, :]
    sin = jnp.sin(ang)[:, :, None, :]
    cos = jnp.broadcast_to(cos, (SEQ, 2, 2, n_f)).reshape(SEQ, dim)
    sin = jnp.concatenate([-sin, sin], axis=2).reshape(SEQ, dim)
    cos = jnp.tile(cos, (1, reps))
    sin = jnp.tile(sin, (1, reps))
    cos = jnp.concatenate([jnp.ones((CTX, dim * reps), F32), cos], axis=0)
    sin = jnp.concatenate([jnp.zeros((CTX, dim * reps), F32), sin], axis=0)
    return cos, sin


def _rope(x, cos, sin_signed, n_f):
    lane = lax.broadcasted_iota(jnp.int32, x.shape, x.ndim - 1)
    partner = jnp.where(lane % (2 * n_f) < n_f,
                        pltpu.roll(x, HEAD_DIM - n_f, axis=x.ndim - 1),
                        pltpu.roll(x, n_f, axis=x.ndim - 1))
    return x * cos + partner * sin_signed


_NA_GROUP_ROWS = RB // GRID_W
_NA_KEY_ROWS = 3 * _NA_GROUP_ROWS
_NA_PATTERN = (0, 1, 2, 2, 2, 2, 2, 3, 4)


def _na_key_block(g):
    return jnp.clip(g - 1, 0, GRID_H // _NA_GROUP_ROWS - 3)


def _na_bias_tables(rpb):
    c = np.arange(GRID_W)
    cs = np.clip(c - NA_WIN_W // 2, 0, GRID_W - NA_WIN_W)
    col_valid = (c[None, :] >= cs[:, None]) & (c[None, :] < cs[:, None] + NA_WIN_W)
    co = np.clip(c[None, :] - c[:, None] + NA_WIN_W - 1, 0, 2 * NA_WIN_W - 2)
    onehot = jnp.asarray((co[None] == np.arange(2 * NA_WIN_W - 1)[:, None, None]) & col_valid[None], F32)
    tcol = jnp.einsum("hro,oqk->hrqk", rpb.astype(F32), onehot, precision=lax.Precision.HIGHEST)
    tcol = jnp.where(jnp.asarray(col_valid)[None, None], tcol, NEG)
    neg_blk = jnp.full((NH, GRID_W, GRID_W), NEG, F32)
    tabs = []
    for g in (0, 1, 2, 7):
        r0 = _NA_GROUP_ROWS * g
        ks = int(np.clip(g - 1, 0, GRID_H // _NA_GROUP_ROWS - 3)) * _NA_GROUP_ROWS
        rows = []
        for dr in range(_NA_GROUP_ROWS):
            r = r0 + dr
            rs = int(np.clip(r - NA_WIN_H // 2, 0, GRID_H - NA_WIN_H))
            blks = []
            for krel in range(_NA_KEY_ROWS):
                kr = ks + krel
                blks.append(tcol[:, kr - r + NA_WIN_H - 1] if rs <= kr < rs + NA_WIN_H else neg_blk)
            rows.append(jnp.concatenate(blks, axis=-1))
        tabs.append(jnp.concatenate(rows, axis=-2))
    tabs.append(jnp.full((NH, RB, 3 * RB), NEG, F32))
    return jnp.stack(tabs, axis=1)


_NA_HPB = 2


def _na_kernel(q_ref, k0_ref, k1_ref, k2_ref, v0_ref, v1_ref, v2_ref, kc_ref, vc_ref, b_ref, o_ref):
    for hh in range(_NA_HPB):
        ln = slice(hh * HEAD_DIM, (hh + 1) * HEAD_DIM)
        q = (q_ref[:, ln].astype(F32) * HEAD_DIM ** -0.5).astype(BF16)
        s = [_dot_nt(q, kr[:, ln]) + b_ref[hh, 0, :, t * RB:(t + 1) * RB]
             for t, kr in enumerate((k0_ref, k1_ref, k2_ref))]
        s.append(_dot_nt(q, kc_ref[:, ln]))
        m = functools.reduce(jnp.maximum, [jnp.max(t, axis=-1, keepdims=True) for t in s])
        p = [jnp.exp(t - m) for t in s]
        l = functools.reduce(jnp.add, [jnp.sum(t, axis=-1, keepdims=True) for t in p])
        vs = (v0_ref, v1_ref, v2_ref, vc_ref)
        o = functools.reduce(jnp.add, [_dot(pt.astype(BF16), vr[:, ln]) for pt, vr in zip(p, vs)])
        o_ref[:, ln] = (o / l).astype(o_ref.dtype)


def _na_attention(proj, bias):
    hb = NH // _NA_HPB

    def qrow(h, g, b):
        return b * BLOCKS_PER_BATCH + jnp.where(g < 8, 1 + g, 0)

    def krow(t):
        return lambda h, g, b: (b * BLOCKS_PER_BATCH + 1 + _na_key_block(jnp.minimum(g, 7)) + t, hb + h)

    def vrow(t):
        return lambda h, g, b: (b * BLOCKS_PER_BATCH + 1 + _na_key_block(jnp.minimum(g, 7)) + t, 2 * hb + h)

    blk = (RB, _NA_HPB * HEAD_DIM)
    pattern = jnp.asarray(_NA_PATTERN, jnp.int32)

    def bias_map(h, g, b, pat_ref):
        return (h, pat_ref[g], 0, 0)

    def wrap(f):
        return lambda h, g, b, pat_ref: f(h, g, b)

    in_specs = [pl.BlockSpec(blk, wrap(lambda h, g, b: (qrow(h, g, b), h)))]
    in_specs += [pl.BlockSpec(blk, wrap(krow(t))) for t in range(3)]
    in_specs += [pl.BlockSpec(blk, wrap(vrow(t))) for t in range(3)]
    in_specs += [pl.BlockSpec(blk, wrap(lambda h, g, b: (b * BLOCKS_PER_BATCH, hb + h))),
                 pl.BlockSpec(blk, wrap(lambda h, g, b: (b * BLOCKS_PER_BATCH, 2 * hb + h))),
                 pl.BlockSpec((_NA_HPB, 1, RB, 3 * RB), bias_map)]

    def body(pat_ref, *refs):
        _na_kernel(*refs)

    return pl.pallas_call(
        body,
        grid_spec=pltpu.PrefetchScalarGridSpec(
            num_scalar_prefetch=1,
            grid=(hb, BLOCKS_PER_BATCH, BATCH),
            in_specs=in_specs,
            out_specs=pl.BlockSpec(blk, wrap(lambda h, g, b: (qrow(h, g, b), h))),
        ),
        out_shape=jax.ShapeDtypeStruct((M, NH * HEAD_DIM), BF16),
        compiler_params=_cparams(("arbitrary", "arbitrary", "arbitrary"), 32),
        name="na_attention",
    )(pattern, *([proj] * 9), bias)


_RET_NC = TB // RET_CHUNK
_RET_CTX_NC = CTX // RET_CHUNK


def _ret_kernel(lg_ref, q_ref, k_ref, v_ref, g_ref, cos_ref, sin_ref, o_ref, qr_ref, kr_ref, sf_ref, sb_ref):
    h = pl.program_id(1)
    lgf = lg_ref[0, h]
    lgb = lg_ref[1, h]
    c = RET_CHUNK
    qr_ref[...] = _rope(q_ref[...].astype(F32), cos_ref[...], sin_ref[...], HEAD_DIM // 4)
    kr_ref[...] = _rope(k_ref[...].astype(F32), cos_ref[...], sin_ref[...], HEAD_DIM // 4) * HEAD_DIM ** -0.5

    pos = lax.broadcasted_iota(jnp.int32, (c, 1), 0).astype(F32)
    kdec_f = jnp.exp(lgf * (c - 1.0 - pos))
    kdec_b = jnp.exp(lgb * pos)
    qdec_f = jnp.exp(lgf * (pos + 1.0))
    qdec_b = jnp.exp(lgb * (c - pos))
    gc_f = jnp.exp(lgf * c)
    gc_b = jnp.exp(lgb * c)
    rel = (lax.broadcasted_iota(jnp.int32, (c, c), 0) - lax.broadcasted_iota(jnp.int32, (c, c), 1)).astype(F32)
    dmask = (jnp.where(rel >= 0, jnp.exp(jnp.maximum(rel, 0.0) * lgf), 0.0)
             + jnp.where(rel <= 0, jnp.exp(jnp.maximum(-rel, 0.0) * lgb), 0.0))

    def kv_step(n, carry):
        rows = pl.ds(pl.multiple_of(n * c, c), c)
        kn = kr_ref[rows, :]
        vn = v_ref[rows, :]
        sf_ref[n] = _dot_tn((kn * kdec_f).astype(BF16), vn)
        sb_ref[n] = _dot_tn((kn * kdec_b).astype(BF16), vn)
        return carry

    lax.fori_loop(0, _RET_NC, kv_step, 0, unroll=3)

    def state_step(n, s, s_ref, gc):
        u = s_ref[n]
        s_ref[n] = s
        return s * gc + u

    zero = jnp.zeros((HEAD_DIM, RET_DV), F32)
    lax.fori_loop(0, _RET_NC, lambda n, s: state_step(n, s, sf_ref, gc_f), zero)
    s = lax.fori_loop(0, _RET_CTX_NC, lambda i, s: state_step(_RET_CTX_NC - 1 - i, s, sb_ref, gc_b), zero)
    lax.fori_loop(0, _RET_NC - _RET_CTX_NC, lambda i, s: state_step(_RET_NC - 1 - i, s, sb_ref, gc_b), s)

    def out_step(n, carry):
        rows = pl.ds(pl.multiple_of(n * c, c), c)
        qn = qr_ref[rows, :]
        vn = v_ref[rows, :]
        scores = _dot_nt(qn.astype(BF16), kr_ref[rows, :].astype(BF16)) * dmask
        intra = _dot(scores.astype(BF16), vn)
        qcat = jnp.concatenate([qn * qdec_f, qn * qdec_b], axis=1).astype(BF16)
        scat = jnp.concatenate([sf_ref[n], sb_ref[n]], axis=0).astype(BF16)
        o = intra + _dot(qcat, scat)
        o = o * lax.rsqrt(jnp.mean(o * o, axis=-1, keepdims=True) + EPS)
        o_ref[rows, :] = (o * _silu(g_ref[rows, :].astype(F32))).astype(o_ref.dtype)
        return carry

    lax.fori_loop(0, _RET_NC, out_step, 0, unroll=2)


def _retention(proj, log_g, cos, sin):
    seq128 = lambda blk0: pl.BlockSpec((TB, HEAD_DIM), lambda b, h, lg: (b, blk0 + h))
    seq256 = lambda blk0: pl.BlockSpec((TB, RET_DV), lambda b, h, lg: (b, blk0 + h))
    tab = pl.BlockSpec((TB, HEAD_DIM), lambda b, h, lg: (0, 0))
    return pl.pallas_call(
        _ret_kernel,
        grid_spec=pltpu.PrefetchScalarGridSpec(
            num_scalar_prefetch=1,
            grid=(BATCH, NH),
            in_specs=[seq128(3072 // HEAD_DIM), seq128(4096 // HEAD_DIM),
                      seq256(5120 // RET_DV), seq256(7168 // RET_DV), tab, tab],
            out_specs=pl.BlockSpec((TB, RET_DV), lambda b, h, lg: (b, h)),
            scratch_shapes=[pltpu.VMEM((TB, HEAD_DIM), F32), pltpu.VMEM((TB, HEAD_DIM), F32),
                            pltpu.VMEM((_RET_NC, HEAD_DIM, RET_DV), F32),
                            pltpu.VMEM((_RET_NC, HEAD_DIM, RET_DV), F32)],
        ),
        out_shape=jax.ShapeDtypeStruct((M, NH * RET_DV), BF16),
        compiler_params=_cparams(("arbitrary", "arbitrary"), 48),
        name="retention",
    )(log_g, proj, proj, proj, proj, cos, sin)


def _diff_kernel(lam_ref, q_ref, k_ref, v_ref, cq_ref, sq_ref, ck_ref, sk_ref, o_ref, kb_ref, vb_ref, *, post_scale):
    qi = pl.program_id(2)

    @pl.when(qi == 0)
    def _():
        kb_ref[...] = _rope(k_ref[...].astype(F32), ck_ref[...], sk_ref[...], DIFF_DK // 4).astype(BF16)
        vb_ref[...] = v_ref[...].astype(BF16)
        o_ref[...] = jnp.zeros_like(o_ref)

    @pl.when(qi > 0)
    def _():
        lam = lam_ref[0]
        q = _rope(q_ref[...].astype(F32), cq_ref[...], sq_ref[...], DIFF_DK // 4) * (DIFF_DK ** -0.5 * LOG2E)
        lane = lax.broadcasted_iota(jnp.int32, q.shape, 1)
        kb = kb_ref[...]
        p, inv = [], []
        for t in range(2):
            qt = jnp.where((lane >= DIFF_DK) == (t == 1), q, 0.0).astype(BF16)
            s = _dot_nt(qt, kb)
            e = jnp.exp2(s - jnp.max(s, axis=-1, keepdims=True))
            p.append(e)
            inv.append(1.0 / jnp.sum(e, axis=-1, keepdims=True))
        a = (p[0] * inv[0] - p[1] * (lam * inv[1])).astype(BF16)
        o = _dot(a, vb_ref[...])
        o = o * lax.rsqrt(jnp.mean(o * o, axis=-1, keepdims=True) + EPS)
        o_ref[...] = (o * post_scale).astype(o_ref.dtype)


def _diff_attention(proj, lam, cos, sin, post_scale):
    blk = (RB, HEAD_DIM)
    seq = (TB, HEAD_DIM)
    qmap = lambda b, h, qi, lam_ref: (b * BLOCKS_PER_BATCH + qi, h)
    return pl.pallas_call(
        functools.partial(_diff_kernel, post_scale=post_scale),
        grid_spec=pltpu.PrefetchScalarGridSpec(
            num_scalar_prefetch=1,
            grid=(BATCH, NH, BLOCKS_PER_BATCH),
            in_specs=[pl.BlockSpec(blk, qmap),
                      pl.BlockSpec(seq, lambda b, h, qi, lam_ref: (b, NH + h)),
                      pl.BlockSpec(seq, lambda b, h, qi, lam_ref: (b, 2 * NH + h)),
                      pl.BlockSpec(blk, lambda b, h, qi, lam_ref: (qi, 0)),
                      pl.BlockSpec(blk, lambda b, h, qi, lam_ref: (qi, 0)),
                      pl.BlockSpec(seq, lambda b, h, qi, lam_ref: (0, 0)),
                      pl.BlockSpec(seq, lambda b, h, qi, lam_ref: (0, 0))],
            out_specs=pl.BlockSpec(blk, qmap),
            scratch_shapes=[pltpu.VMEM(seq, BF16), pltpu.VMEM(seq, BF16)],
        ),
        out_shape=jax.ShapeDtypeStruct((M, NH * HEAD_DIM), BF16),
        compiler_params=_cparams(("arbitrary", "arbitrary", "arbitrary"), 48),
        name="diff_attention",
    )(lam, proj, proj, proj, cos, sin, cos, sin)


_HG_NC = TB // HG_CHUNK
_HG_CTX_NC = CTX // HG_CHUNK


def _chunk_cumsum(x, reverse):
    c = x.shape[0]
    row = lax.broadcasted_iota(jnp.int32, x.shape, 0)
    s = 1
    while s < c:
        if reverse:
            x = x + jnp.where(row < c - s, pltpu.roll(x, c - s, axis=0), 0.0)
        else:
            x = x + jnp.where(row >= s, pltpu.roll(x, s, axis=0), 0.0)
        s *= 2
    return x


def _hg_kernel(q_ref, ff_ref, fb_ref, v_ref, g_ref, lb_ref, o_ref,
               qs_ref, kf_ref, kb_ref, cf_ref, cb_ref, sf_ref, sb_ref, df_ref, db_ref):
    c = HG_CHUNK
    lb = lb_ref[0]
    qraw = q_ref[...].astype(F32)
    qs_ref[...] = qraw * _sigmoid(qraw)

    def gate(raw_ref, k_ref, lf_ref):
        f = lb + (1.0 - lb) * _sigmoid(raw_ref[...].astype(F32))
        k_ref[...] = 1.0 - f
        lf_ref[...] = jnp.log(f)

    gate(ff_ref, kf_ref, cf_ref)
    gate(fb_ref, kb_ref, cb_ref)

    def chunk_prep(n, carry):
        rows = pl.ds(pl.multiple_of(n * c, c), c)
        vn = v_ref[rows, :]
        for cum_ref, k_ref, u_ref, d_ref, reverse in ((cf_ref, kf_ref, sf_ref, df_ref, False),
                                                      (cb_ref, kb_ref, sb_ref, db_ref, True)):
            cum = _chunk_cumsum(cum_ref[rows, :], reverse)
            cum_ref[rows, :] = cum
            last = cum[0:1, :] if reverse else cum[c - 1:c, :]
            u_ref[n] = _dot_tn(vn, (k_ref[rows, :] * jnp.exp(last - cum)).astype(BF16))
            d_ref[n] = jnp.exp(last)
        return carry

    lax.fori_loop(0, _HG_NC, chunk_prep, 0, unroll=4)

    def state_step(n, st, s_ref, d_ref):
        u = s_ref[n]
        s_ref[n] = st
        return st * d_ref[n] + u

    zero = jnp.zeros((HEAD_DIM, HEAD_DIM), F32)
    lax.fori_loop(0, _HG_NC, lambda n, s: state_step(n, s, sf_ref, df_ref), zero)
    s = lax.fori_loop(0, _HG_CTX_NC, lambda i, s: state_step(_HG_CTX_NC - 1 - i, s, sb_ref, db_ref), zero)
    lax.fori_loop(0, _HG_NC - _HG_CTX_NC, lambda i, s: state_step(_HG_NC - 1 - i, s, sb_ref, db_ref), s)

    o_ref[0:CTX, :] = jnp.zeros((CTX, HEAD_DIM), o_ref.dtype)

    def level(q, kf, kb, cf, cb, half):
        span = 2 * half
        row = lax.broadcasted_iota(jnp.int32, (c, 1), 0)
        upper = (row % span) >= half
        ref_f = jnp.concatenate([jnp.broadcast_to(cf[b0 + half - 1:b0 + half, :], (span, HEAD_DIM))
                                 for b0 in range(0, c, span)], axis=0)
        ref_b = jnp.concatenate([jnp.broadcast_to(cb[b0 + half:b0 + half + 1, :], (span, HEAD_DIM))
                                 for b0 in range(0, c, span)], axis=0)
        ef = jnp.exp(jnp.minimum(jnp.where(upper, cf - ref_f, ref_f - cf), 0.0))
        eb = jnp.exp(jnp.minimum(jnp.where(upper, ref_b - cb, cb - ref_b), 0.0))
        qq = jnp.concatenate([jnp.where(upper, q * ef, 0.0), jnp.where(upper, 0.0, q * eb)], axis=1)
        kk = jnp.concatenate([jnp.where(upper, 0.0, kf * ef), jnp.where(upper, kb * eb, 0.0)], axis=1)
        a = _dot_nt(qq.astype(BF16), kk.astype(BF16))
        if span == c:
            return a
        ri = lax.broadcasted_iota(jnp.int32, (c, c), 0)
        ci = lax.broadcasted_iota(jnp.int32, (c, c), 1)
        return jnp.where(ri // span == ci // span, a, 0.0)

    def sub_block(q, kf, kb, lo):
        sub = HG_SUB
        row = lax.broadcasted_iota(jnp.int32, (sub, 1), 0)
        lane = lax.broadcasted_iota(jnp.int32, (sub, c), 1)
        qb, kfb, kbb = q[lo:lo + sub], kf[lo:lo + sub], kb[lo:lo + sub]
        ffb, fbb = 1.0 - kfb, 1.0 - kbb
        pf = [None] * sub
        p = jnp.where(row == sub - 1, 1.0, jnp.zeros((sub, HEAD_DIM), F32))
        pf[sub - 1] = p
        for j in range(sub - 2, -1, -1):
            p = jnp.where(row == j, 1.0, p * ffb[j + 1:j + 2])
            pf[j] = p
        att = jnp.zeros((sub, c), F32)
        p = None
        for j in range(sub):
            p = (jnp.where(row == 0, 1.0, jnp.zeros((sub, HEAD_DIM), F32)) if j == 0
                 else jnp.where(row == j, 1.0, p * fbb[j - 1:j]))
            col = jnp.sum(qb * (pf[j] * kfb[j:j + 1] + p * kbb[j:j + 1]), axis=-1, keepdims=True)
            att = jnp.where(lane == lo + j, col, att)
        return att

    def out_step(n, carry):
        rows = pl.ds(pl.multiple_of(n * c, c), c)
        q = qs_ref[rows, :]
        kf = kf_ref[rows, :]
        kb = kb_ref[rows, :]
        cf = cf_ref[rows, :]
        cb = cb_ref[rows, :]
        qcat = jnp.concatenate([q * jnp.exp(cf), q * jnp.exp(cb)], axis=1).astype(BF16)
        scat = jnp.concatenate([sf_ref[n], sb_ref[n]], axis=1).astype(BF16)
        o = _dot_nt(qcat, scat)
        att = jnp.concatenate([sub_block(q, kf, kb, lo) for lo in range(0, c, HG_SUB)], axis=0)
        half = c // 2
        while half >= HG_SUB:
            att = att + level(q, kf, kb, cf, cb, half)
            half //= 2
        o = o + _dot(att.astype(BF16), v_ref[rows, :])
        o = o * lax.rsqrt(jnp.mean(o * o, axis=-1, keepdims=True) + EPS)
        graw = g_ref[rows, :].astype(F32)
        o_ref[rows, :] = (o * (graw * _sigmoid(graw))).astype(o_ref.dtype)
        return carry

    lax.fori_loop(_HG_CTX_NC, _HG_NC, out_step, 0, unroll=4)


def _hgrn(proj, lb):
    seq = lambda col0: pl.BlockSpec((TB, HEAD_DIM), lambda b, h: (b, col0 // HEAD_DIM + h))
    full = pltpu.VMEM((TB, HEAD_DIM), F32)
    states = pltpu.VMEM((_HG_NC, HEAD_DIM, HEAD_DIM), F32)
    decays = pltpu.VMEM((_HG_NC, 1, HEAD_DIM), F32)
    return pl.pallas_call(
        _hg_kernel,
        grid=(BATCH, NH),
        in_specs=[seq(3072), seq(4096), seq(5120), seq(6144), seq(7168),
                  pl.BlockSpec((1, 1, HEAD_DIM), lambda b, h: (h, 0, 0))],
        out_specs=pl.BlockSpec((TB, HEAD_DIM), lambda b, h: (b, h)),
        out_shape=jax.ShapeDtypeStruct((M, NH * HEAD_DIM), BF16),
        scratch_shapes=[full, full, full, full, full, states, states, decays, decays],
        compiler_params=_cparams(("arbitrary", "arbitrary"), 48),
        name="hgrn",
    )(proj, proj, proj, proj, proj, lb.reshape(NH, 1, HEAD_DIM))


def _row_gather_start(idx_ref, k, src_hbm, dst, sem, n):
    for r in range(n):
        pltpu.make_async_copy(src_hbm.at[pl.ds(idx_ref[0, k, r], 1)], dst.at[pl.ds(r, 1)], sem).start()


def _row_gather_wait(src_hbm, dst, sem, n):
    pltpu.make_async_copy(src_hbm.at[pl.ds(0, n)], dst, sem).wait()


def _expert_weights_step(i, be_ref, nx_ref, w_hbms, wst, wsem, wb_refs, layer):
    e = be_ref[i]

    def fetch(ex):
        return [pltpu.make_async_copy(w.at[layer, ex], wst.at[k], wsem.at[k]) for k, w in enumerate(w_hbms)]

    @pl.when(i == 0)
    def _():
        for cp in fetch(e):
            cp.start()

    @pl.when(jnp.logical_or(i == 0, e != be_ref[jnp.maximum(i - 1, 0)]))
    def _():
        for cp in fetch(e):
            cp.wait()
        for k, wb in enumerate(wb_refs):
            wb[...] = wst[k].astype(BF16)
        nx = nx_ref[i]

        @pl.when(nx != e)
        def _():
            for cp in fetch(nx):
                cp.start()


def _moe_up_kernel(be_ref, nx_ref, nu_ref, rt0_ref, rt1_ref, x_hbm, w1_hbm, w3_hbm, o_ref,
                   xbuf, sem, wst, wsem, w1b_ref, w3b_ref, *, layer):
    i = pl.program_id(0)
    nu = nu_ref[0]

    @pl.when(jnp.logical_and(i == 0, nu > 0))
    def _():
        _row_gather_start(rt0_ref, 0, x_hbm, xbuf.at[0], sem.at[0], MOE_TM)

    @pl.when(i + 1 < nu)
    def _():
        slot = (i + 1) % 2
        _row_gather_start(rt1_ref, 0, x_hbm, xbuf.at[slot], sem.at[slot], MOE_TM)

    _expert_weights_step(i, be_ref, nx_ref, (w1_hbm, w3_hbm), wst, wsem, (w1b_ref, w3b_ref), layer)

    @pl.when(i < nu)
    def _():
        slot = i % 2
        _row_gather_wait(x_hbm, xbuf.at[slot], sem.at[slot], MOE_TM)
        x = xbuf[slot].astype(BF16)
        a = _dot(x, w1b_ref[...])
        o_ref[...] = (_silu(a) * _dot(x, w3b_ref[...])).astype(o_ref.dtype)

    @pl.when(i >= nu)
    def _():
        o_ref[...] = jnp.zeros_like(o_ref)


def _moe_up(layer, blk_e, nxt_e, n_used, row_tok, h2, w1, w3):
    rt = row_tok.reshape(MOE_BLOCKS, 1, MOE_TM)
    smem_blk = lambda f: pl.BlockSpec((1, 1, MOE_TM), f, memory_space=pltpu.SMEM)
    hbm = pl.BlockSpec(memory_space=pl.ANY)
    return pl.pallas_call(
        functools.partial(_moe_up_kernel, layer=layer),
        grid_spec=pltpu.PrefetchScalarGridSpec(
            num_scalar_prefetch=3,
            grid=(MOE_BLOCKS,),
            in_specs=[smem_blk(lambda i, be, nx, nu: (i, 0, 0)),
                      smem_blk(lambda i, be, nx, nu: (jnp.minimum(i + 1, MOE_BLOCKS - 1), 0, 0)),
                      hbm, hbm, hbm],
            out_specs=pl.BlockSpec((MOE_TM, EXPERT_FF), lambda i, be, nx, nu: (i, 0)),
            scratch_shapes=[pltpu.VMEM((2, MOE_TM, D), F32), pltpu.SemaphoreType.DMA((2,)),
                            pltpu.VMEM((2, D, EXPERT_FF), F32), pltpu.SemaphoreType.DMA((2,)),
                            pltpu.VMEM((D, EXPERT_FF), BF16), pltpu.VMEM((D, EXPERT_FF), BF16)],
        ),
        out_shape=jax.ShapeDtypeStruct((MOE_ROWS, EXPERT_FF), BF16),
        compiler_params=_cparams(("arbitrary",), 44),
        name="moe_up",
    )(blk_e, nxt_e, n_used, rt, rt, h2, w1, w3)


def _moe_down_kernel(be_ref, nx_ref, nu_ref, h_ref, w2_hbm, o_ref, wst, wsem, w2b_ref, *, layer):
    i = pl.program_id(0)
    _expert_weights_step(i, be_ref, nx_ref, (w2_hbm,), wst, wsem, (w2b_ref,), layer)

    @pl.when(i < nu_ref[0])
    def _():
        o_ref[...] = _dot(h_ref[...], w2b_ref[...])

    @pl.when(i >= nu_ref[0])
    def _():
        o_ref[...] = jnp.zeros_like(o_ref)


def _moe_down(layer, blk_e, nxt_e, n_used, hmid, w2):
    return pl.pallas_call(
        functools.partial(_moe_down_kernel, layer=layer),
        grid_spec=pltpu.PrefetchScalarGridSpec(
            num_scalar_prefetch=3,
            grid=(MOE_BLOCKS,),
            in_specs=[pl.BlockSpec((MOE_TM, EXPERT_FF), lambda i, be, nx, nu: (i, 0)),
                      pl.BlockSpec(memory_space=pl.ANY)],
            out_specs=pl.BlockSpec((MOE_TM, D), lambda i, be, nx, nu: (i, 0)),
            scratch_shapes=[pltpu.VMEM((1, EXPERT_FF, D), F32), pltpu.SemaphoreType.DMA((1,)),
                            pltpu.VMEM((EXPERT_FF, D), BF16)],
        ),
        out_shape=jax.ShapeDtypeStruct((MOE_ROWS, D), F32),
        compiler_params=_cparams(("arbitrary",), 32),
        name="moe_down",
    )(blk_e, nxt_e, n_used, hmid, w2)


def _combine_kernel(p0_ref, p1_ref, y_hbm, x_ref, w_ref, g_ref, mul_ref, add_ref, *rest, last):
    out_refs, (ybuf, sem) = rest[:-2], rest[-2:]
    i = pl.program_id(0)
    n = pl.num_programs(0)

    def start(p_ref, slot):
        for k in range(TOP_K):
            _row_gather_start(p_ref, k, y_hbm, ybuf.at[slot, k], sem.at[slot], RB)

    @pl.when(i == 0)
    def _():
        start(p0_ref, 0)

    @pl.when(i + 1 < n)
    def _():
        start(p1_ref, (i + 1) % 2)

    slot = i % 2
    for k in range(TOP_K):
        _row_gather_wait(y_hbm, ybuf.at[slot, k], sem.at[slot], RB)
    w = w_ref[...]
    y = w[:, 0:1] * ybuf[slot, 0] + w[:, 1:2] * ybuf[slot, 1]
    t = x_ref[...] + g_ref[0] * y
    nrm = t * lax.rsqrt(jnp.mean(t * t, axis=-1, keepdims=True) + EPS) * mul_ref[0] + add_ref[0]
    if last:
        out_refs[0][0] = nrm
    else:
        out_refs[0][...] = t
        out_refs[1][...] = nrm.astype(BF16)


def _moe_combine(tok, y, pos, gates, gate_mod, mul, add, last):
    nb = M // RB
    pblk = pos.reshape(nb, RB, TOP_K).transpose(0, 2, 1)
    row = pl.BlockSpec((RB, D), lambda i: (i, 0))
    mod = pl.BlockSpec((1, 1, D), _mod_index)
    smem_blk = lambda f: pl.BlockSpec((1, TOP_K, RB), f, memory_space=pltpu.SMEM)
    if last:
        out_specs = pl.BlockSpec((1, RB, D), lambda i: (i // BLOCKS_PER_BATCH,
                                                        jnp.maximum(i % BLOCKS_PER_BATCH - 1, 0), 0))
        out_shape = jax.ShapeDtypeStruct((BATCH, SEQ, D), F32)
    else:
        out_specs = [row, row]
        out_shape = [jax.ShapeDtypeStruct((M, D), F32), jax.ShapeDtypeStruct((M, D), BF16)]
    return pl.pallas_call(
        functools.partial(_combine_kernel, last=last),
        grid=(nb,),
        in_specs=[smem_blk(lambda i: (i, 0, 0)), smem_blk(lambda i: (jnp.minimum(i + 1, nb - 1), 0, 0)),
                  pl.BlockSpec(memory_space=pl.ANY), row,
                  pl.BlockSpec((RB, TOP_K), lambda i: (i, 0)), mod, mod, mod],
        out_specs=out_specs,
        out_shape=out_shape,
        scratch_shapes=[pltpu.VMEM((2, TOP_K, RB, D), F32), pltpu.SemaphoreType.DMA((2,))],
        compiler_params=_cparams(("arbitrary",), 48),
        name="moe_combine",
    )(pblk, pblk, y, tok, gates, gate_mod, mul, add)


def _route(logits, valid):
    lg_g = logits[:, :N_GROUPS]
    p_g = jax.nn.softmax(lg_g, axis=-1)
    g_top = jnp.argmax(lg_g, axis=-1)
    w_grp = jnp.take_along_axis(p_g, g_top[:, None], axis=1)
    lg_e = logits[:, N_GROUPS:N_GROUPS + N_EXPERTS].reshape(M, N_GROUPS, EXPERTS_PER_GROUP)
    lg_in = jnp.take_along_axis(lg_e, g_top[:, None, None], axis=1)[:, 0]
    top_v, top_i = lax.top_k(lg_in, TOP_K)
    gates = w_grp * jax.nn.softmax(top_v, axis=-1)
    gates = jnp.where(valid[:, None], gates, 0.0)
    eid = g_top[:, None] * EXPERTS_PER_GROUP + top_i
    eid = jnp.where(valid[:, None], eid, N_EXPERTS).reshape(-1).astype(jnp.int32)
    n_slots = M * TOP_K
    slot = jnp.arange(n_slots, dtype=jnp.int32)
    experts = jnp.arange(N_EXPERTS + 1, dtype=jnp.int32)
    counts = jnp.sum((eid[:, None] == experts[None, :]).astype(jnp.int32), axis=0)
    padded = (counts + MOE_TM - 1) // MOE_TM * MOE_TM
    p_end = jnp.cumsum(padded)
    p_start = p_end - padded
    u_start = jnp.cumsum(counts) - counts
    shift = p_start - u_start
    s_e, order = lax.sort((eid, slot), num_keys=1, is_stable=True)
    dest = slot + jnp.sum(jnp.where(s_e[:, None] == experts[None, :], shift[None, :], 0), axis=1)
    dest = jnp.where(s_e < N_EXPERTS, dest, jnp.clip(order, 0, jnp.maximum(p_end[N_EXPERTS - 1] - 1, 0)))
    _, pos = lax.sort((order, dest), num_keys=1)
    pos = pos.reshape(M, TOP_K)
    blk_start = jnp.arange(MOE_BLOCKS, dtype=jnp.int32) * MOE_TM
    blk_e = jnp.sum((p_end[None, :N_EXPERTS] <= blk_start[:, None]).astype(jnp.int32), axis=1)
    blk_e = jnp.minimum(blk_e, N_EXPERTS - 1)
    later = jnp.min(jnp.where(blk_e[None, :] > blk_e[:, None], blk_e[None, :], N_EXPERTS), axis=1)
    nxt_e = jnp.where(later < N_EXPERTS, later, blk_e)
    onehot_be = blk_e[:, None] == experts[None, :N_EXPERTS]
    blk_shift = jnp.sum(jnp.where(onehot_be, shift[None, :N_EXPERTS], 0), axis=1)
    blk_lim = jnp.sum(jnp.where(onehot_be, (p_start + counts)[None, :N_EXPERTS], 0), axis=1)
    rows = jnp.arange(MOE_ROWS, dtype=jnp.int32).reshape(MOE_BLOCKS, MOE_TM)
    src = jnp.clip(rows - blk_shift[:, None], 0, n_slots - 1)
    row_tok = jnp.where(rows < blk_lim[:, None], jnp.take(order // TOP_K, src), 0)
    n_used = (p_end[N_EXPERTS - 1] // MOE_TM).astype(jnp.int32).reshape(1)
    return gates, row_tok, pos, blk_e, nxt_e, n_used


def _moe(layer, tok, mul, add, gate_mod, wg, bg, we, be, w1, w3, w2, valid, next_mul, next_add, last):
    wr = jnp.zeros((D, ROUTER_PAD), F32).at[:, :N_GROUPS].set(wg).at[:, N_GROUPS:N_GROUPS + N_EXPERTS].set(we)
    w_hi = wr.astype(BF16)
    w_lo = (wr - w_hi.astype(F32)).astype(BF16)
    r_bias = jnp.zeros((1, ROUTER_PAD), F32).at[0, :N_GROUPS].set(bg).at[0, N_GROUPS:N_GROUPS + N_EXPERTS].set(be)
    h2, logits = _norm_router(tok, mul, add, w_hi, w_lo, r_bias)
    gates, row_tok, pos, blk_e, nxt_e, n_used = _route(logits, valid)
    hmid = _moe_up(layer, blk_e, nxt_e, n_used, row_tok, h2, w1, w3)
    y = _moe_down(layer, blk_e, nxt_e, n_used, hmid, w2)
    return _moe_combine(tok, y, pos, gates, gate_mod, next_mul, next_add, last)


def _mod9(t):
    return t[:BATCH + 1].reshape(BATCH + 1, 1, D)


def kernel(x, c, ctx, c_ctx, ada_w, ada_b, norm1_w, norm2_w, w_in_even, w_out_even, na_rpb, ret_decay,
           w_in_odd, w_out_odd, diff_lambda, hg_lb_logits, router_g_w, router_g_b, router_e_w, router_e_b,
           moe_w1, moe_w3, moe_w2, norm_f_w):
    depth = ada_w.shape[0]
    tok = jnp.concatenate([ctx, x], axis=1).reshape(M, D)
    c16 = jnp.concatenate([c, c_ctx[None], jnp.zeros((16 - BATCH - 1, D), F32)], axis=0)
    mod = _ada_mod(c16, ada_w, ada_b)
    lb_all = jnp.cumsum(jax.nn.softmax(hg_lb_logits.astype(F32), axis=0), axis=0)
    lb_all = lb_all - lb_all[0]
    is_latent = (jnp.arange(M) % TB) >= CTX
    all_rows = jnp.ones((M,), bool)

    mods = [[_mod9(t) for t in jnp.split(mod[l], 6, axis=-1)] for l in range(depth)]
    h = _norm_mod(tok, norm1_w[0] * (1.0 + mods[0][1]), mods[0][0])
    for l in range(depth):
        with_ctx = l < depth - 1
        sh1, s1, g1, sh2, s2, g2 = mods[l]
        if with_ctx:
            next_mul, next_add = norm1_w[l + 1] * (1.0 + mods[l + 1][1]), mods[l + 1][0]
        else:
            next_mul = jnp.broadcast_to(norm_f_w.reshape(1, 1, D), (BATCH + 1, 1, D))
            next_add = jnp.zeros((BATCH + 1, 1, D), F32)
        j = l // 2
        if l % 2 == 0:
            proj = _matmul(h, w_in_even[j].astype(BF16), BF16)
            a = _na_attention(proj, _na_bias_tables(na_rpb[j]))
            cos, sin = _rope_tables(HEAD_DIM, 1)
            r = _retention(proj, -jnp.exp(ret_decay[j].astype(F32)), cos, sin)
            w_out = w_out_even[j].astype(BF16)
            tok = _out_proj(a, r, w_out[:NH * HEAD_DIM], w_out[NH * HEAD_DIM:], tok, g1)
        else:
            proj = _matmul(h, w_in_odd[j].astype(BF16), BF16)
            lp = diff_lambda[j].astype(F32)
            lam_init = 0.8 - 0.6 * math.exp(-0.3 * l)
            lam = jnp.exp(jnp.sum(lp[0] * lp[1])) - jnp.exp(jnp.sum(lp[2] * lp[3])) + lam_init
            cos, sin = _rope_tables(DIFF_DK, 2)
            d_l = _diff_attention(proj, lam.reshape(1), cos, sin, 1.0 - lam_init)
            g_l = _hgrn(proj, lb_all[l])
            w_out = w_out_odd[j].astype(BF16)
            tok = _out_proj(d_l, g_l, w_out[:NH * HEAD_DIM], w_out[NH * HEAD_DIM:], tok, g1)
        res = _moe(l, tok, norm2_w[l] * (1.0 + s2), sh2, g2,
                   router_g_w[l], router_g_b[l], router_e_w[l], router_e_b[l],
                   moe_w1, moe_w3, moe_w2, all_rows if with_ctx else is_latent,
                   next_mul, next_add, not with_ctx)
        if with_ctx:
            tok, h = res
    return res
```

```python
import functools
import math

import numpy as np
import jax
import jax.numpy as jnp
from jax import lax
from jax.experimental import pallas as pl
from jax.experimental.pallas import tpu as pltpu

F32 = jnp.float32
BF16 = jnp.bfloat16

D = 2048
BATCH = 8
SEQ = 2048
CTX = 256
TB = SEQ + CTX
M = BATCH * TB
GRID_W = 64
GRID_H = SEQ // GRID_W
HEAD_DIM = 128
NH = 8
NA_WIN_H = 8
NA_WIN_W = 16
RET_DV = 256
RET_CHUNK = 128
DIFF_DK = 64
HG_CHUNK = 64
HG_SUB = 8
N_GROUPS = 4
EXPERTS_PER_GROUP = 8
N_EXPERTS = 32
TOP_K = 2
EXPERT_FF = 1024
ROPE_BASE = 10000.0
EPS = 1e-6
NEG = -1e30
LOG2E = math.log2(math.e)

RB = 256
BLOCKS_PER_BATCH = TB // RB
MOE_TM = 256
MOE_ROWS = M * TOP_K + N_EXPERTS * MOE_TM
MOE_BLOCKS = MOE_ROWS // MOE_TM
ROUTER_PAD = 128


def _cparams(sem, vmem_mb):
    return pltpu.CompilerParams(dimension_semantics=sem, vmem_limit_bytes=vmem_mb * 1024 * 1024)


def _sigmoid(x):
    return 0.5 * jnp.tanh(0.5 * x) + 0.5


def _silu(x):
    return x * _sigmoid(x)


_NT = (((1,), (1,)), ((), ()))
_TN = (((0,), (0,)), ((), ()))


def _dot(a, b):
    return jnp.dot(a, b, preferred_element_type=F32)


def _dot_nt(a, b):
    return lax.dot_general(a, b, _NT, preferred_element_type=F32)


def _dot_tn(a, b):
    return lax.dot_general(a, b, _TN, preferred_element_type=F32)


def _ada_kernel(c_ref, w_ref, b_ref, o_ref):
    a = _silu(c_ref[...]).astype(BF16)
    o_ref[0] = _dot(a, w_ref[0].astype(BF16)) + b_ref[0]


def _ada_mod(c16, ada_w, ada_b):
    depth, _, n = ada_w.shape
    tn = 1024
    return pl.pallas_call(
        _ada_kernel,
        grid=(depth, n // tn),
        in_specs=[
            pl.BlockSpec((16, D), lambda l, j: (0, 0)),
            pl.BlockSpec((1, D, tn), lambda l, j: (l, 0, j)),
            pl.BlockSpec((1, 1, tn), lambda l, j: (l, 0, j)),
        ],
        out_specs=pl.BlockSpec((1, 16, tn), lambda l, j: (l, 0, j)),
        out_shape=jax.ShapeDtypeStruct((depth, 16, n), F32),
        compiler_params=_cparams(("arbitrary", "arbitrary"), 40),
        name="ada_mod",
    )(c16, ada_w, ada_b.reshape(depth, 1, n))


def _mod_index(i):
    return (jnp.where(i % BLOCKS_PER_BATCH == 0, BATCH, i // BLOCKS_PER_BATCH), 0, 0)


def _rms(x):
    return x * lax.rsqrt(jnp.mean(x * x, axis=-1, keepdims=True) + EPS)


def _first_norm_kernel(ctx_ref, x_ref, mul_ref, add_ref, tok_ref, h_ref):
    is_ctx = pl.program_id(0) % BLOCKS_PER_BATCH == 0
    t = jnp.where(is_ctx, ctx_ref[0], x_ref[0])
    tok_ref[...] = t
    h_ref[...] = (_rms(t) * mul_ref[0] + add_ref[0]).astype(BF16)


def _first_norm(x, ctx, mul, add):
    spec_mod = pl.BlockSpec((1, 1, D), _mod_index)
    row = pl.BlockSpec((RB, D), lambda i: (i, 0))
    return pl.pallas_call(
        _first_norm_kernel,
        grid=(M // RB,),
        in_specs=[pl.BlockSpec((1, CTX, D), lambda i: (i // BLOCKS_PER_BATCH, 0, 0)),
                  pl.BlockSpec((1, RB, D), lambda i: (i // BLOCKS_PER_BATCH,
                                                      jnp.maximum(i % BLOCKS_PER_BATCH - 1, 0), 0)),
                  spec_mod, spec_mod],
        out_specs=[row, row],
        out_shape=[jax.ShapeDtypeStruct((M, D), F32), jax.ShapeDtypeStruct((M, D), BF16)],
        compiler_params=_cparams(("arbitrary",), 32),
        name="first_norm",
    )(ctx, x, mul, add)


def _matmul_kernel(a_ref, w_ref, o_ref):
    o_ref[...] = _dot(a_ref[...], w_ref[...]).astype(o_ref.dtype)


def _matmul(a, w, out_dtype):
    m, k = a.shape
    n = w.shape[1]
    tm, tn = TB, 512
    return pl.pallas_call(
        _matmul_kernel,
        grid=(m // tm, n // tn),
        in_specs=[pl.BlockSpec((tm, k), lambda i, j: (i, 0)), pl.BlockSpec((k, tn), lambda i, j: (0, j))],
        out_specs=pl.BlockSpec((tm, tn), lambda i, j: (i, j)),
        out_shape=jax.ShapeDtypeStruct((m, n), out_dtype),
        compiler_params=_cparams(("arbitrary", "arbitrary"), 48),
        name="proj_in",
    )(a, w)


OUT_TM = TB // 8


def _out_proj_kernel(a1_ref, a2_ref, w1_ref, w2_ref, res_ref, gl_ref, gc_ref, ml_ref, mc_ref, al_ref, ac_ref,
                     whi_ref, wlo_ref, rb_ref, tok_ref, h_ref, lg_ref):
    acc = _dot(a1_ref[...], w1_ref[...]) + _dot(a2_ref[...], w2_ref[...])
    row = pl.program_id(0) * OUT_TM + lax.broadcasted_iota(jnp.int32, (OUT_TM, 1), 0)
    is_ctx = row % TB < CTX
    t = res_ref[...] + jnp.where(is_ctx, gc_ref[0], gl_ref[0]) * acc
    tok_ref[...] = t
    h = _rms(t) * jnp.where(is_ctx, mc_ref[0], ml_ref[0]) + jnp.where(is_ctx, ac_ref[0], al_ref[0])
    h_ref[...] = h
    hi = h.astype(BF16)
    lo = (h - hi.astype(F32)).astype(BF16)
    lg_ref[...] = (_dot(hi, whi_ref[...]) + _dot(hi, wlo_ref[...]) + _dot(lo, whi_ref[...])) + rb_ref[...]


def _out_proj(a1, a2, w1, w2, tok, gate, mul, add, w_hi, w_lo, r_bias):
    tm = OUT_TM
    k1, k2 = a1.shape[1], a2.shape[1]
    row = lambda n: pl.BlockSpec((tm, n), lambda i: (i, 0))
    whole = lambda shape: pl.BlockSpec(shape, lambda i: (0,) * len(shape))
    lat = pl.BlockSpec((1, 1, D), lambda i: (i // (TB // tm), 0, 0))
    cx = pl.BlockSpec((1, 1, D), lambda i: (BATCH, 0, 0))
    return pl.pallas_call(
        _out_proj_kernel,
        grid=(M // tm,),
        in_specs=[row(k1), row(k2), whole((k1, D)), whole((k2, D)), row(D), lat, cx, lat, cx, lat, cx,
                  whole((D, ROUTER_PAD)), whole((D, ROUTER_PAD)), whole((1, ROUTER_PAD))],
        out_specs=[row(D), row(D), row(ROUTER_PAD)],
        out_shape=[jax.ShapeDtypeStruct((M, D), F32), jax.ShapeDtypeStruct((M, D), F32),
                   jax.ShapeDtypeStruct((M, ROUTER_PAD), F32)],
        compiler_params=_cparams(("arbitrary",), 56),
        name="proj_out",
    )(a1, a2, w1, w2, tok, gate, gate, mul, mul, add, add, w_hi, w_lo, r_bias)


def _rope_tables(dim, reps):
    n_f = dim // 4
    inv = ROPE_BASE ** (-jnp.arange(n_f, dtype=F32) / n_f)
    t = jnp.arange(SEQ)
    pos = jnp.stack([t // GRID_W, t % GRID_W], axis=-1).astype(F32)
    ang = pos[:, :, None] * inv
    cos = jnp.cos(ang)[:, :, None, :]
    sin = jnp.sin(ang)[:, :, None, :]
    cos = jnp.broadcast_to(cos, (SEQ, 2, 2, n_f)).reshape(SEQ, dim)
    sin = jnp.concatenate([-sin, sin], axis=2).reshape(SEQ, dim)
    cos = jnp.tile(cos, (1, reps))
    sin = jnp.tile(sin, (1, reps))
    cos = jnp.concatenate([jnp.ones((CTX, dim * reps), F32), cos], axis=0)
    sin = jnp.concatenate([jnp.zeros((CTX, dim * reps), F32), sin], axis=0)
    return cos, sin


def _rope(x, cos, sin_signed, n_f):
    lane = lax.broadcasted_iota(jnp.int32, x.shape, x.ndim - 1)
    partner = jnp.where(lane % (2 * n_f) < n_f,
                        pltpu.roll(x, HEAD_DIM - n_f, axis=x.ndim - 1),
                        pltpu.roll(x, n_f, axis=x.ndim - 1))
    return x * cos + partner * sin_signed


_NA_GROUP_ROWS = RB // GRID_W
_NA_KEY_ROWS = 3 * _NA_GROUP_ROWS
_NA_PATTERN = (0, 1, 2, 2, 2, 2, 2, 3, 4)


def _na_key_block(g):
    return jnp.clip(g - 1, 0, GRID_H // _NA_GROUP_ROWS - 3)


def _na_bias_tables(rpb):
    c = np.arange(GRID_W)
    cs = np.clip(c - NA_WIN_W // 2, 0, GRID_W - NA_WIN_W)
    col_valid = (c[None, :] >= cs[:, None]) & (c[None, :] < cs[:, None] + NA_WIN_W)
    co = np.clip(c[None, :] - c[:, None] + NA_WIN_W - 1, 0, 2 * NA_WIN_W - 2)
    onehot = jnp.asarray((co[None] == np.arange(2 * NA_WIN_W - 1)[:, None, None]) & col_valid[None], F32)
    tcol = jnp.einsum("hro,oqk->hrqk", rpb.astype(F32), onehot, precision=lax.Precision.HIGHEST)
    tcol = jnp.where(jnp.asarray(col_valid)[None, None], tcol, NEG)
    neg_blk = jnp.full((NH, GRID_W, GRID_W), NEG, F32)
    tabs = []
    for g in (0, 1, 2, 7):
        r0 = _NA_GROUP_ROWS * g
        ks = int(np.clip(g - 1, 0, GRID_H // _NA_GROUP_ROWS - 3)) * _NA_GROUP_ROWS
        rows = []
        for dr in range(_NA_GROUP_ROWS):
            r = r0 + dr
            rs = int(np.clip(r - NA_WIN_H // 2, 0, GRID_H - NA_WIN_H))
            blks = []
            for krel in range(_NA_KEY_ROWS):
                kr = ks + krel
                blks.append(tcol[:, kr - r + NA_WIN_H - 1] if rs <= kr < rs + NA_WIN_H else neg_blk)
            rows.append(jnp.concatenate(blks, axis=-1))
        tabs.append(jnp.concatenate(rows, axis=-2))
    tabs.append(jnp.full((NH, RB, 3 * RB), NEG, F32))
    return jnp.stack(tabs, axis=1)


_NA_HPB = 2


def _na_kernel(q_ref, k0_ref, k1_ref, k2_ref, v0_ref, v1_ref, v2_ref, kc_ref, vc_ref, b_ref, o_ref):
    for hh in range(_NA_HPB):
        ln = slice(hh * HEAD_DIM, (hh + 1) * HEAD_DIM)
        q = (q_ref[:, ln].astype(F32) * HEAD_DIM ** -0.5).astype(BF16)
        s = [_dot_nt(q, kr[:, ln]) + b_ref[hh, 0, :, t * RB:(t + 1) * RB]
             for t, kr in enumerate((k0_ref, k1_ref, k2_ref))]
        s.append(_dot_nt(q, kc_ref[:, ln]))
        m = functools.reduce(jnp.maximum, [jnp.max(t, axis=-1, keepdims=True) for t in s])
        p = [jnp.exp(t - m) for t in s]
        l = functools.reduce(jnp.add, [jnp.sum(t, axis=-1, keepdims=True) for t in p])
        vs = (v0_ref, v1_ref, v2_ref, vc_ref)
        o = functools.reduce(jnp.add, [_dot(pt.astype(BF16), vr[:, ln]) for pt, vr in zip(p, vs)])
        o_ref[:, ln] = (o / l).astype(o_ref.dtype)


def _na_attention(proj, bias):
    hb = NH // _NA_HPB

    def qrow(h, g, b):
        return b * BLOCKS_PER_BATCH + jnp.where(g < 8, 1 + g, 0)

    def krow(t):
        return lambda h, g, b: (b * BLOCKS_PER_BATCH + 1 + _na_key_block(jnp.minimum(g, 7)) + t, hb + h)

    def vrow(t):
        return lambda h, g, b: (b * BLOCKS_PER_BATCH + 1 + _na_key_block(jnp.minimum(g, 7)) + t, 2 * hb + h)

    blk = (RB, _NA_HPB * HEAD_DIM)
    pattern = jnp.asarray(_NA_PATTERN, jnp.int32)

    def bias_map(h, g, b, pat_ref):
        return (h, pat_ref[g], 0, 0)

    def wrap(f):
        return lambda h, g, b, pat_ref: f(h, g, b)

    in_specs = [pl.BlockSpec(blk, wrap(lambda h, g, b: (qrow(h, g, b), h)))]
    in_specs += [pl.BlockSpec(blk, wrap(krow(t))) for t in range(3)]
    in_specs += [pl.BlockSpec(blk, wrap(vrow(t))) for t in range(3)]
    in_specs += [pl.BlockSpec(blk, wrap(lambda h, g, b: (b * BLOCKS_PER_BATCH, hb + h))),
                 pl.BlockSpec(blk, wrap(lambda h, g, b: (b * BLOCKS_PER_BATCH, 2 * hb + h))),
                 pl.BlockSpec((_NA_HPB, 1, RB, 3 * RB), bias_map)]

    def body(pat_ref, *refs):
        _na_kernel(*refs)

    return pl.pallas_call(
        body,
        grid_spec=pltpu.PrefetchScalarGridSpec(
            num_scalar_prefetch=1,
            grid=(hb, BLOCKS_PER_BATCH, BATCH),
            in_specs=in_specs,
            out_specs=pl.BlockSpec(blk, wrap(lambda h, g, b: (qrow(h, g, b), h))),
        ),
        out_shape=jax.ShapeDtypeStruct((M, NH * HEAD_DIM), BF16),
        compiler_params=_cparams(("arbitrary", "arbitrary", "arbitrary"), 32),
        name="na_attention",
    )(pattern, *([proj] * 9), bias)


_RET_NC = TB // RET_CHUNK
_RET_CTX_NC = CTX // RET_CHUNK


def _ret_kernel(lg_ref, q_ref, k_ref, v_ref, g_ref, cos_ref, sin_ref, o_ref, qr_ref, kr_ref, sf_ref, sb_ref):
    h = pl.program_id(1)
    lgf = lg_ref[0, h]
    lgb = lg_ref[1, h]
    c = RET_CHUNK
    qr_ref[...] = _rope(q_ref[...].astype(F32), cos_ref[...], sin_ref[...], HEAD_DIM // 4)
    kr_ref[...] = _rope(k_ref[...].astype(F32), cos_ref[...], sin_ref[...], HEAD_DIM // 4) * HEAD_DIM ** -0.5

    pos = lax.broadcasted_iota(jnp.int32, (c, 1), 0).astype(F32)
    kdec_f = jnp.exp(lgf * (c - 1.0 - pos))
    kdec_b = jnp.exp(lgb * pos)
    qdec_f = jnp.exp(lgf * (pos + 1.0))
    qdec_b = jnp.exp(lgb * (c - pos))
    gc_f = jnp.exp(lgf * c)
    gc_b = jnp.exp(lgb * c)
    rel = (lax.broadcasted_iota(jnp.int32, (c, c), 0) - lax.broadcasted_iota(jnp.int32, (c, c), 1)).astype(F32)
    dmask = (jnp.where(rel >= 0, jnp.exp(jnp.maximum(rel, 0.0) * lgf), 0.0)
             + jnp.where(rel <= 0, jnp.exp(jnp.maximum(-rel, 0.0) * lgb), 0.0))

    def kv_step(n, carry):
        rows = pl.ds(pl.multiple_of(n * c, c), c)
        kn = kr_ref[rows, :]
        vn = v_ref[rows, :]
        sf_ref[n] = _dot_tn((kn * kdec_f).astype(BF16), vn)
        sb_ref[n] = _dot_tn((kn * kdec_b).astype(BF16), vn)
        return carry

    lax.fori_loop(0, _RET_NC, kv_step, 0, unroll=3)

    def state_step(n, s, s_ref, gc):
        u = s_ref[n]
        s_ref[n] = s
        return s * gc + u

    zero = jnp.zeros((HEAD_DIM, RET_DV), F32)
    lax.fori_loop(0, _RET_NC, lambda n, s: state_step(n, s, sf_ref, gc_f), zero)
    s = lax.fori_loop(0, _RET_CTX_NC, lambda i, s: state_step(_RET_CTX_NC - 1 - i, s, sb_ref, gc_b), zero)
    lax.fori_loop(0, _RET_NC - _RET_CTX_NC, lambda i, s: state_step(_RET_NC - 1 - i, s, sb_ref, gc_b), s)

    def out_step(n, carry):
        rows = pl.ds(pl.multiple_of(n * c, c), c)
        qn = qr_ref[rows, :]
        vn = v_ref[rows, :]
        scores = _dot_nt(qn.astype(BF16), kr_ref[rows, :].astype(BF16)) * dmask
        intra = _dot(scores.astype(BF16), vn)
        qcat = jnp.concatenate([qn * qdec_f, qn * qdec_b], axis=1).astype(BF16)
        scat = jnp.concatenate([sf_ref[n], sb_ref[n]], axis=0).astype(BF16)
        o = intra + _dot(qcat, scat)
        o = o * lax.rsqrt(jnp.mean(o * o, axis=-1, keepdims=True) + EPS)
        o_ref[rows, :] = (o * _silu(g_ref[rows, :].astype(F32))).astype(o_ref.dtype)
        return carry

    lax.fori_loop(0, _RET_NC, out_step, 0, unroll=3)


def _retention(proj, log_g, cos, sin):
    seq128 = lambda blk0: pl.BlockSpec((TB, HEAD_DIM), lambda b, h, lg: (b, blk0 + h))
    seq256 = lambda blk0: pl.BlockSpec((TB, RET_DV), lambda b, h, lg: (b, blk0 + h))
    tab = pl.BlockSpec((TB, HEAD_DIM), lambda b, h, lg: (0, 0))
    return pl.pallas_call(
        _ret_kernel,
        grid_spec=pltpu.PrefetchScalarGridSpec(
            num_scalar_prefetch=1,
            grid=(BATCH, NH),
            in_specs=[seq128(3072 // HEAD_DIM), seq128(4096 // HEAD_DIM),
                      seq256(5120 // RET_DV), seq256(7168 // RET_DV), tab, tab],
            out_specs=pl.BlockSpec((TB, RET_DV), lambda b, h, lg: (b, h)),
            scratch_shapes=[pltpu.VMEM((TB, HEAD_DIM), F32), pltpu.VMEM((TB, HEAD_DIM), F32),
                            pltpu.VMEM((_RET_NC, HEAD_DIM, RET_DV), F32),
                            pltpu.VMEM((_RET_NC, HEAD_DIM, RET_DV), F32)],
        ),
        out_shape=jax.ShapeDtypeStruct((M, NH * RET_DV), BF16),
        compiler_params=_cparams(("arbitrary", "arbitrary"), 48),
        name="retention",
    )(log_g, proj, proj, proj, proj, cos, sin)


def _diff_kernel(lam_ref, q_ref, k_ref, v_ref, cq_ref, sq_ref, ck_ref, sk_ref, o_ref, kb_ref, vb_ref, *, post_scale):
    qi = pl.program_id(2)

    @pl.when(qi == 0)
    def _():
        kb_ref[...] = _rope(k_ref[...].astype(F32), ck_ref[...], sk_ref[...], DIFF_DK // 4).astype(BF16)
        vb_ref[...] = v_ref[...].astype(BF16)
        o_ref[...] = jnp.zeros_like(o_ref)

    @pl.when(qi > 0)
    def _():
        lam = lam_ref[0]
        q = _rope(q_ref[...].astype(F32), cq_ref[...], sq_ref[...], DIFF_DK // 4) * (DIFF_DK ** -0.5 * LOG2E)
        lane = lax.broadcasted_iota(jnp.int32, q.shape, 1)
        kb = kb_ref[...]
        p, inv = [], []
        for t in range(2):
            qt = jnp.where((lane >= DIFF_DK) == (t == 1), q, 0.0).astype(BF16)
            s = _dot_nt(qt, kb)
            e = jnp.exp2(s - jnp.max(s, axis=-1, keepdims=True))
            p.append(e)
            inv.append(1.0 / jnp.sum(e, axis=-1, keepdims=True))
        a = (p[0] * inv[0] - p[1] * (lam * inv[1])).astype(BF16)
        o = _dot(a, vb_ref[...])
        o = o * lax.rsqrt(jnp.mean(o * o, axis=-1, keepdims=True) + EPS)
        o_ref[...] = (o * post_scale).astype(o_ref.dtype)


def _diff_attention(proj, lam, cos, sin, post_scale):
    blk = (RB, HEAD_DIM)
    seq = (TB, HEAD_DIM)
    qmap = lambda b, h, qi, lam_ref: (b * BLOCKS_PER_BATCH + qi, h)
    return pl.pallas_call(
        functools.partial(_diff_kernel, post_scale=post_scale),
        grid_spec=pltpu.PrefetchScalarGridSpec(
            num_scalar_prefetch=1,
            grid=(BATCH, NH, BLOCKS_PER_BATCH),
            in_specs=[pl.BlockSpec(blk, qmap),
                      pl.BlockSpec(seq, lambda b, h, qi, lam_ref: (b, NH + h)),
                      pl.BlockSpec(seq, lambda b, h, qi, lam_ref: (b, 2 * NH + h)),
                      pl.BlockSpec(blk, lambda b, h, qi, lam_ref: (qi, 0)),
                      pl.BlockSpec(blk, lambda b, h, qi, lam_ref: (qi, 0)),
                      pl.BlockSpec(seq, lambda b, h, qi, lam_ref: (0, 0)),
                      pl.BlockSpec(seq, lambda b, h, qi, lam_ref: (0, 0))],
            out_specs=pl.BlockSpec(blk, qmap),
            scratch_shapes=[pltpu.VMEM(seq, BF16), pltpu.VMEM(seq, BF16)],
        ),
        out_shape=jax.ShapeDtypeStruct((M, NH * HEAD_DIM), BF16),
        compiler_params=_cparams(("arbitrary", "arbitrary", "arbitrary"), 48),
        name="diff_attention",
    )(lam, proj, proj, proj, cos, sin, cos, sin)


_HG_NC = TB // HG_CHUNK
_HG_CTX_NC = CTX // HG_CHUNK


def _chunk_cumsum(x, reverse):
    c = x.shape[0]
    row = lax.broadcasted_iota(jnp.int32, x.shape, 0)
    s = 1
    while s < c:
        if reverse:
            x = x + jnp.where(row < c - s, pltpu.roll(x, c - s, axis=0), 0.0)
        else:
            x = x + jnp.where(row >= s, pltpu.roll(x, s, axis=0), 0.0)
        s *= 2
    return x


def _hg_kernel(q_ref, ff_ref, fb_ref, v_ref, g_ref, lb_ref, o_ref,
               qs_ref, kf_ref, kb_ref, cf_ref, cb_ref, sf_ref, sb_ref, df_ref, db_ref):
    c = HG_CHUNK
    lb = lb_ref[0]
    qraw = q_ref[...].astype(F32)
    qs_ref[...] = qraw * _sigmoid(qraw)

    def gate(raw_ref, k_ref, lf_ref):
        f = lb + (1.0 - lb) * _sigmoid(raw_ref[...].astype(F32))
        k_ref[...] = 1.0 - f
        lf_ref[...] = jnp.log(f)

    gate(ff_ref, kf_ref, cf_ref)
    gate(fb_ref, kb_ref, cb_ref)

    def chunk_prep(n, carry):
        rows = pl.ds(pl.multiple_of(n * c, c), c)
        vn = v_ref[rows, :]
        for cum_ref, k_ref, u_ref, d_ref, reverse in ((cf_ref, kf_ref, sf_ref, df_ref, False),
                                                      (cb_ref, kb_ref, sb_ref, db_ref, True)):
            cum = _chunk_cumsum(cum_ref[rows, :], reverse)
            cum_ref[rows, :] = cum
            last = cum[0:1, :] if reverse else cum[c - 1:c, :]
            u_ref[n] = _dot_tn(vn, (k_ref[rows, :] * jnp.exp(last - cum)).astype(BF16))
            d_ref[n] = jnp.exp(last)
        return carry

    lax.fori_loop(0, _HG_NC, chunk_prep, 0, unroll=4)

    def state_step(n, st, s_ref, d_ref):
        u = s_ref[n]
        s_ref[n] = st
        return st * d_ref[n] + u

    zero = jnp.zeros((HEAD_DIM, HEAD_DIM), F32)
    lax.fori_loop(0, _HG_NC, lambda n, s: state_step(n, s, sf_ref, df_ref), zero)
    s = lax.fori_loop(0, _HG_CTX_NC, lambda i, s: state_step(_HG_CTX_NC - 1 - i, s, sb_ref, db_ref), zero)
    lax.fori_loop(0, _HG_NC - _HG_CTX_NC, lambda i, s: state_step(_HG_NC - 1 - i, s, sb_ref, db_ref), s)

    o_ref[0:CTX, :] = jnp.zeros((CTX, HEAD_DIM), o_ref.dtype)

    def level(q, kf, kb, cf, cb, half):
        span = 2 * half
        row = lax.broadcasted_iota(jnp.int32, (c, 1), 0)
        upper = (row % span) >= half
        ref_f = jnp.concatenate([jnp.broadcast_to(cf[b0 + half - 1:b0 + half, :], (span, HEAD_DIM))
                                 for b0 in range(0, c, span)], axis=0)
        ref_b = jnp.concatenate([jnp.broadcast_to(cb[b0 + half:b0 + half + 1, :], (span, HEAD_DIM))
                                 for b0 in range(0, c, span)], axis=0)
        ef = jnp.exp(jnp.minimum(jnp.where(upper, cf - ref_f, ref_f - cf), 0.0))
        eb = jnp.exp(jnp.minimum(jnp.where(upper, ref_b - cb, cb - ref_b), 0.0))
        qq = jnp.concatenate([jnp.where(upper, q * ef, 0.0), jnp.where(upper, 0.0, q * eb)], axis=1)
        kk = jnp.concatenate([jnp.where(upper, 0.0, kf * ef), jnp.where(upper, kb * eb, 0.0)], axis=1)
        a = _dot_nt(qq.astype(BF16), kk.astype(BF16))
        if span == c:
            return a
        ri = lax.broadcasted_iota(jnp.int32, (c, c), 0)
        ci = lax.broadcasted_iota(jnp.int32, (c, c), 1)
        return jnp.where(ri // span == ci // span, a, 0.0)

    def sub_block(q, kf, kb, lo):
        sub = HG_SUB
        row = lax.broadcasted_iota(jnp.int32, (sub, 1), 0)
        lane = lax.broadcasted_iota(jnp.int32, (sub, c), 1)
        qb, kfb, kbb = q[lo:lo + sub], kf[lo:lo + sub], kb[lo:lo + sub]
        ffb, fbb = 1.0 - kfb, 1.0 - kbb
        pf = [None] * sub
        p = jnp.where(row == sub - 1, 1.0, jnp.zeros((sub, HEAD_DIM), F32))
        pf[sub - 1] = p
        for j in range(sub - 2, -1, -1):
            p = jnp.where(row == j, 1.0, p * ffb[j + 1:j + 2])
            pf[j] = p
        att = jnp.zeros((sub, c), F32)
        p = None
        for j in range(sub):
            p = (jnp.where(row == 0, 1.0, jnp.zeros((sub, HEAD_DIM), F32)) if j == 0
                 else jnp.where(row == j, 1.0, p * fbb[j - 1:j]))
            col = jnp.sum(qb * (pf[j] * kfb[j:j + 1] + p * kbb[j:j + 1]), axis=-1, keepdims=True)
            att = jnp.where(lane == lo + j, col, att)
        return att

    def out_step(n, carry):
        rows = pl.ds(pl.multiple_of(n * c, c), c)
        q = qs_ref[rows, :]
        kf = kf_ref[rows, :]
        kb = kb_ref[rows, :]
        cf = cf_ref[rows, :]
        cb = cb_ref[rows, :]
        qcat = jnp.concatenate([q * jnp.exp(cf), q * jnp.exp(cb)], axis=1).astype(BF16)
        scat = jnp.concatenate([sf_ref[n], sb_ref[n]], axis=1).astype(BF16)
        o = _dot_nt(qcat, scat)
        att = jnp.concatenate([sub_block(q, kf, kb, lo) for lo in range(0, c, HG_SUB)], axis=0)
        half = c // 2
        while half >= HG_SUB:
            att = att + level(q, kf, kb, cf, cb, half)
            half //= 2
        o = o + _dot(att.astype(BF16), v_ref[rows, :])
        o = o * lax.rsqrt(jnp.mean(o * o, axis=-1, keepdims=True) + EPS)
        graw = g_ref[rows, :].astype(F32)
        o_ref[rows, :] = (o * (graw * _sigmoid(graw))).astype(o_ref.dtype)
        return carry

    lax.fori_loop(_HG_CTX_NC, _HG_NC, out_step, 0, unroll=4)


def _hgrn(proj, lb):
    seq = lambda col0: pl.BlockSpec((TB, HEAD_DIM), lambda b, h: (b, col0 // HEAD_DIM + h))
    full = pltpu.VMEM((TB, HEAD_DIM), F32)
    states = pltpu.VMEM((_HG_NC, HEAD_DIM, HEAD_DIM), F32)
    decays = pltpu.VMEM((_HG_NC, 1, HEAD_DIM), F32)
    return pl.pallas_call(
        _hg_kernel,
        grid=(BATCH, NH),
        in_specs=[seq(3072), seq(4096), seq(5120), seq(6144), seq(7168),
                  pl.BlockSpec((1, 1, HEAD_DIM), lambda b, h: (h, 0, 0))],
        out_specs=pl.BlockSpec((TB, HEAD_DIM), lambda b, h: (b, h)),
        out_shape=jax.ShapeDtypeStruct((M, NH * HEAD_DIM), BF16),
        scratch_shapes=[full, full, full, full, full, states, states, decays, decays],
        compiler_params=_cparams(("arbitrary", "arbitrary"), 48),
        name="hgrn",
    )(proj, proj, proj, proj, proj, lb.reshape(NH, 1, HEAD_DIM))


def _row_gather_start(idx_ref, k, src_hbm, dst, sem, n):
    for r in range(n):
        pltpu.make_async_copy(src_hbm.at[pl.ds(idx_ref[0, k, r], 1)], dst.at[pl.ds(r, 1)], sem).start()


def _row_gather_wait(src_hbm, dst, sem, n):
    pltpu.make_async_copy(src_hbm.at[pl.ds(0, n)], dst, sem).wait()


def _expert_weights_step(i, be_ref, nx_ref, w_hbms, wst, wsem, wb_refs, layer):
    e = be_ref[i]

    def fetch(ex):
        return [pltpu.make_async_copy(w.at[layer, ex], wst.at[k], wsem.at[k]) for k, w in enumerate(w_hbms)]

    @pl.when(i == 0)
    def _():
        for cp in fetch(e):
            cp.start()

    @pl.when(jnp.logical_or(i == 0, e != be_ref[jnp.maximum(i - 1, 0)]))
    def _():
        for cp in fetch(e):
            cp.wait()
        for k, wb in enumerate(wb_refs):
            wb[...] = wst[k].astype(BF16)
        nx = nx_ref[i]

        @pl.when(nx != e)
        def _():
            for cp in fetch(nx):
                cp.start()


def _moe_up_kernel(be_ref, nx_ref, nu_ref, rt0_ref, rt1_ref, x_hbm, w1_hbm, w3_hbm, o_ref,
                   xbuf, sem, wst, wsem, w1b_ref, w3b_ref, *, layer):
    i = pl.program_id(0)
    nu = nu_ref[0]

    @pl.when(jnp.logical_and(i == 0, nu > 0))
    def _():
        _row_gather_start(rt0_ref, 0, x_hbm, xbuf.at[0], sem.at[0], MOE_TM)

    @pl.when(i + 1 < nu)
    def _():
        slot = (i + 1) % 2
        _row_gather_start(rt1_ref, 0, x_hbm, xbuf.at[slot], sem.at[slot], MOE_TM)

    _expert_weights_step(i, be_ref, nx_ref, (w1_hbm, w3_hbm), wst, wsem, (w1b_ref, w3b_ref), layer)

    @pl.when(i < nu)
    def _():
        slot = i % 2
        _row_gather_wait(x_hbm, xbuf.at[slot], sem.at[slot], MOE_TM)
        x = xbuf[slot].astype(BF16)
        a = _dot(x, w1b_ref[...])
        o_ref[...] = (_silu(a) * _dot(x, w3b_ref[...])).astype(o_ref.dtype)

    @pl.when(i >= nu)
    def _():
        o_ref[...] = jnp.zeros_like(o_ref)


def _moe_up(layer, blk_e, nxt_e, n_used, row_tok, h2, w1, w3):
    rt = row_tok.reshape(MOE_BLOCKS, 1, MOE_TM)
    smem_blk = lambda f: pl.BlockSpec((1, 1, MOE_TM), f, memory_space=pltpu.SMEM)
    hbm = pl.BlockSpec(memory_space=pl.ANY)
    return pl.pallas_call(
        functools.partial(_moe_up_kernel, layer=layer),
        grid_spec=pltpu.PrefetchScalarGridSpec(
            num_scalar_prefetch=3,
            grid=(MOE_BLOCKS,),
            in_specs=[smem_blk(lambda i, be, nx, nu: (i, 0, 0)),
                      smem_blk(lambda i, be, nx, nu: (jnp.minimum(i + 1, MOE_BLOCKS - 1), 0, 0)),
                      hbm, hbm, hbm],
            out_specs=pl.BlockSpec((MOE_TM, EXPERT_FF), lambda i, be, nx, nu: (i, 0)),
            scratch_shapes=[pltpu.VMEM((2, MOE_TM, D), F32), pltpu.SemaphoreType.DMA((2,)),
                            pltpu.VMEM((2, D, EXPERT_FF), F32), pltpu.SemaphoreType.DMA((2,)),
                            pltpu.VMEM((D, EXPERT_FF), BF16), pltpu.VMEM((D, EXPERT_FF), BF16)],
        ),
        out_shape=jax.ShapeDtypeStruct((MOE_ROWS, EXPERT_FF), BF16),
        compiler_params=_cparams(("arbitrary",), 44),
        name="moe_up",
    )(blk_e, nxt_e, n_used, rt, rt, h2, w1, w3)


def _moe_down_kernel(be_ref, nx_ref, nu_ref, h_ref, w2_hbm, o_ref, wst, wsem, w2b_ref, *, layer):
    i = pl.program_id(0)
    _expert_weights_step(i, be_ref, nx_ref, (w2_hbm,), wst, wsem, (w2b_ref,), layer)

    @pl.when(i < nu_ref[0])
    def _():
        o_ref[...] = _dot(h_ref[...], w2b_ref[...])

    @pl.when(i >= nu_ref[0])
    def _():
        o_ref[...] = jnp.zeros_like(o_ref)


def _moe_down(layer, blk_e, nxt_e, n_used, hmid, w2):
    return pl.pallas_call(
        functools.partial(_moe_down_kernel, layer=layer),
        grid_spec=pltpu.PrefetchScalarGridSpec(
            num_scalar_prefetch=3,
            grid=(MOE_BLOCKS,),
            in_specs=[pl.BlockSpec((MOE_TM, EXPERT_FF), lambda i, be, nx, nu: (i, 0)),
                      pl.BlockSpec(memory_space=pl.ANY)],
            out_specs=pl.BlockSpec((MOE_TM, D), lambda i, be, nx, nu: (i, 0)),
            scratch_shapes=[pltpu.VMEM((1, EXPERT_FF, D), F32), pltpu.SemaphoreType.DMA((1,)),
                            pltpu.VMEM((EXPERT_FF, D), BF16)],
        ),
        out_shape=jax.ShapeDtypeStruct((MOE_ROWS, D), F32),
        compiler_params=_cparams(("arbitrary",), 32),
        name="moe_down",
    )(blk_e, nxt_e, n_used, hmid, w2)


def _combine_kernel(p0_ref, p1_ref, y_hbm, x_ref, w_ref, g_ref, mul_ref, add_ref, *rest, last):
    out_refs, (ybuf, sem) = rest[:-2], rest[-2:]
    i = pl.program_id(0)
    n = pl.num_programs(0)

    def start(p_ref, slot):
        for k in range(TOP_K):
            _row_gather_start(p_ref, k, y_hbm, ybuf.at[slot, k], sem.at[slot], RB)

    @pl.when(i == 0)
    def _():
        start(p0_ref, 0)

    @pl.when(i + 1 < n)
    def _():
        start(p1_ref, (i + 1) % 2)

    slot = i % 2
    for k in range(TOP_K):
        _row_gather_wait(y_hbm, ybuf.at[slot, k], sem.at[slot], RB)
    w = w_ref[...]
    y = w[:, 0:1] * ybuf[slot, 0] + w[:, 1:2] * ybuf[slot, 1]
    t = x_ref[...] + g_ref[0] * y
    nrm = t * lax.rsqrt(jnp.mean(t * t, axis=-1, keepdims=True) + EPS) * mul_ref[0] + add_ref[0]
    if last:
        out_refs[0][0] = nrm
    else:
        out_refs[0][...] = t
        out_refs[1][...] = nrm.astype(BF16)


def _moe_combine(tok, y, pos, gates, gate_mod, mul, add, last):
    nb = M // RB
    pblk = pos.reshape(nb, RB, TOP_K).transpose(0, 2, 1)
    row = pl.BlockSpec((RB, D), lambda i: (i, 0))
    mod = pl.BlockSpec((1, 1, D), _mod_index)
    smem_blk = lambda f: pl.BlockSpec((1, TOP_K, RB), f, memory_space=pltpu.SMEM)
    if last:
        out_specs = pl.BlockSpec((1, RB, D), lambda i: (i // BLOCKS_PER_BATCH,
                                                        jnp.maximum(i % BLOCKS_PER_BATCH - 1, 0), 0))
        out_shape = jax.ShapeDtypeStruct((BATCH, SEQ, D), F32)
    else:
        out_specs = [row, row]
        out_shape = [jax.ShapeDtypeStruct((M, D), F32), jax.ShapeDtypeStruct((M, D), BF16)]
    return pl.pallas_call(
        functools.partial(_combine_kernel, last=last),
        grid=(nb,),
        in_specs=[smem_blk(lambda i: (i, 0, 0)), smem_blk(lambda i: (jnp.minimum(i + 1, nb - 1), 0, 0)),
                  pl.BlockSpec(memory_space=pl.ANY), row,
                  pl.BlockSpec((RB, TOP_K), lambda i: (i, 0)), mod, mod, mod],
        out_specs=out_specs,
        out_shape=out_shape,
        scratch_shapes=[pltpu.VMEM((2, TOP_K, RB, D), F32), pltpu.SemaphoreType.DMA((2,))],
        compiler_params=_cparams(("arbitrary",), 48),
        name="moe_combine",
    )(pblk, pblk, y, tok, gates, gate_mod, mul, add)


def _route(logits, valid):
    lg_g = logits[:, :N_GROUPS]
    p_g = jax.nn.softmax(lg_g, axis=-1)
    g_top = jnp.argmax(lg_g, axis=-1)
    w_grp = jnp.take_along_axis(p_g, g_top[:, None], axis=1)
    lg_e = logits[:, N_GROUPS:N_GROUPS + N_EXPERTS].reshape(M, N_GROUPS, EXPERTS_PER_GROUP)
    lg_in = jnp.take_along_axis(lg_e, g_top[:, None, None], axis=1)[:, 0]
    top_v, top_i = lax.top_k(lg_in, TOP_K)
    gates = w_grp * jax.nn.softmax(top_v, axis=-1)
    gates = jnp.where(valid[:, None], gates, 0.0)
    eid = g_top[:, None] * EXPERTS_PER_GROUP + top_i
    eid = jnp.where(valid[:, None], eid, N_EXPERTS).reshape(-1).astype(jnp.int32)
    n_slots = M * TOP_K
    slot = jnp.arange(n_slots, dtype=jnp.int32)
    experts = jnp.arange(N_EXPERTS + 1, dtype=jnp.int32)
    counts = jnp.sum((eid[:, None] == experts[None, :]).astype(jnp.int32), axis=0)
    padded = (counts + MOE_TM - 1) // MOE_TM * MOE_TM
    p_end = jnp.cumsum(padded)
    p_start = p_end - padded
    u_start = jnp.cumsum(counts) - counts
    shift = p_start - u_start
    s_e, order = lax.sort((eid, slot), num_keys=1, is_stable=True)
    dest = slot + jnp.sum(jnp.where(s_e[:, None] == experts[None, :], shift[None, :], 0), axis=1)
    dest = jnp.where(s_e < N_EXPERTS, dest, jnp.clip(order, 0, jnp.maximum(p_end[N_EXPERTS - 1] - 1, 0)))
    _, pos = lax.sort((order, dest), num_keys=1)
    pos = pos.reshape(M, TOP_K)
    blk_start = jnp.arange(MOE_BLOCKS, dtype=jnp.int32) * MOE_TM
    blk_e = jnp.sum((p_end[None, :N_EXPERTS] <= blk_start[:, None]).astype(jnp.int32), axis=1)
    blk_e = jnp.minimum(blk_e, N_EXPERTS - 1)
    later = jnp.min(jnp.where(blk_e[None, :] > blk_e[:, None], blk_e[None, :], N_EXPERTS), axis=1)
    nxt_e = jnp.where(later < N_EXPERTS, later, blk_e)
    onehot_be = blk_e[:, None] == experts[None, :N_EXPERTS]
    blk_shift = jnp.sum(jnp.where(onehot_be, shift[None, :N_EXPERTS], 0), axis=1)
    blk_lim = jnp.sum(jnp.where(onehot_be, (p_start + counts)[None, :N_EXPERTS], 0), axis=1)
    rows = jnp.arange(MOE_ROWS, dtype=jnp.int32).reshape(MOE_BLOCKS, MOE_TM)
    src = jnp.clip(rows - blk_shift[:, None], 0, n_slots - 1)
    row_tok = jnp.where(rows < blk_lim[:, None], jnp.take(order // TOP_K, src), 0)
    n_used = (p_end[N_EXPERTS - 1] // MOE_TM).astype(jnp.int32).reshape(1)
    return gates, row_tok, pos, blk_e, nxt_e, n_used


def _router_tables(wg, bg, we, be):
    wr = jnp.zeros((D, ROUTER_PAD), F32).at[:, :N_GROUPS].set(wg).at[:, N_GROUPS:N_GROUPS + N_EXPERTS].set(we)
    w_hi = wr.astype(BF16)
    w_lo = (wr - w_hi.astype(F32)).astype(BF16)
    r_bias = jnp.zeros((1, ROUTER_PAD), F32).at[0, :N_GROUPS].set(bg).at[0, N_GROUPS:N_GROUPS + N_EXPERTS].set(be)
    return w_hi, w_lo, r_bias


def _moe(layer, tok, h2, logits, gate_mod, w1, w3, w2, valid, next_mul, next_add, last):
    gates, row_tok, pos, blk_e, nxt_e, n_used = _route(logits, valid)
    hmid = _moe_up(layer, blk_e, nxt_e, n_used, row_tok, h2, w1, w3)
    y = _moe_down(layer, blk_e, nxt_e, n_used, hmid, w2)
    return _moe_combine(tok, y, pos, gates, gate_mod, next_mul, next_add, last)


def _mod9(t):
    return t[:BATCH + 1].reshape(BATCH + 1, 1, D)


def kernel(x, c, ctx, c_ctx, ada_w, ada_b, norm1_w, norm2_w, w_in_even, w_out_even, na_rpb, ret_decay,
           w_in_odd, w_out_odd, diff_lambda, hg_lb_logits, router_g_w, router_g_b, router_e_w, router_e_b,
           moe_w1, moe_w3, moe_w2, norm_f_w):
    depth = ada_w.shape[0]
    c16 =jnp.concatenate([c, c_ctx[None], jnp.zeros((16 - BATCH - 1, D), F32)], axis=0)
    mod = _ada_mod(c16, ada_w, ada_b)
    lb_all = jnp.cumsum(jax.nn.softmax(hg_lb_logits.astype(F32), axis=0), axis=0)
    lb_all = lb_all - lb_all[0]
    is_latent = (jnp.arange(M) % TB) >= CTX
    all_rows = jnp.ones((M,), bool)

    mods = [[_mod9(t) for t in jnp.split(mod[l], 6, axis=-1)] for l in range(depth)]
    tok, h = _first_norm(x, ctx, norm1_w[0] * (1.0 + mods[0][1]), mods[0][0])
    for l in range(depth):
        with_ctx = l < depth - 1
        sh1, s1, g1, sh2, s2, g2 = mods[l]
        if with_ctx:
            next_mul, next_add = norm1_w[l + 1] * (1.0 + mods[l + 1][1]), mods[l + 1][0]
        else:
            next_mul = jnp.broadcast_to(norm_f_w.reshape(1, 1, D), (BATCH + 1, 1, D))
            next_add = jnp.zeros((BATCH + 1, 1, D), F32)
        j = l // 2
        if l % 2 == 0:
            proj = _matmul(h, w_in_even[j].astype(BF16), BF16)
            m1 = _na_attention(proj, _na_bias_tables(na_rpb[j]))
            cos, sin = _rope_tables(HEAD_DIM, 1)
            m2 = _retention(proj, -jnp.exp(ret_decay[j].astype(F32)), cos, sin)
            w_out = w_out_even[j].astype(BF16)
        else:
            proj = _matmul(h, w_in_odd[j].astype(BF16), BF16)
            lp = diff_lambda[j].astype(F32)
            lam_init = 0.8 - 0.6 * math.exp(-0.3 * l)
            lam = jnp.exp(jnp.sum(lp[0] * lp[1])) - jnp.exp(jnp.sum(lp[2] * lp[3])) + lam_init
            cos, sin = _rope_tables(DIFF_DK, 2)
            m1 = _diff_attention(proj, lam.reshape(1), cos, sin, 1.0 - lam_init)
            m2 = _hgrn(proj, lb_all[l])
            w_out = w_out_odd[j].astype(BF16)
        tok, h2, logits = _out_proj(m1, m2, w_out[:NH * HEAD_DIM], w_out[NH * HEAD_DIM:], tok, g1,
                                    norm2_w[l] * (1.0 + s2), sh2,
                                    *_router_tables(router_g_w[l], router_g_b[l], router_e_w[l], router_e_b[l]))
        res = _moe(l, tok, h2, logits, g2, moe_w1, moe_w3, moe_w2, all_rows if with_ctx else is_latent,
                   next_mul, next_add, not with_ctx)
        if with_ctx:
            tok, h = res
    return res
```

```python
import functools
import math

import numpy as np
import jax
import jax.numpy as jnp
from jax import lax
from jax.experimental import pallas as pl
from jax.experimental.pallas import tpu as pltpu

F32 = jnp.float32
BF16 = jnp.bfloat16

D = 2048
BATCH = 8
SEQ = 2048
CTX = 256
TB = SEQ + CTX
M = BATCH * TB
GRID_W = 64
GRID_H = SEQ // GRID_W
HEAD_DIM = 128
NH = 8
NA_WIN_H = 8
NA_WIN_W = 16
RET_DV = 256
RET_CHUNK = 128
DIFF_DK = 64
HG_CHUNK = 64
HG_SUB = 8
N_GROUPS = 4
EXPERTS_PER_GROUP = 8
N_EXPERTS = 32
TOP_K = 2
EXPERT_FF = 1024
ROPE_BASE = 10000.0
EPS = 1e-6
NEG = -1e30
LOG2E = math.log2(math.e)

RB = 256
BLOCKS_PER_BATCH = TB // RB
MOE_TM = 256
MOE_ROWS = M * TOP_K + N_EXPERTS * MOE_TM
MOE_BLOCKS = MOE_ROWS // MOE_TM
ROUTER_PAD = 128


def _cparams(sem, vmem_mb):
    return pltpu.CompilerParams(dimension_semantics=sem, vmem_limit_bytes=vmem_mb * 1024 * 1024)


def _sigmoid(x):
    return 0.5 * jnp.tanh(0.5 * x) + 0.5


def _silu(x):
    return x * _sigmoid(x)


_NT = (((1,), (1,)), ((), ()))
_TN = (((0,), (0,)), ((), ()))


def _dot(a, b):
    return jnp.dot(a, b, preferred_element_type=F32)


def _dot_nt(a, b):
    return lax.dot_general(a, b, _NT, preferred_element_type=F32)


def _dot_tn(a, b):
    return lax.dot_general(a, b, _TN, preferred_element_type=F32)


def _ada_kernel(c_ref, w_ref, b_ref, o_ref):
    a = _silu(c_ref[...]).astype(BF16)
    o_ref[0] = _dot(a, w_ref[0].astype(BF16)) + b_ref[0]


def _ada_mod(c16, ada_w, ada_b):
    depth, _, n = ada_w.shape
    tn = 1024
    return pl.pallas_call(
        _ada_kernel,
        grid=(depth, n // tn),
        in_specs=[
            pl.BlockSpec((16, D), lambda l, j: (0, 0)),
            pl.BlockSpec((1, D, tn), lambda l, j: (l, 0, j)),
            pl.BlockSpec((1, 1, tn), lambda l, j: (l, 0, j)),
        ],
        out_specs=pl.BlockSpec((1, 16, tn), lambda l, j: (l, 0, j)),
        out_shape=jax.ShapeDtypeStruct((depth, 16, n), F32),
        compiler_params=_cparams(("arbitrary", "arbitrary"), 40),
        name="ada_mod",
    )(c16, ada_w, ada_b.reshape(depth, 1, n))


def _mod_index(i):
    return (jnp.where(i % BLOCKS_PER_BATCH == 0, BATCH, i // BLOCKS_PER_BATCH), 0, 0)


def _rms(x):
    return x * lax.rsqrt(jnp.mean(x * x, axis=-1, keepdims=True) + EPS)


def _first_norm_kernel(ctx_ref, x_ref, mul_ref, add_ref, tok_ref, h_ref):
    is_ctx = pl.program_id(0) % BLOCKS_PER_BATCH == 0
    t = jnp.where(is_ctx, ctx_ref[0], x_ref[0])
    tok_ref[...] = t
    h_ref[...] = (_rms(t) * mul_ref[0] + add_ref[0]).astype(BF16)


def _first_norm(x, ctx, mul, add):
    spec_mod = pl.BlockSpec((1, 1, D), _mod_index)
    row = pl.BlockSpec((RB, D), lambda i: (i, 0))
    return pl.pallas_call(
        _first_norm_kernel,
        grid=(M // RB,),
        in_specs=[pl.BlockSpec((1, CTX, D), lambda i: (i // BLOCKS_PER_BATCH, 0, 0)),
                  pl.BlockSpec((1, RB, D), lambda i: (i // BLOCKS_PER_BATCH,
                                                      jnp.maximum(i % BLOCKS_PER_BATCH - 1, 0), 0)),
                  spec_mod, spec_mod],
        out_specs=[row, row],
        out_shape=[jax.ShapeDtypeStruct((M, D), F32), jax.ShapeDtypeStruct((M, D), BF16)],
        compiler_params=_cparams(("arbitrary",), 32),
        name="first_norm",
    )(ctx, x, mul, add)


def _matmul_kernel(a_ref, w_ref, o_ref):
    o_ref[...] = _dot(a_ref[...], w_ref[...]).astype(o_ref.dtype)


def _matmul(a, w, out_dtype):
    m, k = a.shape
    n = w.shape[1]
    tm, tn = TB, 512
    return pl.pallas_call(
        _matmul_kernel,
        grid=(m // tm, n // tn),
        in_specs=[pl.BlockSpec((tm, k), lambda i, j: (i, 0)), pl.BlockSpec((k, tn), lambda i, j: (0, j))],
        out_specs=pl.BlockSpec((tm, tn), lambda i, j: (i, j)),
        out_shape=jax.ShapeDtypeStruct((m, n), out_dtype),
        compiler_params=_cparams(("arbitrary", "arbitrary"), 48),
        name="proj_in",
    )(a, w)


OUT_TM = TB // 8


def _out_proj_kernel(a1_ref, a2_ref, w1_ref, w2_ref, res_ref, gl_ref, gc_ref, ml_ref, mc_ref, al_ref, ac_ref,
                     whi_ref, wlo_ref, rb_ref, tok_ref, h_ref, lg_ref):
    acc = _dot(a1_ref[...], w1_ref[...]) + _dot(a2_ref[...], w2_ref[...])
    row = pl.program_id(0) * OUT_TM + lax.broadcasted_iota(jnp.int32, (OUT_TM, 1), 0)
    is_ctx = row % TB < CTX
    t = res_ref[...] + jnp.where(is_ctx, gc_ref[0], gl_ref[0]) * acc
    tok_ref[...] = t
    h = _rms(t) * jnp.where(is_ctx, mc_ref[0], ml_ref[0]) + jnp.where(is_ctx, ac_ref[0], al_ref[0])
    h_ref[...] = h
    hi = h.astype(BF16)
    lo = (h - hi.astype(F32)).astype(BF16)
    lg_ref[...] = (_dot(hi, whi_ref[...]) + _dot(hi, wlo_ref[...]) + _dot(lo, whi_ref[...])) + rb_ref[...]


def _out_proj(a1, a2, w1, w2, tok, gate, mul, add, w_hi, w_lo, r_bias):
    tm = OUT_TM
    k1, k2 = a1.shape[1], a2.shape[1]
    row = lambda n: pl.BlockSpec((tm, n), lambda i: (i, 0))
    whole = lambda shape: pl.BlockSpec(shape, lambda i: (0,) * len(shape))
    lat = pl.BlockSpec((1, 1, D), lambda i: (i // (TB // tm), 0, 0))
    cx = pl.BlockSpec((1, 1, D), lambda i: (BATCH, 0, 0))
    return pl.pallas_call(
        _out_proj_kernel,
        grid=(M // tm,),
        in_specs=[row(k1), row(k2), whole((k1, D)), whole((k2, D)), row(D), lat, cx, lat, cx, lat, cx,
                  whole((D, ROUTER_PAD)), whole((D, ROUTER_PAD)), whole((1, ROUTER_PAD))],
        out_specs=[row(D), row(D), row(ROUTER_PAD)],
        out_shape=[jax.ShapeDtypeStruct((M, D), F32), jax.ShapeDtypeStruct((M, D), F32),
                   jax.ShapeDtypeStruct((M, ROUTER_PAD), F32)],
        compiler_params=_cparams(("arbitrary",), 56),
        name="proj_out",
    )(a1, a2, w1, w2, tok, gate, gate, mul, mul, add, add, w_hi, w_lo, r_bias)


def _rope_tables(dim, reps):
    n_f = dim // 4
    inv = ROPE_BASE ** (-jnp.arange(n_f, dtype=F32) / n_f)
    t = jnp.arange(SEQ)
    pos = jnp.stack([t // GRID_W, t % GRID_W], axis=-1).astype(F32)
    ang = pos[:, :, None] * inv
    cos = jnp.cos(ang)[:, :, None, :]
    sin = jnp.sin(ang)[:, :, None, :]
    cos = jnp.broadcast_to(cos, (SEQ, 2, 2, n_f)).reshape(SEQ, dim)
    sin = jnp.concatenate([-sin, sin], axis=2).reshape(SEQ, dim)
    cos = jnp.tile(cos, (1, reps))
    sin = jnp.tile(sin, (1, reps))
    cos = jnp.concatenate([jnp.ones((CTX, dim * reps), F32), cos], axis=0)
    sin = jnp.concatenate([jnp.zeros((CTX, dim * reps), F32), sin], axis=0)
    return cos, sin


def _rope(x, cos, sin_signed, n_f):
    lane = lax.broadcasted_iota(jnp.int32, x.shape, x.ndim - 1)
    partner = jnp.where(lane % (2 * n_f) < n_f,
                        pltpu.roll(x, HEAD_DIM - n_f, axis=x.ndim - 1),
                        pltpu.roll(x, n_f, axis=x.ndim - 1))
    return x * cos + partner * sin_signed


_NA_GROUP_ROWS = RB // GRID_W
_NA_KEY_ROWS = 3 * _NA_GROUP_ROWS
_NA_PATTERN = (0, 1, 2, 2, 2, 2, 2, 3, 4)


def _na_key_block(g):
    return jnp.clip(g - 1, 0, GRID_H // _NA_GROUP_ROWS - 3)


def _na_bias_tables(rpb):
    c = np.arange(GRID_W)
    cs = np.clip(c - NA_WIN_W // 2, 0, GRID_W - NA_WIN_W)
    col_valid = (c[None, :] >= cs[:, None]) & (c[None, :] < cs[:, None] + NA_WIN_W)
    co = np.clip(c[None, :] - c[:, None] + NA_WIN_W - 1, 0, 2 * NA_WIN_W - 2)
    onehot = jnp.asarray((co[None] == np.arange(2 * NA_WIN_W - 1)[:, None, None]) & col_valid[None], F32)
    tcol = jnp.einsum("hro,oqk->hrqk", rpb.astype(F32), onehot, precision=lax.Precision.HIGHEST)
    tcol = jnp.where(jnp.asarray(col_valid)[None, None], tcol, NEG)
    neg_blk = jnp.full((NH, GRID_W, GRID_W), NEG, F32)
    tabs = []
    for g in (0, 1, 2, 7):
        r0 = _NA_GROUP_ROWS * g
        ks = int(np.clip(g - 1, 0, GRID_H // _NA_GROUP_ROWS - 3)) * _NA_GROUP_ROWS
        rows = []
        for dr in range(_NA_GROUP_ROWS):
            r = r0 + dr
            rs = int(np.clip(r - NA_WIN_H // 2, 0, GRID_H - NA_WIN_H))
            blks = []
            for krel in range(_NA_KEY_ROWS):
                kr = ks + krel
                blks.append(tcol[:, kr - r + NA_WIN_H - 1] if rs <= kr < rs + NA_WIN_H else neg_blk)
            rows.append(jnp.concatenate(blks, axis=-1))
        tabs.append(jnp.concatenate(rows, axis=-2))
    tabs.append(jnp.full((NH, RB, 3 * RB), NEG, F32))
    return jnp.stack(tabs, axis=1)


_NA_HPB = 2


def _na_kernel(q_ref, k0_ref, k1_ref, k2_ref, v0_ref, v1_ref, v2_ref, kc_ref, vc_ref, b_ref, o_ref):
    for hh in range(_NA_HPB):
        ln = slice(hh * HEAD_DIM, (hh + 1) * HEAD_DIM)
        q = (q_ref[:, ln].astype(F32) * HEAD_DIM ** -0.5).astype(BF16)
        s = [_dot_nt(q, kr[:, ln]) + b_ref[hh, 0, :, t * RB:(t + 1) * RB]
             for t, kr in enumerate((k0_ref, k1_ref, k2_ref))]
        s.append(_dot_nt(q, kc_ref[:, ln]))
        m = functools.reduce(jnp.maximum, [jnp.max(t, axis=-1, keepdims=True) for t in s])
        p = [jnp.exp(t - m) for t in s]
        l = functools.reduce(jnp.add, [jnp.sum(t, axis=-1, keepdims=True) for t in p])
        vs = (v0_ref, v1_ref, v2_ref, vc_ref)
        o = functools.reduce(jnp.add, [_dot(pt.astype(BF16), vr[:, ln]) for pt, vr in zip(p, vs)])
        o_ref[:, ln] = (o / l).astype(o_ref.dtype)


def _na_attention(proj, bias):
    hb = NH // _NA_HPB

    def qrow(h, g, b):
        return b * BLOCKS_PER_BATCH + jnp.where(g < 8, 1 + g, 0)

    def krow(t):
        return lambda h, g, b: (b * BLOCKS_PER_BATCH + 1 + _na_key_block(jnp.minimum(g, 7)) + t, hb + h)

    def vrow(t):
        return lambda h, g, b: (b * BLOCKS_PER_BATCH + 1 + _na_key_block(jnp.minimum(g, 7)) + t, 2 * hb + h)

    blk = (RB, _NA_HPB * HEAD_DIM)
    pattern = jnp.asarray(_NA_PATTERN, jnp.int32)

    def bias_map(h, g, b, pat_ref):
        return (h, pat_ref[g], 0, 0)

    def wrap(f):
        return lambda h, g, b, pat_ref: f(h, g, b)

    in_specs = [pl.BlockSpec(blk, wrap(lambda h, g, b: (qrow(h, g, b), h)))]
    in_specs += [pl.BlockSpec(blk, wrap(krow(t))) for t in range(3)]
    in_specs += [pl.BlockSpec(blk, wrap(vrow(t))) for t in range(3)]
    in_specs += [pl.BlockSpec(blk, wrap(lambda h, g, b: (b * BLOCKS_PER_BATCH, hb + h))),
                 pl.BlockSpec(blk, wrap(lambda h, g, b: (b * BLOCKS_PER_BATCH, 2 * hb + h))),
                 pl.BlockSpec((_NA_HPB, 1, RB, 3 * RB), bias_map)]

    def body(pat_ref, *refs):
        _na_kernel(*refs)

    return pl.pallas_call(
        body,
        grid_spec=pltpu.PrefetchScalarGridSpec(
            num_scalar_prefetch=1,
            grid=(hb, BLOCKS_PER_BATCH, BATCH),
            in_specs=in_specs,
            out_specs=pl.BlockSpec(blk, wrap(lambda h, g, b: (qrow(h, g, b), h))),
        ),
        out_shape=jax.ShapeDtypeStruct((M, NH * HEAD_DIM), BF16),
        compiler_params=_cparams(("arbitrary", "arbitrary", "arbitrary"), 32),
        name="na_attention",
    )(pattern, *([proj] * 9), bias)


_RET_NC = TB // RET_CHUNK
_RET_CTX_NC = CTX // RET_CHUNK


def _ret_kernel(lg_ref, q_ref, k_ref, v_ref, g_ref, cos_ref, sin_ref, o_ref, qr_ref, kr_ref, sf_ref, sb_ref):
    h = pl.program_id(1)
    lgf = lg_ref[0, h]
    lgb = lg_ref[1, h]
    c = RET_CHUNK
    qr_ref[...] = _rope(q_ref[...].astype(F32), cos_ref[...], sin_ref[...], HEAD_DIM // 4)
    kr_ref[...] = _rope(k_ref[...].astype(F32), cos_ref[...], sin_ref[...], HEAD_DIM // 4) * HEAD_DIM ** -0.5

    pos = lax.broadcasted_iota(jnp.int32, (c, 1), 0).astype(F32)
    kdec_f = jnp.exp(lgf * (c - 1.0 - pos))
    kdec_b = jnp.exp(lgb * pos)
    qdec_f = jnp.exp(lgf * (pos + 1.0))
    qdec_b = jnp.exp(lgb * (c - pos))
    gc_f = jnp.exp(lgf * c)
    gc_b = jnp.exp(lgb * c)
    rel = (lax.broadcasted_iota(jnp.int32, (c, c), 0) - lax.broadcasted_iota(jnp.int32, (c, c), 1)).astype(F32)
    dmask = (jnp.where(rel >= 0, jnp.exp(jnp.maximum(rel, 0.0) * lgf), 0.0)
             + jnp.where(rel <= 0, jnp.exp(jnp.maximum(-rel, 0.0) * lgb), 0.0))

    def kv_step(n, carry):
        rows = pl.ds(pl.multiple_of(n * c, c), c)
        kn = kr_ref[rows, :]
        vn = v_ref[rows, :]
        sf_ref[n] = _dot_tn((kn * kdec_f).astype(BF16), vn)
        sb_ref[n] = _dot_tn((kn * kdec_b).astype(BF16), vn)
        return carry

    lax.fori_loop(0, _RET_NC, kv_step, 0, unroll=3)

    def state_step(n, s, s_ref, gc):
        u = s_ref[n]
        s_ref[n] = s
        return s * gc + u

    zero = jnp.zeros((HEAD_DIM, RET_DV), F32)
    lax.fori_loop(0, _RET_NC, lambda n, s: state_step(n, s, sf_ref, gc_f), zero)
    s = lax.fori_loop(0, _RET_CTX_NC, lambda i, s: state_step(_RET_CTX_NC - 1 - i, s, sb_ref, gc_b), zero)
    lax.fori_loop(0, _RET_NC - _RET_CTX_NC, lambda i, s: state_step(_RET_NC - 1 - i, s, sb_ref, gc_b), s)

    def out_step(n, carry):
        rows = pl.ds(pl.multiple_of(n * c, c), c)
        qn = qr_ref[rows, :]
        vn = v_ref[rows, :]
        scores = _dot_nt(qn.astype(BF16), kr_ref[rows, :].astype(BF16)) * dmask
        intra = _dot(scores.astype(BF16), vn)
        qcat = jnp.concatenate([qn * qdec_f, qn * qdec_b], axis=1).astype(BF16)
        scat = jnp.concatenate([sf_ref[n], sb_ref[n]], axis=0).astype(BF16)
        o = intra + _dot(qcat, scat)
        o = o * lax.rsqrt(jnp.mean(o * o, axis=-1, keepdims=True) + EPS)
        o_ref[rows, :] = (o * _silu(g_ref[rows, :].astype(F32))).astype(o_ref.dtype)
        return carry

    lax.fori_loop(0, _RET_NC, out_step, 0, unroll=3)


def _retention(proj, log_g, cos, sin):
    seq128 = lambda blk0: pl.BlockSpec((TB, HEAD_DIM), lambda b, h, lg: (b, blk0 + h))
    seq256 = lambda blk0: pl.BlockSpec((TB, RET_DV), lambda b, h, lg: (b, blk0 + h))
    tab = pl.BlockSpec((TB, HEAD_DIM), lambda b, h, lg: (0, 0))
    return pl.pallas_call(
        _ret_kernel,
        grid_spec=pltpu.PrefetchScalarGridSpec(
            num_scalar_prefetch=1,
            grid=(BATCH, NH),
            in_specs=[seq128(3072 // HEAD_DIM), seq128(4096 // HEAD_DIM),
                      seq256(5120 // RET_DV), seq256(7168 // RET_DV), tab, tab],
            out_specs=pl.BlockSpec((TB, RET_DV), lambda b, h, lg: (b, h)),
            scratch_shapes=[pltpu.VMEM((TB, HEAD_DIM), F32), pltpu.VMEM((TB, HEAD_DIM), F32),
                            pltpu.VMEM((_RET_NC, HEAD_DIM, RET_DV), F32),
                            pltpu.VMEM((_RET_NC, HEAD_DIM, RET_DV), F32)],
        ),
        out_shape=jax.ShapeDtypeStruct((M, NH * RET_DV), BF16),
        compiler_params=_cparams(("arbitrary", "arbitrary"), 48),
        name="retention",
    )(log_g, proj, proj, proj, proj, cos, sin)


def _diff_kernel(lam_ref, q_ref, k_ref, v_ref, cq_ref, sq_ref, ck_ref, sk_ref, o_ref, kb_ref, vb_ref, *, post_scale):
    qi = pl.program_id(2)

    @pl.when(qi == 0)
    def _():
        kb_ref[...] = _rope(k_ref[...].astype(F32), ck_ref[...], sk_ref[...], DIFF_DK // 4).astype(BF16)
        vb_ref[...] = v_ref[...].astype(BF16)
        o_ref[...] = jnp.zeros_like(o_ref)

    @pl.when(qi > 0)
    def _():
        lam = lam_ref[0]
        q = _rope(q_ref[...].astype(F32), cq_ref[...], sq_ref[...], DIFF_DK // 4) * (DIFF_DK ** -0.5 * LOG2E)
        lane = lax.broadcasted_iota(jnp.int32, q.shape, 1)
        kb = kb_ref[...]
        p, inv = [], []
        for t in range(2):
            qt = jnp.where((lane >= DIFF_DK) == (t == 1), q, 0.0).astype(BF16)
            s = _dot_nt(qt, kb)
            e = jnp.exp2(s - jnp.max(s, axis=-1, keepdims=True))
            p.append(e)
            inv.append(1.0 / jnp.sum(e, axis=-1, keepdims=True))
        a = (p[0] * inv[0] - p[1] * (lam * inv[1])).astype(BF16)
        o = _dot(a, vb_ref[...])
        o = o * lax.rsqrt(jnp.mean(o * o, axis=-1, keepdims=True) + EPS)
        o_ref[...] = (o * post_scale).astype(o_ref.dtype)


def _diff_attention(proj, lam, cos, sin, post_scale):
    blk = (RB, HEAD_DIM)
    seq = (TB, HEAD_DIM)
    qmap = lambda b, h, qi, lam_ref: (b * BLOCKS_PER_BATCH + qi, h)
    return pl.pallas_call(
        functools.partial(_diff_kernel, post_scale=post_scale),
        grid_spec=pltpu.PrefetchScalarGridSpec(
            num_scalar_prefetch=1,
            grid=(BATCH, NH, BLOCKS_PER_BATCH),
            in_specs=[pl.BlockSpec(blk, qmap),
                      pl.BlockSpec(seq, lambda b, h, qi, lam_ref: (b, NH + h)),
                      pl.BlockSpec(seq, lambda b, h, qi, lam_ref: (b, 2 * NH + h)),
                      pl.BlockSpec(blk, lambda b, h, qi, lam_ref: (qi, 0)),
                      pl.BlockSpec(blk, lambda b, h, qi, lam_ref: (qi, 0)),
                      pl.BlockSpec(seq, lambda b, h, qi, lam_ref: (0, 0)),
                      pl.BlockSpec(seq, lambda b, h, qi, lam_ref: (0, 0))],
            out_specs=pl.BlockSpec(blk, qmap),
            scratch_shapes=[pltpu.VMEM(seq, BF16), pltpu.VMEM(seq, BF16)],
        ),
        out_shape=jax.ShapeDtypeStruct((M, NH * HEAD_DIM), BF16),
        compiler_params=_cparams(("arbitrary", "arbitrary", "arbitrary"), 48),
        name="diff_attention",
    )(lam, proj, proj, proj, cos, sin, cos, sin)


_HG_NC = TB // HG_CHUNK
_HG_CTX_NC = CTX // HG_CHUNK


def _chunk_cumsum(x, reverse):
    c = x.shape[0]
    row = lax.broadcasted_iota(jnp.int32, x.shape, 0)
    s = 1
    while s < c:
        if reverse:
            x = x + jnp.where(row < c - s, pltpu.roll(x, c - s, axis=0), 0.0)
        else:
            x = x + jnp.where(row >= s, pltpu.roll(x, s, axis=0), 0.0)
        s *= 2
    return x


def _hg_kernel(q_ref, ff_ref, fb_ref, v_ref, g_ref, lb_ref, o_ref,
               qs_ref, kf_ref, kb_ref, cf_ref, cb_ref, sf_ref, sb_ref, df_ref, db_ref):
    c = HG_CHUNK
    lb = lb_ref[0]
    qraw = q_ref[...].astype(F32)
    qs_ref[...] = qraw * _sigmoid(qraw)

    def gate(raw_ref, k_ref, lf_ref):
        f = lb + (1.0 - lb) * _sigmoid(raw_ref[...].astype(F32))
        k_ref[...] = 1.0 - f
        lf_ref[...] = jnp.log(f)

    gate(ff_ref, kf_ref, cf_ref)
    gate(fb_ref, kb_ref, cb_ref)

    def chunk_prep(n, carry):
        rows = pl.ds(pl.multiple_of(n * c, c), c)
        vn = v_ref[rows, :]
        for cum_ref, k_ref, u_ref, d_ref, reverse in ((cf_ref, kf_ref, sf_ref, df_ref, False),
                                                      (cb_ref, kb_ref, sb_ref, db_ref, True)):
            cum = _chunk_cumsum(cum_ref[rows, :], reverse)
            cum_ref[rows, :] = cum
            last = cum[0:1, :] if reverse else cum[c - 1:c, :]
            u_ref[n] = _dot_tn(vn, (k_ref[rows, :] * jnp.exp(last - cum)).astype(BF16))
            d_ref[n] = jnp.exp(last)
        return carry

    lax.fori_loop(0, _HG_NC, chunk_prep, 0, unroll=4)

    def state_step(n, st, s_ref, d_ref):
        u = s_ref[n]
        s_ref[n] = st
        return st * d_ref[n] + u

    zero = jnp.zeros((HEAD_DIM, HEAD_DIM), F32)
    lax.fori_loop(0, _HG_NC, lambda n, s: state_step(n, s, sf_ref, df_ref), zero)
    s = lax.fori_loop(0, _HG_CTX_NC, lambda i, s: state_step(_HG_CTX_NC - 1 - i, s, sb_ref, db_ref), zero)
    lax.fori_loop(0, _HG_NC - _HG_CTX_NC, lambda i, s: state_step(_HG_NC - 1 - i, s, sb_ref, db_ref), s)

    o_ref[0:CTX, :] = jnp.zeros((CTX, HEAD_DIM), o_ref.dtype)

    def level(q, kf, kb, cf, cb, half):
        span = 2 * half
        row = lax.broadcasted_iota(jnp.int32, (c, 1), 0)
        upper = (row % span) >= half
        ref_f = jnp.concatenate([jnp.broadcast_to(cf[b0 + half - 1:b0 + half, :], (span, HEAD_DIM))
                                 for b0 in range(0, c, span)], axis=0)
        ref_b = jnp.concatenate([jnp.broadcast_to(cb[b0 + half:b0 + half + 1, :], (span, HEAD_DIM))
                                 for b0 in range(0, c, span)], axis=0)
        ef = jnp.exp(jnp.minimum(jnp.where(upper, cf - ref_f, ref_f - cf), 0.0))
        eb = jnp.exp(jnp.minimum(jnp.where(upper, ref_b - cb, cb - ref_b), 0.0))
        qq = jnp.concatenate([jnp.where(upper, q * ef, 0.0), jnp.where(upper, 0.0, q * eb)], axis=1)
        kk = jnp.concatenate([jnp.where(upper, 0.0, kf * ef), jnp.where(upper, kb * eb, 0.0)], axis=1)
        a = _dot_nt(qq.astype(BF16), kk.astype(BF16))
        if span == c:
            return a
        ri = lax.broadcasted_iota(jnp.int32, (c, c), 0)
        ci = lax.broadcasted_iota(jnp.int32, (c, c), 1)
        return jnp.where(ri // span == ci // span, a, 0.0)

    def sub_block(q, kf, kb, lo):
        sub = HG_SUB
        row = lax.broadcasted_iota(jnp.int32, (sub, 1), 0)
        lane = lax.broadcasted_iota(jnp.int32, (sub, c), 1)
        qb, kfb, kbb = q[lo:lo + sub], kf[lo:lo + sub], kb[lo:lo + sub]
        ffb, fbb = 1.0 - kfb, 1.0 - kbb
        pf = [None] * sub
        p = jnp.where(row == sub - 1, 1.0, jnp.zeros((sub, HEAD_DIM), F32))
        pf[sub - 1] = p
        for j in range(sub - 2, -1, -1):
            p = jnp.where(row == j, 1.0, p * ffb[j + 1:j + 2])
            pf[j] = p
        att = jnp.zeros((sub, c), F32)
        p = None
        for j in range(sub):
            p = (jnp.where(row == 0, 1.0, jnp.zeros((sub, HEAD_DIM), F32)) if j == 0
                 else jnp.where(row == j, 1.0, p * fbb[j - 1:j]))
            col = jnp.sum(qb * (pf[j] * kfb[j:j + 1] + p * kbb[j:j + 1]), axis=-1, keepdims=True)
            att = jnp.where(lane == lo + j, col, att)
        return att

    def out_step(n, carry):
        rows = pl.ds(pl.multiple_of(n * c, c), c)
        q = qs_ref[rows, :]
        kf = kf_ref[rows, :]
        kb = kb_ref[rows, :]
        cf = cf_ref[rows, :]
        cb = cb_ref[rows, :]
        qcat = jnp.concatenate([q * jnp.exp(cf), q * jnp.exp(cb)], axis=1).astype(BF16)
        scat = jnp.concatenate([sf_ref[n], sb_ref[n]], axis=1).astype(BF16)
        o = _dot_nt(qcat, scat)
        att = jnp.concatenate([sub_block(q, kf, kb, lo) for lo in range(0, c, HG_SUB)], axis=0)
        half = c // 2
        while half >= HG_SUB:
            att = att + level(q, kf, kb, cf, cb, half)
            half //= 2
        o = o + _dot(att.astype(BF16), v_ref[rows, :])
        o = o * lax.rsqrt(jnp.mean(o * o, axis=-1, keepdims=True) + EPS)
        graw = g_ref[rows, :].astype(F32)
        o_ref[rows, :] = (o * (graw * _sigmoid(graw))).astype(o_ref.dtype)
        return carry

    lax.fori_loop(_HG_CTX_NC, _HG_NC, out_step, 0, unroll=4)


def _hgrn(proj, lb):
    seq = lambda col0: pl.BlockSpec((TB, HEAD_DIM), lambda b, h: (b, col0 // HEAD_DIM + h))
    full = pltpu.VMEM((TB, HEAD_DIM), F32)
    states = pltpu.VMEM((_HG_NC, HEAD_DIM, HEAD_DIM), F32)
    decays = pltpu.VMEM((_HG_NC, 1, HEAD_DIM), F32)
    return pl.pallas_call(
        _hg_kernel,
        grid=(BATCH, NH),
        in_specs=[seq(3072), seq(4096), seq(5120), seq(6144), seq(7168),
                  pl.BlockSpec((1, 1, HEAD_DIM), lambda b, h: (h, 0, 0))],
        out_specs=pl.BlockSpec((TB, HEAD_DIM), lambda b, h: (b, h)),
        out_shape=jax.ShapeDtypeStruct((M, NH * HEAD_DIM), BF16),
        scratch_shapes=[full, full, full, full, full, states, states, decays, decays],
        compiler_params=_cparams(("arbitrary", "arbitrary"), 48),
        name="hgrn",
    )(proj, proj, proj, proj, proj, lb.reshape(NH, 1, HEAD_DIM))


BULK_DMA_PRIORITY = 1


def _row_gather_start(idx_ref, k, src_hbm, dst, sem, n, alternate=False):
    for r in range(n):
        cp = pltpu.make_async_copy(src_hbm.at[pl.ds(idx_ref[0, k, r], 1)], dst.at[pl.ds(r, 1)], sem)
        cp.start(priority=r % 2 if alternate else 0)


def _row_gather_wait(src_hbm, dst, sem, n):
    pltpu.make_async_copy(src_hbm.at[pl.ds(0, n)], dst, sem).wait()


def _expert_weights_step(i, be_ref, nx_ref, w_hbms, wst, wsem, wb_refs, layer):
    e = be_ref[i]

    def fetch(ex):
        return [pltpu.make_async_copy(w.at[layer, ex], wst.at[k], wsem.at[k]) for k, w in enumerate(w_hbms)]

    @pl.when(i == 0)
    def _():
        for cp in fetch(e):
            cp.start(priority=BULK_DMA_PRIORITY)

    @pl.when(jnp.logical_or(i == 0, e != be_ref[jnp.maximum(i - 1, 0)]))
    def _():
        for cp in fetch(e):
            cp.wait()
        for k, wb in enumerate(wb_refs):
            wb[...] = wst[k].astype(BF16)
        nx = nx_ref[i]

        @pl.when(nx != e)
        def _():
            for cp in fetch(nx):
                cp.start(priority=BULK_DMA_PRIORITY)


def _moe_up_kernel(be_ref, nx_ref, nu_ref, rt0_ref, rt1_ref, x_hbm, w1_hbm, w3_hbm, o_ref,
                   xbuf, sem, wst, wsem, w1b_ref, w3b_ref, *, layer):
    i = pl.program_id(0)
    nu = nu_ref[0]

    @pl.when(jnp.logical_and(i == 0, nu > 0))
    def _():
        _row_gather_start(rt0_ref, 0, x_hbm, xbuf.at[0], sem.at[0], MOE_TM)

    @pl.when(i + 1 < nu)
    def _():
        slot = (i + 1) % 2
        _row_gather_start(rt1_ref, 0, x_hbm, xbuf.at[slot], sem.at[slot], MOE_TM)

    _expert_weights_step(i, be_ref, nx_ref, (w1_hbm, w3_hbm), wst, wsem, (w1b_ref, w3b_ref), layer)

    @pl.when(i < nu)
    def _():
        slot = i % 2
        _row_gather_wait(x_hbm, xbuf.at[slot], sem.at[slot], MOE_TM)
        x = xbuf[slot].astype(BF16)
        a = _dot(x, w1b_ref[...])
        o_ref[...] = (_silu(a) * _dot(x, w3b_ref[...])).astype(o_ref.dtype)

    @pl.when(i >= nu)
    def _():
        o_ref[...] = jnp.zeros_like(o_ref)


def _moe_up(layer, blk_e, nxt_e, n_used, row_tok, h2, w1, w3):
    rt = row_tok.reshape(MOE_BLOCKS, 1, MOE_TM)
    smem_blk = lambda f: pl.BlockSpec((1, 1, MOE_TM), f, memory_space=pltpu.SMEM)
    hbm = pl.BlockSpec(memory_space=pl.ANY)
    return pl.pallas_call(
        functools.partial(_moe_up_kernel, layer=layer),
        grid_spec=pltpu.PrefetchScalarGridSpec(
            num_scalar_prefetch=3,
            grid=(MOE_BLOCKS,),
            in_specs=[smem_blk(lambda i, be, nx, nu: (i, 0, 0)),
                      smem_blk(lambda i, be, nx, nu: (jnp.minimum(i + 1, MOE_BLOCKS - 1), 0, 0)),
                      hbm, hbm, hbm],
            out_specs=pl.BlockSpec((MOE_TM, EXPERT_FF), lambda i, be, nx, nu: (i, 0)),
            scratch_shapes=[pltpu.VMEM((2, MOE_TM, D), F32), pltpu.SemaphoreType.DMA((2,)),
                            pltpu.VMEM((2, D, EXPERT_FF), F32), pltpu.SemaphoreType.DMA((2,)),
                            pltpu.VMEM((D, EXPERT_FF), BF16), pltpu.VMEM((D, EXPERT_FF), BF16)],
        ),
        out_shape=jax.ShapeDtypeStruct((MOE_ROWS, EXPERT_FF), BF16),
        compiler_params=_cparams(("arbitrary",), 44),
        name="moe_up",
    )(blk_e, nxt_e, n_used, rt, rt, h2, w1, w3)


def _moe_down_kernel(be_ref, nx_ref, nu_ref, h_ref, w2_hbm, o_ref, wst, wsem, w2b_ref, *, layer):
    i = pl.program_id(0)
    _expert_weights_step(i, be_ref, nx_ref, (w2_hbm,), wst, wsem, (w2b_ref,), layer)

    @pl.when(i < nu_ref[0])
    def _():
        o_ref[...] = _dot(h_ref[...], w2b_ref[...])

    @pl.when(i >= nu_ref[0])
    def _():
        o_ref[...] = jnp.zeros_like(o_ref)


def _moe_down(layer, blk_e, nxt_e, n_used, hmid, w2):
    return pl.pallas_call(
        functools.partial(_moe_down_kernel, layer=layer),
        grid_spec=pltpu.PrefetchScalarGridSpec(
            num_scalar_prefetch=3,
            grid=(MOE_BLOCKS,),
            in_specs=[pl.BlockSpec((MOE_TM, EXPERT_FF), lambda i, be, nx, nu: (i, 0)),
                      pl.BlockSpec(memory_space=pl.ANY)],
            out_specs=pl.BlockSpec((MOE_TM, D), lambda i, be, nx, nu: (i, 0)),
            scratch_shapes=[pltpu.VMEM((1, EXPERT_FF, D), F32), pltpu.SemaphoreType.DMA((1,)),
                            pltpu.VMEM((EXPERT_FF, D), BF16)],
        ),
        out_shape=jax.ShapeDtypeStruct((MOE_ROWS, D), F32),
        compiler_params=_cparams(("arbitrary",), 32),
        name="moe_down",
    )(blk_e, nxt_e, n_used, hmid, w2)


def _combine_kernel(p0_ref, p1_ref, y_hbm, x_ref, w_ref, g_ref, mul_ref, add_ref, *rest, last):
    out_refs, (ybuf, sem) = rest[:-2], rest[-2:]
    i = pl.program_id(0)
    n = pl.num_programs(0)

    def start(p_ref, slot):
        for k in range(TOP_K):
            _row_gather_start(p_ref, k, y_hbm, ybuf.at[slot, k], sem.at[slot], RB, alternate=True)

    @pl.when(i == 0)
    def _():
        start(p0_ref, 0)

    @pl.when(i + 1 < n)
    def _():
        start(p1_ref, (i + 1) % 2)

    slot = i % 2
    for k in range(TOP_K):
        _row_gather_wait(y_hbm, ybuf.at[slot, k], sem.at[slot], RB)
    w = w_ref[...]
    y = w[:, 0:1] * ybuf[slot, 0] + w[:, 1:2] * ybuf[slot, 1]
    t = x_ref[...] + g_ref[0] * y
    nrm = t * lax.rsqrt(jnp.mean(t * t, axis=-1, keepdims=True) + EPS) * mul_ref[0] + add_ref[0]
    if last:
        out_refs[0][0] = nrm
    else:
        out_refs[0][...] = t
        out_refs[1][...] = nrm.astype(BF16)


def _moe_combine(tok, y, pos, gates, gate_mod, mul, add, last):
    nb = M // RB
    pblk = pos.reshape(nb, RB, TOP_K).transpose(0, 2, 1)
    row = pl.BlockSpec((RB, D), lambda i: (i, 0))
    mod = pl.BlockSpec((1, 1, D), _mod_index)
    smem_blk = lambda f: pl.BlockSpec((1, TOP_K, RB), f, memory_space=pltpu.SMEM)
    if last:
        out_specs = pl.BlockSpec((1, RB, D), lambda i: (i // BLOCKS_PER_BATCH,
                                                        jnp.maximum(i % BLOCKS_PER_BATCH - 1, 0), 0))
        out_shape = jax.ShapeDtypeStruct((BATCH, SEQ, D), F32)
    else:
        out_specs = [row, row]
        out_shape = [jax.ShapeDtypeStruct((M, D), F32), jax.ShapeDtypeStruct((M, D), BF16)]
    return pl.pallas_call(
        functools.partial(_combine_kernel, last=last),
        grid=(nb,),
        in_specs=[smem_blk(lambda i: (i, 0, 0)), smem_blk(lambda i: (jnp.minimum(i + 1, nb - 1), 0, 0)),
                  pl.BlockSpec(memory_space=pl.ANY), row,
                  pl.BlockSpec((RB, TOP_K), lambda i: (i, 0)), mod, mod, mod],
        out_specs=out_specs,
        out_shape=out_shape,
        scratch_shapes=[pltpu.VMEM((2, TOP_K, RB, D), F32), pltpu.SemaphoreType.DMA((2,))],
        compiler_params=_cparams(("arbitrary",), 48),
        name="moe_combine",
    )(pblk, pblk, y, tok, gates, gate_mod, mul, add)


def _route(logits, valid):
    lg_g = logits[:, :N_GROUPS]
    p_g = jax.nn.softmax(lg_g, axis=-1)
    g_top = jnp.argmax(lg_g, axis=-1)
    w_grp = jnp.take_along_axis(p_g, g_top[:, None], axis=1)
    lg_e = logits[:, N_GROUPS:N_GROUPS + N_EXPERTS].reshape(M, N_GROUPS, EXPERTS_PER_GROUP)
    lg_in = jnp.take_along_axis(lg_e, g_top[:, None, None], axis=1)[:, 0]
    top_v, top_i = lax.top_k(lg_in, TOP_K)
    gates = w_grp * jax.nn.softmax(top_v, axis=-1)
    gates = jnp.where(valid[:, None], gates, 0.0)
    eid = g_top[:, None] * EXPERTS_PER_GROUP + top_i
    eid = jnp.where(valid[:, None], eid, N_EXPERTS).reshape(-1).astype(jnp.int32)
    n_slots = M * TOP_K
    slot = jnp.arange(n_slots, dtype=jnp.int32)
    experts = jnp.arange(N_EXPERTS + 1, dtype=jnp.int32)
    counts = jnp.sum((eid[:, None] == experts[None, :]).astype(jnp.int32), axis=0)
    padded = (counts + MOE_TM - 1) // MOE_TM * MOE_TM
    p_end = jnp.cumsum(padded)
    p_start = p_end - padded
    u_start = jnp.cumsum(counts) - counts
    shift = p_start - u_start
    s_e, order = lax.sort((eid, slot), num_keys=1, is_stable=True)
    dest = slot + jnp.sum(jnp.where(s_e[:, None] == experts[None, :], shift[None, :], 0), axis=1)
    dest = jnp.where(s_e < N_EXPERTS, dest, jnp.clip(order, 0, jnp.maximum(p_end[N_EXPERTS - 1] - 1, 0)))
    _, pos = lax.sort((order, dest), num_keys=1)
    pos = pos.reshape(M, TOP_K)
    blk_start = jnp.arange(MOE_BLOCKS, dtype=jnp.int32) * MOE_TM
    blk_e = jnp.sum((p_end[None, :N_EXPERTS] <= blk_start[:, None]).astype(jnp.int32), axis=1)
    blk_e = jnp.minimum(blk_e, N_EXPERTS - 1)
    later = jnp.min(jnp.where(blk_e[None, :] > blk_e[:, None], blk_e[None, :], N_EXPERTS), axis=1)
    nxt_e = jnp.where(later < N_EXPERTS, later, blk_e)
    onehot_be = blk_e[:, None] == experts[None, :N_EXPERTS]
    blk_shift = jnp.sum(jnp.where(onehot_be, shift[None, :N_EXPERTS], 0), axis=1)
    blk_lim = jnp.sum(jnp.where(onehot_be, (p_start + counts)[None, :N_EXPERTS], 0), axis=1)
    rows = jnp.arange(MOE_ROWS, dtype=jnp.int32).reshape(MOE_BLOCKS, MOE_TM)
    src = jnp.clip(rows - blk_shift[:, None], 0, n_slots - 1)
    row_tok = jnp.where(rows < blk_lim[:, None], jnp.take(order // TOP_K, src), 0)
    n_used = (p_end[N_EXPERTS - 1] // MOE_TM).astype(jnp.int32).reshape(1)
    return gates, row_tok, pos, blk_e, nxt_e, n_used


def _router_tables(wg, bg, we, be):
    wr = jnp.zeros((D, ROUTER_PAD), F32).at[:, :N_GROUPS].set(wg).at[:, N_GROUPS:N_GROUPS + N_EXPERTS].set(we)
    w_hi = wr.astype(BF16)
    w_lo = (wr - w_hi.astype(F32)).astype(BF16)
    r_bias = jnp.zeros((1, ROUTER_PAD), F32).at[0, :N_GROUPS].set(bg).at[0, N_GROUPS:N_GROUPS + N_EXPERTS].set(be)
    return w_hi, w_lo, r_bias


def _moe(layer, tok, h2, logits, gate_mod, w1, w3, w2, valid, next_mul, next_add, last):
    gates, row_tok, pos, blk_e, nxt_e, n_used = _route(logits, valid)
    hmid = _moe_up(layer, blk_e, nxt_e, n_used, row_tok, h2, w1, w3)
    y = _moe_down(layer, blk_e, nxt_e, n_used, hmid, w2)
    return _moe_combine(tok, y, pos, gates, gate_mod, next_mul, next_add, last)


def _mod9(t):
    return t[:BATCH + 1].reshape(BATCH + 1, 1, D)


def kernel(x, c, ctx, c_ctx, ada_w, ada_b, norm1_w, norm2_w, w_in_even, w_out_even, na_rpb, ret_decay,
           w_in_odd, w_out_odd, diff_lambda, hg_lb_logits, router_g_w, router_g_b, router_e_w, router_e_b,
           moe_w1, moe_w3, moe_w2, norm_f_w):
    depth = ada_w.shape[0]
    c16 =jnp.concatenate([c, c_ctx[None], jnp.zeros((16 - BATCH - 1, D), F32)], axis=0)
    mod = _ada_mod(c16, ada_w, ada_b)
    lb_all = jnp.cumsum(jax.nn.softmax(hg_lb_logits.astype(F32), axis=0), axis=0)
    lb_all = lb_all - lb_all[0]
    is_latent = (jnp.arange(M) % TB) >= CTX
    all_rows = jnp.ones((M,), bool)

    mods = [[_mod9(t) for t in jnp.split(mod[l], 6, axis=-1)] for l in range(depth)]
    tok, h = _first_norm(x, ctx, norm1_w[0] * (1.0 + mods[0][1]), mods[0][0])
    for l in range(depth):
        with_ctx = l < depth - 1
        sh1, s1, g1, sh2, s2, g2 = mods[l]
        if with_ctx:
            next_mul, next_add = norm1_w[l + 1] * (1.0 + mods[l + 1][1]), mods[l + 1][0]
        else:
            next_mul = jnp.broadcast_to(norm_f_w.reshape(1, 1, D), (BATCH + 1, 1, D))
            next_add = jnp.zeros((BATCH + 1, 1, D), F32)
        j = l // 2
        if l % 2 == 0:
            proj = _matmul(h, w_in_even[j].astype(BF16), BF16)
            m1 = _na_attention(proj, _na_bias_tables(na_rpb[j]))
            cos, sin = _rope_tables(HEAD_DIM, 1)
            m2 = _retention(proj, -jnp.exp(ret_decay[j].astype(F32)), cos, sin)
            w_out = w_out_even[j].astype(BF16)
        else:
            proj = _matmul(h, w_in_odd[j].astype(BF16), BF16)
            lp = diff_lambda[j].astype(F32)
            lam_init = 0.8 - 0.6 * math.exp(-0.3 * l)
            lam = jnp.exp(jnp.sum(lp[0] * lp[1])) - jnp.exp(jnp.sum(lp[2] * lp[3])) + lam_init
            cos, sin = _rope_tables(DIFF_DK, 2)
            m1 = _diff_attention(proj, lam.reshape(1), cos, sin, 1.0 - lam_init)
            m2 = _hgrn(proj, lb_all[l])
            w_out = w_out_odd[j].astype(BF16)
        tok, h2, logits = _out_proj(m1, m2, w_out[:NH * HEAD_DIM], w_out[NH * HEAD_DIM:], tok, g1,
                                    norm2_w[l] * (1.0 + s2), sh2,
                                    *_router_tables(router_g_w[l], router_g_b[l], router_e_w[l], router_e_b[l]))
        res = _moe(l, tok, h2, logits, g2, moe_w1, moe_w3, moe_w2, all_rows if with_ctx else is_latent,
                   next_mul, next_add, not with_ctx)
        if with_ctx:
            tok, h = res
    return res
```

```python
import functools
import math

import numpy as np
import jax
import jax.numpy as jnp
from jax import lax
from jax.experimental import pallas as pl
from jax.experimental.pallas import tpu as pltpu

F32 = jnp.float32
BF16 = jnp.bfloat16

D = 2048
BATCH = 8
SEQ = 2048
CTX = 256
TB = SEQ + CTX
M = BATCH * TB
GRID_W = 64
GRID_H = SEQ // GRID_W
HEAD_DIM = 128
NH = 8
NA_WIN_H = 8
NA_WIN_W = 16
RET_DV = 256
RET_CHUNK = 128
DIFF_DK = 64
HG_CHUNK = 64
HG_SUB = 8
N_GROUPS = 4
EXPERTS_PER_GROUP = 8
N_EXPERTS = 32
TOP_K = 2
EXPERT_FF = 1024
ROPE_BASE = 10000.0
EPS = 1e-6
NEG = -1e30
LOG2E = math.log2(math.e)

RB = 256
BLOCKS_PER_BATCH = TB // RB
MOE_TM = 256
MOE_ROWS = M * TOP_K + N_EXPERTS * MOE_TM
MOE_BLOCKS = MOE_ROWS // MOE_TM
ROUTER_PAD = 128


def _cparams(sem, vmem_mb):
    return pltpu.CompilerParams(dimension_semantics=sem, vmem_limit_bytes=vmem_mb * 1024 * 1024)


def _sigmoid(x):
    return 0.5 * jnp.tanh(0.5 * x) + 0.5


def _silu(x):
    return x * _sigmoid(x)


_NT = (((1,), (1,)), ((), ()))
_TN = (((0,), (0,)), ((), ()))


def _dot(a, b):
    return jnp.dot(a, b, preferred_element_type=F32)


def _dot_nt(a, b):
    return lax.dot_general(a, b, _NT, preferred_element_type=F32)


def _dot_tn(a, b):
    return lax.dot_general(a, b, _TN, preferred_element_type=F32)


def _ada_kernel(c_ref, w_ref, b_ref, o_ref):
    a = _silu(c_ref[...]).astype(BF16)
    o_ref[0] = _dot(a, w_ref[0].astype(BF16)) + b_ref[0]


def _ada_mod(c16, ada_w, ada_b):
    depth, _, n = ada_w.shape
    tn = 1024
    return pl.pallas_call(
        _ada_kernel,
        grid=(depth, n // tn),
        in_specs=[
            pl.BlockSpec((16, D), lambda l, j: (0, 0)),
            pl.BlockSpec((1, D, tn), lambda l, j: (l, 0, j)),
            pl.BlockSpec((1, 1, tn), lambda l, j: (l, 0, j)),
        ],
        out_specs=pl.BlockSpec((1, 16, tn), lambda l, j: (l, 0, j)),
        out_shape=jax.ShapeDtypeStruct((depth, 16, n), F32),
        compiler_params=_cparams(("arbitrary", "arbitrary"), 40),
        name="ada_mod",
    )(c16, ada_w, ada_b.reshape(depth, 1, n))


def _mod_index(i):
    return (jnp.where(i % BLOCKS_PER_BATCH == 0, BATCH, i // BLOCKS_PER_BATCH), 0, 0)


def _rms(x):
    return x * lax.rsqrt(jnp.mean(x * x, axis=-1, keepdims=True) + EPS)


def _first_norm_kernel(ctx_ref, x_ref, mul_ref, add_ref, tok_ref, h_ref):
    is_ctx = pl.program_id(0) % BLOCKS_PER_BATCH == 0
    t = jnp.where(is_ctx, ctx_ref[0], x_ref[0])
    tok_ref[...] = t
    h_ref[...] = (_rms(t) * mul_ref[0] + add_ref[0]).astype(BF16)


def _first_norm(x, ctx, mul, add):
    spec_mod = pl.BlockSpec((1, 1, D), _mod_index)
    row = pl.BlockSpec((RB, D), lambda i: (i, 0))
    return pl.pallas_call(
        _first_norm_kernel,
        grid=(M // RB,),
        in_specs=[pl.BlockSpec((1, CTX, D), lambda i: (i // BLOCKS_PER_BATCH, 0, 0)),
                  pl.BlockSpec((1, RB, D), lambda i: (i // BLOCKS_PER_BATCH,
                                                      jnp.maximum(i % BLOCKS_PER_BATCH - 1, 0), 0)),
                  spec_mod, spec_mod],
        out_specs=[row, row],
        out_shape=[jax.ShapeDtypeStruct((M, D), F32), jax.ShapeDtypeStruct((M, D), BF16)],
        compiler_params=_cparams(("arbitrary",), 32),
        name="first_norm",
    )(ctx, x, mul, add)


def _matmul_kernel(a_ref, w_ref, o_ref):
    o_ref[...] = _dot(a_ref[...], w_ref[...]).astype(o_ref.dtype)


def _matmul(a, w, out_dtype):
    m, k = a.shape
    n = w.shape[1]
    tm, tn = TB, 512
    return pl.pallas_call(
        _matmul_kernel,
        grid=(m // tm, n // tn),
        in_specs=[pl.BlockSpec((tm, k), lambda i, j: (i, 0)), pl.BlockSpec((k, tn), lambda i, j: (0, j))],
        out_specs=pl.BlockSpec((tm, tn), lambda i, j: (i, j)),
        out_shape=jax.ShapeDtypeStruct((m, n), out_dtype),
        compiler_params=_cparams(("arbitrary", "arbitrary"), 48),
        name="proj_in",
    )(a, w)


OUT_TM = TB // 8


def _out_proj_kernel(a1_ref, a2_ref, w1_ref, w2_ref, res_ref, gl_ref, gc_ref, ml_ref, mc_ref, al_ref, ac_ref,
                     whi_ref, wlo_ref, rb_ref, tok_ref, h_ref, lg_ref):
    acc = _dot(a1_ref[...], w1_ref[...]) + _dot(a2_ref[...], w2_ref[...])
    row = pl.program_id(0) * OUT_TM + lax.broadcasted_iota(jnp.int32, (OUT_TM, 1), 0)
    is_ctx = row % TB < CTX
    t = res_ref[...] + jnp.where(is_ctx, gc_ref[0], gl_ref[0]) * acc
    tok_ref[...] = t
    h = _rms(t) * jnp.where(is_ctx, mc_ref[0], ml_ref[0]) + jnp.where(is_ctx, ac_ref[0], al_ref[0])
    h_ref[...] = h
    hi = h.astype(BF16)
    lo = (h - hi.astype(F32)).astype(BF16)
    lg_ref[...] = (_dot(hi, whi_ref[...]) + _dot(hi, wlo_ref[...]) + _dot(lo, whi_ref[...])) + rb_ref[...]


def _out_proj(a1, a2, w1, w2, tok, gate, mul, add, w_hi, w_lo, r_bias):
    tm = OUT_TM
    k1, k2 = a1.shape[1], a2.shape[1]
    row = lambda n: pl.BlockSpec((tm, n), lambda i: (i, 0))
    whole = lambda shape: pl.BlockSpec(shape, lambda i: (0,) * len(shape))
    lat = pl.BlockSpec((1, 1, D), lambda i: (i // (TB // tm), 0, 0))
    cx = pl.BlockSpec((1, 1, D), lambda i: (BATCH, 0, 0))
    return pl.pallas_call(
        _out_proj_kernel,
        grid=(M // tm,),
        in_specs=[row(k1), row(k2), whole((k1, D)), whole((k2, D)), row(D), lat, cx, lat, cx, lat, cx,
                  whole((D, ROUTER_PAD)), whole((D, ROUTER_PAD)), whole((1, ROUTER_PAD))],
        out_specs=[row(D), row(D), row(ROUTER_PAD)],
        out_shape=[jax.ShapeDtypeStruct((M, D), F32), jax.ShapeDtypeStruct((M, D), F32),
                   jax.ShapeDtypeStruct((M, ROUTER_PAD), F32)],
        compiler_params=_cparams(("arbitrary",), 56),
        name="proj_out",
    )(a1, a2, w1, w2, tok, gate, gate, mul, mul, add, add, w_hi, w_lo, r_bias)


def _rope_tables(dim, reps):
    n_f = dim // 4
    inv = ROPE_BASE ** (-jnp.arange(n_f, dtype=F32) / n_f)
    t = jnp.arange(SEQ)
    pos = jnp.stack([t // GRID_W, t % GRID_W], axis=-1).astype(F32)
    ang = pos[:, :, None] * inv
    cos = jnp.cos(ang)[:, :, None, :]
    sin = jnp.sin(ang)[:, :, None, :]
    cos = jnp.broadcast_to(cos, (SEQ, 2, 2, n_f)).reshape(SEQ, dim)
    sin = jnp.concatenate([-sin, sin], axis=2).reshape(SEQ, dim)
    cos = jnp.tile(cos, (1, reps))
    sin = jnp.tile(sin, (1, reps))
    cos = jnp.concatenate([jnp.ones((CTX, dim * reps), F32), cos], axis=0)
    sin = jnp.concatenate([jnp.zeros((CTX, dim * reps), F32), sin], axis=0)
    return cos, sin


def _rope(x, cos, sin_signed, n_f):
    lane = lax.broadcasted_iota(jnp.int32, x.shape, x.ndim - 1)
    partner = jnp.where(lane % (2 * n_f) < n_f,
                        pltpu.roll(x, HEAD_DIM - n_f, axis=x.ndim - 1),
                        pltpu.roll(x, n_f, axis=x.ndim - 1))
    return x * cos + partner * sin_signed


_NA_GROUP_ROWS = RB // GRID_W
_NA_KEY_ROWS = 3 * _NA_GROUP_ROWS
_NA_PATTERN = (0, 1, 2, 2, 2, 2, 2, 3, 4)


def _na_key_block(g):
    return jnp.clip(g - 1, 0, GRID_H // _NA_GROUP_ROWS - 3)


def _na_bias_tables(rpb):
    c = np.arange(GRID_W)
    cs = np.clip(c - NA_WIN_W // 2, 0, GRID_W - NA_WIN_W)
    col_valid = (c[None, :] >= cs[:, None]) & (c[None, :] < cs[:, None] + NA_WIN_W)
    co = np.clip(c[None, :] - c[:, None] + NA_WIN_W - 1, 0, 2 * NA_WIN_W - 2)
    onehot = jnp.asarray((co[None] == np.arange(2 * NA_WIN_W - 1)[:, None, None]) & col_valid[None], F32)
    tcol = jnp.einsum("hro,oqk->hrqk", rpb.astype(F32), onehot, precision=lax.Precision.HIGHEST)
    tcol = jnp.where(jnp.asarray(col_valid)[None, None], tcol, NEG)
    neg_blk = jnp.full((NH, GRID_W, GRID_W), NEG, F32)
    tabs = []
    for g in (0, 1, 2, 7):
        r0 = _NA_GROUP_ROWS * g
        ks = int(np.clip(g - 1, 0, GRID_H // _NA_GROUP_ROWS - 3)) * _NA_GROUP_ROWS
        rows = []
        for dr in range(_NA_GROUP_ROWS):
            r = r0 + dr
            rs = int(np.clip(r - NA_WIN_H // 2, 0, GRID_H - NA_WIN_H))
            blks = []
            for krel in range(_NA_KEY_ROWS):
                kr = ks + krel
                blks.append(tcol[:, kr - r + NA_WIN_H - 1] if rs <= kr < rs + NA_WIN_H else neg_blk)
            rows.append(jnp.concatenate(blks, axis=-1))
        tabs.append(jnp.concatenate(rows, axis=-2))
    tabs.append(jnp.full((NH, RB, 3 * RB), NEG, F32))
    return jnp.stack(tabs, axis=1)


_NA_HPB = 2


def _na_kernel(q_ref, k0_ref, k1_ref, k2_ref, v0_ref, v1_ref, v2_ref, kc_ref, vc_ref, b_ref, o_ref):
    for hh in range(_NA_HPB):
        ln = slice(hh * HEAD_DIM, (hh + 1) * HEAD_DIM)
        q = (q_ref[:, ln].astype(F32) * HEAD_DIM ** -0.5).astype(BF16)
        s = [_dot_nt(q, kr[:, ln]) + b_ref[hh, 0, :, t * RB:(t + 1) * RB]
             for t, kr in enumerate((k0_ref, k1_ref, k2_ref))]
        s.append(_dot_nt(q, kc_ref[:, ln]))
        m = functools.reduce(jnp.maximum, [jnp.max(t, axis=-1, keepdims=True) for t in s])
        p = [jnp.exp(t - m) for t in s]
        l = functools.reduce(jnp.add, [jnp.sum(t, axis=-1, keepdims=True) for t in p])
        vs = (v0_ref, v1_ref, v2_ref, vc_ref)
        o = functools.reduce(jnp.add, [_dot(pt.astype(BF16), vr[:, ln]) for pt, vr in zip(p, vs)])
        o_ref[:, ln] = (o / l).astype(o_ref.dtype)


def _na_attention(proj, bias):
    hb = NH // _NA_HPB

    def qrow(h, g, b):
        return b * BLOCKS_PER_BATCH + jnp.where(g < 8, 1 + g, 0)

    def krow(t):
        return lambda h, g, b: (b * BLOCKS_PER_BATCH + 1 + _na_key_block(jnp.minimum(g, 7)) + t, hb + h)

    def vrow(t):
        return lambda h, g, b: (b * BLOCKS_PER_BATCH + 1 + _na_key_block(jnp.minimum(g, 7)) + t, 2 * hb + h)

    blk = (RB, _NA_HPB * HEAD_DIM)
    pattern = jnp.asarray(_NA_PATTERN, jnp.int32)

    def bias_map(h, g, b, pat_ref):
        return (h, pat_ref[g], 0, 0)

    def wrap(f):
        return lambda h, g, b, pat_ref: f(h, g, b)

    in_specs = [pl.BlockSpec(blk, wrap(lambda h, g, b: (qrow(h, g, b), h)))]
    in_specs += [pl.BlockSpec(blk, wrap(krow(t))) for t in range(3)]
    in_specs += [pl.BlockSpec(blk, wrap(vrow(t))) for t in range(3)]
    in_specs += [pl.BlockSpec(blk, wrap(lambda h, g, b: (b * BLOCKS_PER_BATCH, hb + h))),
                 pl.BlockSpec(blk, wrap(lambda h, g, b: (b * BLOCKS_PER_BATCH, 2 * hb + h))),
                 pl.BlockSpec((_NA_HPB, 1, RB, 3 * RB), bias_map)]

    def body(pat_ref, *refs):
        _na_kernel(*refs)

    return pl.pallas_call(
        body,
        grid_spec=pltpu.PrefetchScalarGridSpec(
            num_scalar_prefetch=1,
            grid=(hb, BLOCKS_PER_BATCH, BATCH),
            in_specs=in_specs,
            out_specs=pl.BlockSpec(blk, wrap(lambda h, g, b: (qrow(h, g, b), h))),
        ),
        out_shape=jax.ShapeDtypeStruct((M, NH * HEAD_DIM), BF16),
        compiler_params=_cparams(("arbitrary", "arbitrary", "arbitrary"), 32),
        name="na_attention",
    )(pattern, *([proj] * 9), bias)


_RET_NC = TB // RET_CHUNK
_RET_CTX_NC = CTX // RET_CHUNK


def _ret_kernel(lg_ref, q_ref, k_ref, v_ref, g_ref, cos_ref, sin_ref, o_ref, qr_ref, kr_ref, sf_ref, sb_ref):
    h = pl.program_id(1)
    lgf = lg_ref[0, h]
    lgb = lg_ref[1, h]
    c = RET_CHUNK
    qr_ref[...] = _rope(q_ref[...].astype(F32), cos_ref[...], sin_ref[...], HEAD_DIM // 4)
    kr_ref[...] = _rope(k_ref[...].astype(F32), cos_ref[...], sin_ref[...], HEAD_DIM // 4) * HEAD_DIM ** -0.5

    pos = lax.broadcasted_iota(jnp.int32, (c, 1), 0).astype(F32)
    kdec_f = jnp.exp(lgf * (c - 1.0 - pos))
    kdec_b = jnp.exp(lgb * pos)
    qdec_f = jnp.exp(lgf * (pos + 1.0))
    qdec_b = jnp.exp(lgb * (c - pos))
    gc_f = jnp.exp(lgf * c)
    gc_b = jnp.exp(lgb * c)
    rel = (lax.broadcasted_iota(jnp.int32, (c, c), 0) - lax.broadcasted_iota(jnp.int32, (c, c), 1)).astype(F32)
    dmask = (jnp.where(rel >= 0, jnp.exp(jnp.maximum(rel, 0.0) * lgf), 0.0)
             + jnp.where(rel <= 0, jnp.exp(jnp.maximum(-rel, 0.0) * lgb), 0.0))

    def kv_step(n, carry):
        rows = pl.ds(pl.multiple_of(n * c, c), c)
        kn = kr_ref[rows, :]
        vn = v_ref[rows, :]
        sf_ref[n] = _dot_tn((kn * kdec_f).astype(BF16), vn)
        sb_ref[n] = _dot_tn((kn * kdec_b).astype(BF16), vn)
        return carry

    lax.fori_loop(0, _RET_NC, kv_step, 0, unroll=3)

    def state_step(n, s, s_ref, gc):
        u = s_ref[n]
        s_ref[n] = s
        return s * gc + u

    zero = jnp.zeros((HEAD_DIM, RET_DV), F32)
    lax.fori_loop(0, _RET_NC, lambda n, s: state_step(n, s, sf_ref, gc_f), zero)
    s = lax.fori_loop(0, _RET_CTX_NC, lambda i, s: state_step(_RET_CTX_NC - 1 - i, s, sb_ref, gc_b), zero)
    lax.fori_loop(0, _RET_NC - _RET_CTX_NC, lambda i, s: state_step(_RET_NC - 1 - i, s, sb_ref, gc_b), s)

    def out_step(n, carry):
        rows = pl.ds(pl.multiple_of(n * c, c), c)
        qn = qr_ref[rows, :]
        vn = v_ref[rows, :]
        scores = _dot_nt(qn.astype(BF16), kr_ref[rows, :].astype(BF16)) * dmask
        intra = _dot(scores.astype(BF16), vn)
        qcat = jnp.concatenate([qn * qdec_f, qn * qdec_b], axis=1).astype(BF16)
        scat = jnp.concatenate([sf_ref[n], sb_ref[n]], axis=0).astype(BF16)
        o = intra + _dot(qcat, scat)
        o = o * lax.rsqrt(jnp.mean(o * o, axis=-1, keepdims=True) + EPS)
        o_ref[rows, :] = (o * _silu(g_ref[rows, :].astype(F32))).astype(o_ref.dtype)
        return carry

    lax.fori_loop(0, _RET_NC, out_step, 0, unroll=3)


def _retention(proj, log_g, cos, sin):
    seq128 = lambda blk0: pl.BlockSpec((TB, HEAD_DIM), lambda b, h, lg: (b, blk0 + h))
    seq256 = lambda blk0: pl.BlockSpec((TB, RET_DV), lambda b, h, lg: (b, blk0 + h))
    tab = pl.BlockSpec((TB, HEAD_DIM), lambda b, h, lg: (0, 0))
    return pl.pallas_call(
        _ret_kernel,
        grid_spec=pltpu.PrefetchScalarGridSpec(
            num_scalar_prefetch=1,
            grid=(BATCH, NH),
            in_specs=[seq128(3072 // HEAD_DIM), seq128(4096 // HEAD_DIM),
                      seq256(5120 // RET_DV), seq256(7168 // RET_DV), tab, tab],
            out_specs=pl.BlockSpec((TB, RET_DV), lambda b, h, lg: (b, h)),
            scratch_shapes=[pltpu.VMEM((TB, HEAD_DIM), F32), pltpu.VMEM((TB, HEAD_DIM), F32),
                            pltpu.VMEM((_RET_NC, HEAD_DIM, RET_DV), F32),
                            pltpu.VMEM((_RET_NC, HEAD_DIM, RET_DV), F32)],
        ),
        out_shape=jax.ShapeDtypeStruct((M, NH * RET_DV), BF16),
        compiler_params=_cparams(("arbitrary", "arbitrary"), 48),
        name="retention",
    )(log_g, proj, proj, proj, proj, cos, sin)


def _diff_kernel(lam_ref, q_ref, k_ref, v_ref, cq_ref, sq_ref, ck_ref, sk_ref, o_ref, kb_ref, vb_ref, *, post_scale):
    qi = pl.program_id(2)

    @pl.when(qi == 0)
    def _():
        kb_ref[...] = _rope(k_ref[...].astype(F32), ck_ref[...], sk_ref[...], DIFF_DK // 4).astype(BF16)
        vb_ref[...] = v_ref[...].astype(BF16)
        o_ref[...] = jnp.zeros_like(o_ref)

    @pl.when(qi > 0)
    def _():
        lam = lam_ref[0]
        q = _rope(q_ref[...].astype(F32), cq_ref[...], sq_ref[...], DIFF_DK // 4) * (DIFF_DK ** -0.5 * LOG2E)
        lane = lax.broadcasted_iota(jnp.int32, q.shape, 1)
        kb = kb_ref[...]
        parts = []
        for t in range(2):
            qt = jnp.where((lane >= DIFF_DK) == (t == 1), q, 0.0).astype(BF16)
            s = _dot_nt(qt, kb)
            e = jnp.exp2(s - jnp.max(s, axis=-1, keepdims=True))
            inv = 1.0 / jnp.sum(e, axis=-1, keepdims=True)
            parts.append(_dot(e.astype(BF16), vb_ref[...]) * inv)
        o = parts[0] - lam * parts[1]
        o = o * lax.rsqrt(jnp.mean(o * o, axis=-1, keepdims=True) + EPS)
        o_ref[...] = (o * post_scale).astype(o_ref.dtype)


def _diff_attention(proj, lam, cos, sin, post_scale):
    blk = (RB, HEAD_DIM)
    seq = (TB, HEAD_DIM)
    qmap = lambda b, h, qi, lam_ref: (b * BLOCKS_PER_BATCH + qi, h)
    return pl.pallas_call(
        functools.partial(_diff_kernel, post_scale=post_scale),
        grid_spec=pltpu.PrefetchScalarGridSpec(
            num_scalar_prefetch=1,
            grid=(BATCH, NH, BLOCKS_PER_BATCH),
            in_specs=[pl.BlockSpec(blk, qmap),
                      pl.BlockSpec(seq, lambda b, h, qi, lam_ref: (b, NH + h)),
                      pl.BlockSpec(seq, lambda b, h, qi, lam_ref: (b, 2 * NH + h)),
                      pl.BlockSpec(blk, lambda b, h, qi, lam_ref: (qi, 0)),
                      pl.BlockSpec(blk, lambda b, h, qi, lam_ref: (qi, 0)),
                      pl.BlockSpec(seq, lambda b, h, qi, lam_ref: (0, 0)),
                      pl.BlockSpec(seq, lambda b, h, qi, lam_ref: (0, 0))],
            out_specs=pl.BlockSpec(blk, qmap),
            scratch_shapes=[pltpu.VMEM(seq, BF16), pltpu.VMEM(seq, BF16)],
        ),
        out_shape=jax.ShapeDtypeStruct((M, NH * HEAD_DIM), BF16),
        compiler_params=_cparams(("arbitrary", "arbitrary", "arbitrary"), 48),
        name="diff_attention",
    )(lam, proj, proj, proj, cos, sin, cos, sin)


_HG_NC = TB // HG_CHUNK
_HG_CTX_NC = CTX // HG_CHUNK


def _chunk_cumsum(x, reverse):
    c = x.shape[0]
    row = lax.broadcasted_iota(jnp.int32, x.shape, 0)
    s = 1
    while s < c:
        if reverse:
            x = x + jnp.where(row < c - s, pltpu.roll(x, c - s, axis=0), 0.0)
        else:
            x = x + jnp.where(row >= s, pltpu.roll(x, s, axis=0), 0.0)
        s *= 2
    return x


def _hg_kernel(q_ref, ff_ref, fb_ref, v_ref, g_ref, lb_ref, o_ref,
               qs_ref, kf_ref, kb_ref, cf_ref, cb_ref, sf_ref, sb_ref, df_ref, db_ref):
    c = HG_CHUNK
    lb = lb_ref[0]
    qraw = q_ref[...].astype(F32)
    qs_ref[...] = qraw * _sigmoid(qraw)

    def gate(raw_ref, k_ref, lf_ref):
        f = lb + (1.0 - lb) * _sigmoid(raw_ref[...].astype(F32))
        k_ref[...] = 1.0 - f
        lf_ref[...] = jnp.log(f)

    gate(ff_ref, kf_ref, cf_ref)
    gate(fb_ref, kb_ref, cb_ref)

    def chunk_prep(n, carry):
        rows = pl.ds(pl.multiple_of(n * c, c), c)
        vn = v_ref[rows, :]
        for cum_ref, k_ref, u_ref, d_ref, reverse in ((cf_ref, kf_ref, sf_ref, df_ref, False),
                                                      (cb_ref, kb_ref, sb_ref, db_ref, True)):
            cum = _chunk_cumsum(cum_ref[rows, :], reverse)
            cum_ref[rows, :] = cum
            last = cum[0:1, :] if reverse else cum[c - 1:c, :]
            u_ref[n] = _dot_tn(vn, (k_ref[rows, :] * jnp.exp(last - cum)).astype(BF16))
            d_ref[n] = jnp.exp(last)
        return carry

    lax.fori_loop(0, _HG_NC, chunk_prep, 0, unroll=4)

    def state_step(n, st, s_ref, d_ref):
        u = s_ref[n]
        s_ref[n] = st
        return st * d_ref[n] + u

    zero = jnp.zeros((HEAD_DIM, HEAD_DIM), F32)
    lax.fori_loop(0, _HG_NC, lambda n, s: state_step(n, s, sf_ref, df_ref), zero)
    s = lax.fori_loop(0, _HG_CTX_NC, lambda i, s: state_step(_HG_CTX_NC - 1 - i, s, sb_ref, db_ref), zero)
    lax.fori_loop(0, _HG_NC - _HG_CTX_NC, lambda i, s: state_step(_HG_NC - 1 - i, s, sb_ref, db_ref), s)

    o_ref[0:CTX, :] = jnp.zeros((CTX, HEAD_DIM), o_ref.dtype)

    def level(q, kf, kb, cf, cb, half):
        span = 2 * half
        row = lax.broadcasted_iota(jnp.int32, (c, 1), 0)
        upper = (row % span) >= half
        ref_f = jnp.concatenate([jnp.broadcast_to(cf[b0 + half - 1:b0 + half, :], (span, HEAD_DIM))
                                 for b0 in range(0, c, span)], axis=0)
        ref_b = jnp.concatenate([jnp.broadcast_to(cb[b0 + half:b0 + half + 1, :], (span, HEAD_DIM))
                                 for b0 in range(0, c, span)], axis=0)
        ef = jnp.exp(jnp.minimum(jnp.where(upper, cf - ref_f, ref_f - cf), 0.0))
        eb = jnp.exp(jnp.minimum(jnp.where(upper, ref_b - cb, cb - ref_b), 0.0))
        qq = jnp.concatenate([jnp.where(upper, q * ef, 0.0), jnp.where(upper, 0.0, q * eb)], axis=1)
        kk = jnp.concatenate([jnp.where(upper, 0.0, kf * ef), jnp.where(upper, kb * eb, 0.0)], axis=1)
        a = _dot_nt(qq.astype(BF16), kk.astype(BF16))
        if span == c:
            return a
        ri = lax.broadcasted_iota(jnp.int32, (c, c), 0)
        ci = lax.broadcasted_iota(jnp.int32, (c, c), 1)
        return jnp.where(ri // span == ci // span, a, 0.0)

    def sub_block(q, kf, kb, lo):
        sub = HG_SUB
        row = lax.broadcasted_iota(jnp.int32, (sub, 1), 0)
        lane = lax.broadcasted_iota(jnp.int32, (sub, c), 1)
        qb, kfb, kbb = q[lo:lo + sub], kf[lo:lo + sub], kb[lo:lo + sub]
        ffb, fbb = 1.0 - kfb, 1.0 - kbb
        pf = [None] * sub
        p = jnp.where(row == sub - 1, 1.0, jnp.zeros((sub, HEAD_DIM), F32))
        pf[sub - 1] = p
        for j in range(sub - 2, -1, -1):
            p = jnp.where(row == j, 1.0, p * ffb[j + 1:j + 2])
            pf[j] = p
        att = jnp.zeros((sub, c), F32)
        p = None
        for j in range(sub):
            p = (jnp.where(row == 0, 1.0, jnp.zeros((sub, HEAD_DIM), F32)) if j == 0
                 else jnp.where(row == j, 1.0, p * fbb[j - 1:j]))
            col = jnp.sum(qb * (pf[j] * kfb[j:j + 1] + p * kbb[j:j + 1]), axis=-1, keepdims=True)
            att = jnp.where(lane == lo + j, col, att)
        return att

    def out_step(n, carry):
        rows = pl.ds(pl.multiple_of(n * c, c), c)
        q = qs_ref[rows, :]
        kf = kf_ref[rows, :]
        kb = kb_ref[rows, :]
        cf = cf_ref[rows, :]
        cb = cb_ref[rows, :]
        qcat = jnp.concatenate([q * jnp.exp(cf), q * jnp.exp(cb)], axis=1).astype(BF16)
        scat = jnp.concatenate([sf_ref[n], sb_ref[n]], axis=1).astype(BF16)
        o = _dot_nt(qcat, scat)
        att = jnp.concatenate([sub_block(q, kf, kb, lo) for lo in range(0, c, HG_SUB)], axis=0)
        half = c // 2
        while half >= HG_SUB:
            att = att + level(q, kf, kb, cf, cb, half)
            half //= 2
        o = o + _dot(att.astype(BF16), v_ref[rows, :])
        o = o * lax.rsqrt(jnp.mean(o * o, axis=-1, keepdims=True) + EPS)
        graw = g_ref[rows, :].astype(F32)
        o_ref[rows, :] = (o * (graw * _sigmoid(graw))).astype(o_ref.dtype)
        return carry

    lax.fori_loop(_HG_CTX_NC, _HG_NC, out_step, 0, unroll=4)


def _hgrn(proj, lb):
    seq = lambda col0: pl.BlockSpec((TB, HEAD_DIM), lambda b, h: (b, col0 // HEAD_DIM + h))
    full = pltpu.VMEM((TB, HEAD_DIM), F32)
    states = pltpu.VMEM((_HG_NC, HEAD_DIM, HEAD_DIM), F32)
    decays = pltpu.VMEM((_HG_NC, 1, HEAD_DIM), F32)
    return pl.pallas_call(
        _hg_kernel,
        grid=(BATCH, NH),
        in_specs=[seq(3072), seq(4096), seq(5120), seq(6144), seq(7168),
                  pl.BlockSpec((1, 1, HEAD_DIM), lambda b, h: (h, 0, 0))],
        out_specs=pl.BlockSpec((TB, HEAD_DIM), lambda b, h: (b, h)),
        out_shape=jax.ShapeDtypeStruct((M, NH * HEAD_DIM), BF16),
        scratch_shapes=[full, full, full, full, full, states, states, decays, decays],
        compiler_params=_cparams(("arbitrary", "arbitrary"), 48),
        name="hgrn",
    )(proj, proj, proj, proj, proj, lb.reshape(NH, 1, HEAD_DIM))


BULK_DMA_PRIORITY = 1


def _row_gather_start(idx_ref, k, src_hbm, dst, sem, n, alternate=False):
    for r in range(n):
        cp = pltpu.make_async_copy(src_hbm.at[pl.ds(idx_ref[0, k, r], 1)], dst.at[pl.ds(r, 1)], sem)
        cp.start(priority=r % 2 if alternate else 0)


def _row_gather_wait(src_hbm, dst, sem, n):
    pltpu.make_async_copy(src_hbm.at[pl.ds(0, n)], dst, sem).wait()


def _expert_weights_step(i, be_ref, nx_ref, w_hbms, wst, wsem, wb_refs, layer):
    e = be_ref[i]

    def fetch(ex):
        return [pltpu.make_async_copy(w.at[layer, ex], wst.at[k], wsem.at[k]) for k, w in enumerate(w_hbms)]

    @pl.when(i == 0)
    def _():
        for cp in fetch(e):
            cp.start(priority=BULK_DMA_PRIORITY)

    @pl.when(jnp.logical_or(i == 0, e != be_ref[jnp.maximum(i - 1, 0)]))
    def _():
        for cp in fetch(e):
            cp.wait()
        for k, wb in enumerate(wb_refs):
            wb[...] = wst[k].astype(BF16)
        nx = nx_ref[i]

        @pl.when(nx != e)
        def _():
            for cp in fetch(nx):
                cp.start(priority=BULK_DMA_PRIORITY)


def _moe_up_kernel(be_ref, nx_ref, nu_ref, rt0_ref, rt1_ref, x_hbm, w1_hbm, w3_hbm, o_ref,
                   xbuf, sem, wst, wsem, w1b_ref, w3b_ref, *, layer):
    i = pl.program_id(0)
    nu = nu_ref[0]

    @pl.when(jnp.logical_and(i == 0, nu > 0))
    def _():
        _row_gather_start(rt0_ref, 0, x_hbm, xbuf.at[0], sem.at[0], MOE_TM)

    @pl.when(i + 1 < nu)
    def _():
        slot = (i + 1) % 2
        _row_gather_start(rt1_ref, 0, x_hbm, xbuf.at[slot], sem.at[slot], MOE_TM)

    _expert_weights_step(i, be_ref, nx_ref, (w1_hbm, w3_hbm), wst, wsem, (w1b_ref, w3b_ref), layer)

    @pl.when(i < nu)
    def _():
        slot = i % 2
        _row_gather_wait(x_hbm, xbuf.at[slot], sem.at[slot], MOE_TM)
        x = xbuf[slot].astype(BF16)
        a = _dot(x, w1b_ref[...])
        o_ref[...] = (_silu(a) * _dot(x, w3b_ref[...])).astype(o_ref.dtype)

    @pl.when(i >= nu)
    def _():
        o_ref[...] = jnp.zeros_like(o_ref)


def _moe_up(layer, blk_e, nxt_e, n_used, row_tok, h2, w1, w3):
    rt = row_tok.reshape(MOE_BLOCKS, 1, MOE_TM)
    smem_blk = lambda f: pl.BlockSpec((1, 1, MOE_TM), f, memory_space=pltpu.SMEM)
    hbm = pl.BlockSpec(memory_space=pl.ANY)
    return pl.pallas_call(
        functools.partial(_moe_up_kernel, layer=layer),
        grid_spec=pltpu.PrefetchScalarGridSpec(
            num_scalar_prefetch=3,
            grid=(MOE_BLOCKS,),
            in_specs=[smem_blk(lambda i, be, nx, nu: (i, 0, 0)),
                      smem_blk(lambda i, be, nx, nu: (jnp.minimum(i + 1, MOE_BLOCKS - 1), 0, 0)),
                      hbm, hbm, hbm],
            out_specs=pl.BlockSpec((MOE_TM, EXPERT_FF), lambda i, be, nx, nu: (i, 0)),
            scratch_shapes=[pltpu.VMEM((2, MOE_TM, D), F32), pltpu.SemaphoreType.DMA((2,)),
                            pltpu.VMEM((2, D, EXPERT_FF), F32), pltpu.SemaphoreType.DMA((2,)),
                            pltpu.VMEM((D, EXPERT_FF), BF16), pltpu.VMEM((D, EXPERT_FF), BF16)],
        ),
        out_shape=jax.ShapeDtypeStruct((MOE_ROWS, EXPERT_FF), BF16),
        compiler_params=_cparams(("arbitrary",), 44),
        name="moe_up",
    )(blk_e, nxt_e, n_used, rt, rt, h2, w1, w3)


def _moe_down_kernel(be_ref, nx_ref, nu_ref, h_ref, w2_hbm, o_ref, wst, wsem, w2b_ref, *, layer):
    i = pl.program_id(0)
    _expert_weights_step(i, be_ref, nx_ref, (w2_hbm,), wst, wsem, (w2b_ref,), layer)

    @pl.when(i < nu_ref[0])
    def _():
        o_ref[...] = _dot(h_ref[...], w2b_ref[...])

    @pl.when(i >= nu_ref[0])
    def _():
        o_ref[...] = jnp.zeros_like(o_ref)


def _moe_down(layer, blk_e, nxt_e, n_used, hmid, w2):
    return pl.pallas_call(
        functools.partial(_moe_down_kernel, layer=layer),
        grid_spec=pltpu.PrefetchScalarGridSpec(
            num_scalar_prefetch=3,
            grid=(MOE_BLOCKS,),
            in_specs=[pl.BlockSpec((MOE_TM, EXPERT_FF), lambda i, be, nx, nu: (i, 0)),
                      pl.BlockSpec(memory_space=pl.ANY)],
            out_specs=pl.BlockSpec((MOE_TM, D), lambda i, be, nx, nu: (i, 0)),
            scratch_shapes=[pltpu.VMEM((1, EXPERT_FF, D), F32), pltpu.SemaphoreType.DMA((1,)),
                            pltpu.VMEM((EXPERT_FF, D), BF16)],
        ),
        out_shape=jax.ShapeDtypeStruct((MOE_ROWS, D), F32),
        compiler_params=_cparams(("arbitrary",), 32),
        name="moe_down",
    )(blk_e, nxt_e, n_used, hmid, w2)


def _combine_kernel(p0_ref, p1_ref, y_hbm, x_ref, w_ref, g_ref, mul_ref, add_ref, *rest, last):
    out_refs, (ybuf, sem) = rest[:-2], rest[-2:]
    i = pl.program_id(0)
    n = pl.num_programs(0)

    def start(p_ref, slot):
        for k in range(TOP_K):
            _row_gather_start(p_ref, k, y_hbm, ybuf.at[slot, k], sem.at[slot], RB, alternate=True)

    @pl.when(i == 0)
    def _():
        start(p0_ref, 0)

    @pl.when(i + 1 < n)
    def _():
        start(p1_ref, (i + 1) % 2)

    slot = i % 2
    for k in range(TOP_K):
        _row_gather_wait(y_hbm, ybuf.at[slot, k], sem.at[slot], RB)
    w = w_ref[...]
    y = w[:, 0:1] * ybuf[slot, 0] + w[:, 1:2] * ybuf[slot, 1]
    t = x_ref[...] + g_ref[0] * y
    nrm = t * lax.rsqrt(jnp.mean(t * t, axis=-1, keepdims=True) + EPS) * mul_ref[0] + add_ref[0]
    if last:
        out_refs[0][0] = nrm
    else:
        out_refs[0][...] = t
        out_refs[1][...] = nrm.astype(BF16)


def _moe_combine(tok, y, pos, gates, gate_mod, mul, add, last):
    nb = M // RB
    pblk = pos.reshape(nb, RB, TOP_K).transpose(0, 2, 1)
    row = pl.BlockSpec((RB, D), lambda i: (i, 0))
    mod = pl.BlockSpec((1, 1, D), _mod_index)
    smem_blk = lambda f: pl.BlockSpec((1, TOP_K, RB), f, memory_space=pltpu.SMEM)
    if last:
        out_specs = pl.BlockSpec((1, RB, D), lambda i: (i // BLOCKS_PER_BATCH,
                                                        jnp.maximum(i % BLOCKS_PER_BATCH - 1, 0), 0))
        out_shape = jax.ShapeDtypeStruct((BATCH, SEQ, D), F32)
    else:
        out_specs = [row, row]
        out_shape = [jax.ShapeDtypeStruct((M, D), F32), jax.ShapeDtypeStruct((M, D), BF16)]
    return pl.pallas_call(
        functools.partial(_combine_kernel, last=last),
        grid=(nb,),
        in_specs=[smem_blk(lambda i: (i, 0, 0)), smem_blk(lambda i: (jnp.minimum(i + 1, nb - 1), 0, 0)),
                  pl.BlockSpec(memory_space=pl.ANY), row,
                  pl.BlockSpec((RB, TOP_K), lambda i: (i, 0)), mod, mod, mod],
        out_specs=out_specs,
        out_shape=out_shape,
        scratch_shapes=[pltpu.VMEM((2, TOP_K, RB, D), F32), pltpu.SemaphoreType.DMA((2,))],
        compiler_params=_cparams(("arbitrary",), 48),
        name="moe_combine",
    )(pblk, pblk, y, tok, gates, gate_mod, mul, add)


def _route(logits, valid):
    lg_g = logits[:, :N_GROUPS]
    p_g = jax.nn.softmax(lg_g, axis=-1)
    g_top = jnp.argmax(lg_g, axis=-1)
    w_grp = jnp.take_along_axis(p_g, g_top[:, None], axis=1)
    lg_e = logits[:, N_GROUPS:N_GROUPS + N_EXPERTS].reshape(M, N_GROUPS, EXPERTS_PER_GROUP)
    lg_in = jnp.take_along_axis(lg_e, g_top[:, None, None], axis=1)[:, 0]
    top_v, top_i = lax.top_k(lg_in, TOP_K)
    gates = w_grp * jax.nn.softmax(top_v, axis=-1)
    gates = jnp.where(valid[:, None], gates, 0.0)
    eid = g_top[:, None] * EXPERTS_PER_GROUP + top_i
    eid = jnp.where(valid[:, None], eid, N_EXPERTS).reshape(-1).astype(jnp.int32)
    n_slots = M * TOP_K
    slot = jnp.arange(n_slots, dtype=jnp.int32)
    experts = jnp.arange(N_EXPERTS + 1, dtype=jnp.int32)
    counts = jnp.sum((eid[:, None] == experts[None, :]).astype(jnp.int32), axis=0)
    padded = (counts + MOE_TM - 1) // MOE_TM * MOE_TM
    p_end = jnp.cumsum(padded)
    p_start = p_end - padded
    u_start = jnp.cumsum(counts) - counts
    shift = p_start - u_start
    s_e, order = lax.sort((eid, slot), num_keys=1, is_stable=True)
    dest = slot + jnp.sum(jnp.where(s_e[:, None] == experts[None, :], shift[None, :], 0), axis=1)
    dest = jnp.where(s_e < N_EXPERTS, dest, jnp.clip(order, 0, jnp.maximum(p_end[N_EXPERTS - 1] - 1, 0)))
    _, pos = lax.sort((order, dest), num_keys=1)
    pos = pos.reshape(M, TOP_K)
    blk_start = jnp.arange(MOE_BLOCKS, dtype=jnp.int32) * MOE_TM
    blk_e = jnp.sum((p_end[None, :N_EXPERTS] <= blk_start[:, None]).astype(jnp.int32), axis=1)
    blk_e = jnp.minimum(blk_e, N_EXPERTS - 1)
    later = jnp.min(jnp.where(blk_e[None, :] > blk_e[:, None], blk_e[None, :], N_EXPERTS), axis=1)
    nxt_e = jnp.where(later < N_EXPERTS, later, blk_e)
    onehot_be = blk_e[:, None] == experts[None, :N_EXPERTS]
    blk_shift = jnp.sum(jnp.where(onehot_be, shift[None, :N_EXPERTS], 0), axis=1)
    blk_lim = jnp.sum(jnp.where(onehot_be, (p_start + counts)[None, :N_EXPERTS], 0), axis=1)
    rows = jnp.arange(MOE_ROWS, dtype=jnp.int32).reshape(MOE_BLOCKS, MOE_TM)
    src = jnp.clip(rows - blk_shift[:, None], 0, n_slots - 1)
    row_tok = jnp.where(rows < blk_lim[:, None], jnp.take(order // TOP_K, src), 0)
    n_used = (p_end[N_EXPERTS - 1] // MOE_TM).astype(jnp.int32).reshape(1)
    return gates, row_tok, pos, blk_e, nxt_e, n_used


def _router_tables(wg, bg, we, be):
    wr = jnp.zeros((D, ROUTER_PAD), F32).at[:, :N_GROUPS].set(wg).at[:, N_GROUPS:N_GROUPS + N_EXPERTS].set(we)
    w_hi = wr.astype(BF16)
    w_lo = (wr - w_hi.astype(F32)).astype(BF16)
    r_bias = jnp.zeros((1, ROUTER_PAD), F32).at[0, :N_GROUPS].set(bg).at[0, N_GROUPS:N_GROUPS + N_EXPERTS].set(be)
    return w_hi, w_lo, r_bias


def _moe(layer, tok, h2, logits, gate_mod, w1, w3, w2, valid, next_mul, next_add, last):
    gates, row_tok, pos, blk_e, nxt_e, n_used = _route(logits, valid)
    hmid = _moe_up(layer, blk_e, nxt_e, n_used, row_tok, h2, w1, w3)
    y = _moe_down(layer, blk_e, nxt_e, n_used, hmid, w2)
    return _moe_combine(tok, y, pos, gates, gate_mod, next_mul, next_add, last)


def _mod9(t):
    return t[:BATCH + 1].reshape(BATCH + 1, 1, D)


def kernel(x, c, ctx, c_ctx, ada_w, ada_b, norm1_w, norm2_w, w_in_even, w_out_even, na_rpb, ret_decay,
           w_in_odd, w_out_odd, diff_lambda, hg_lb_logits, router_g_w, router_g_b, router_e_w, router_e_b,
           moe_w1, moe_w3, moe_w2, norm_f_w):
    depth = ada_w.shape[0]
    c16 =jnp.concatenate([c, c_ctx[None], jnp.zeros((16 - BATCH - 1, D), F32)], axis=0)
    mod = _ada_mod(c16, ada_w, ada_b)
    lb_all = jnp.cumsum(jax.nn.softmax(hg_lb_logits.astype(F32), axis=0), axis=0)
    lb_all = lb_all - lb_all[0]
    is_latent = (jnp.arange(M) % TB) >= CTX
    all_rows = jnp.ones((M,), bool)

    mods = [[_mod9(t) for t in jnp.split(mod[l], 6, axis=-1)] for l in range(depth)]
    tok, h = _first_norm(x, ctx, norm1_w[0] * (1.0 + mods[0][1]), mods[0][0])
    for l in range(depth):
        with_ctx = l < depth - 1
        sh1, s1, g1, sh2, s2, g2 = mods[l]
        if with_ctx:
            next_mul, next_add = norm1_w[l + 1] * (1.0 + mods[l + 1][1]), mods[l + 1][0]
        else:
            next_mul = jnp.broadcast_to(norm_f_w.reshape(1, 1, D), (BATCH + 1, 1, D))
            next_add = jnp.zeros((BATCH + 1, 1, D), F32)
        j = l // 2
        if l % 2 == 0:
            proj = _matmul(h, w_in_even[j].astype(BF16), BF16)
            m1 = _na_attention(proj, _na_bias_tables(na_rpb[j]))
            cos, sin = _rope_tables(HEAD_DIM, 1)
            m2 = _retention(proj, -jnp.exp(ret_decay[j].astype(F32)), cos, sin)
            w_out = w_out_even[j].astype(BF16)
        else:
            proj = _matmul(h, w_in_odd[j].astype(BF16), BF16)
            lp = diff_lambda[j].astype(F32)
            lam_init = 0.8 - 0.6 * math.exp(-0.3 * l)
            lam = jnp.exp(jnp.sum(lp[0] * lp[1])) - jnp.exp(jnp.sum(lp[2] * lp[3])) + lam_init
            cos, sin = _rope_tables(DIFF_DK, 2)
            m1 = _diff_attention(proj, lam.reshape(1), cos, sin, 1.0 - lam_init)
            m2 = _hgrn(proj, lb_all[l])
            w_out = w_out_odd[j].astype(BF16)
        tok, h2, logits = _out_proj(m1, m2, w_out[:NH * HEAD_DIM], w_out[NH * HEAD_DIM:], tok, g1,
                                    norm2_w[l] * (1.0 + s2), sh2,
                                    *_router_tables(router_g_w[l], router_g_b[l], router_e_w[l], router_e_b[l]))
        res = _moe(l, tok, h2, logits, g2, moe_w1, moe_w3, moe_w2, all_rows if with_ctx else is_latent,
                   next_mul, next_add, not with_ctx)
        if with_ctx:
            tok, h = res
    return res
```

```python
import functools
import math

import numpy as np
import jax
import jax.numpy as jnp
from jax import lax
from jax.experimental import pallas as pl
from jax.experimental.pallas import tpu as pltpu

F32 = jnp.float32
BF16 = jnp.bfloat16

D = 2048
BATCH = 8
SEQ = 2048
CTX = 256
TB = SEQ + CTX
M = BATCH * TB
GRID_W = 64
GRID_H = SEQ // GRID_W
HEAD_DIM = 128
NH = 8
NA_WIN_H = 8
NA_WIN_W = 16
RET_DV = 256
RET_CHUNK = 128
DIFF_DK = 64
HG_CHUNK = 64
HG_SUB = 8
N_GROUPS = 4
EXPERTS_PER_GROUP = 8
N_EXPERTS = 32
TOP_K = 2
EXPERT_FF = 1024
ROPE_BASE = 10000.0
EPS = 1e-6
NEG = -1e30
LOG2E = math.log2(math.e)

RB = 256
BLOCKS_PER_BATCH = TB // RB
MOE_TM = 512
MOE_SUB = 256
MOE_ROWS = M * TOP_K + N_EXPERTS * MOE_TM
MOE_BLOCKS = MOE_ROWS // MOE_TM
ROUTER_PAD = 128


def _cparams(sem, vmem_mb):
    return pltpu.CompilerParams(dimension_semantics=sem, vmem_limit_bytes=vmem_mb * 1024 * 1024)


def _sigmoid(x):
    return 0.5 * jnp.tanh(0.5 * x) + 0.5


def _silu(x):
    return x * _sigmoid(x)


_NT = (((1,), (1,)), ((), ()))
_TN = (((0,), (0,)), ((), ()))


def _dot(a, b):
    return jnp.dot(a, b, preferred_element_type=F32)


def _dot_nt(a, b):
    return lax.dot_general(a, b, _NT, preferred_element_type=F32)


def _dot_tn(a, b):
    return lax.dot_general(a, b, _TN, preferred_element_type=F32)


def _ada_kernel(c_ref, w_ref, b_ref, o_ref):
    a = _silu(c_ref[...]).astype(BF16)
    o_ref[0] = _dot(a, w_ref[0].astype(BF16)) + b_ref[0]


def _ada_mod(c16, ada_w, ada_b):
    depth, _, n = ada_w.shape
    tn = 1024
    return pl.pallas_call(
        _ada_kernel,
        grid=(depth, n // tn),
        in_specs=[
            pl.BlockSpec((16, D), lambda l, j: (0, 0)),
            pl.BlockSpec((1, D, tn), lambda l, j: (l, 0, j)),
            pl.BlockSpec((1, 1, tn), lambda l, j: (l, 0, j)),
        ],
        out_specs=pl.BlockSpec((1, 16, tn), lambda l, j: (l, 0, j)),
        out_shape=jax.ShapeDtypeStruct((depth, 16, n), F32),
        compiler_params=_cparams(("arbitrary", "arbitrary"), 40),
        name="ada_mod",
    )(c16, ada_w, ada_b.reshape(depth, 1, n))


def _mod_index(i):
    return (jnp.where(i % BLOCKS_PER_BATCH == 0, BATCH, i // BLOCKS_PER_BATCH), 0, 0)


def _rms(x):
    return x * lax.rsqrt(jnp.mean(x * x, axis=-1, keepdims=True) + EPS)


def _first_norm_kernel(ctx_ref, x_ref, mul_ref, add_ref, tok_ref, h_ref):
    is_ctx = pl.program_id(0) % BLOCKS_PER_BATCH == 0
    t = jnp.where(is_ctx, ctx_ref[0], x_ref[0])
    tok_ref[...] = t
    h_ref[...] = (_rms(t) * mul_ref[0] + add_ref[0]).astype(BF16)


def _first_norm(x, ctx, mul, add):
    spec_mod = pl.BlockSpec((1, 1, D), _mod_index)
    row = pl.BlockSpec((RB, D), lambda i: (i, 0))
    return pl.pallas_call(
        _first_norm_kernel,
        grid=(M // RB,),
        in_specs=[pl.BlockSpec((1, CTX, D), lambda i: (i // BLOCKS_PER_BATCH, 0, 0)),
                  pl.BlockSpec((1, RB, D), lambda i: (i // BLOCKS_PER_BATCH,
                                                      jnp.maximum(i % BLOCKS_PER_BATCH - 1, 0), 0)),
                  spec_mod, spec_mod],
        out_specs=[row, row],
        out_shape=[jax.ShapeDtypeStruct((M, D), F32), jax.ShapeDtypeStruct((M, D), BF16)],
        compiler_params=_cparams(("arbitrary",), 32),
        name="first_norm",
    )(ctx, x, mul, add)


def _matmul_kernel(a_ref, w_ref, o_ref):
    o_ref[...] = _dot(a_ref[...], w_ref[...]).astype(o_ref.dtype)


def _matmul(a, w, out_dtype):
    m, k = a.shape
    n = w.shape[1]
    tm, tn = TB, 512
    return pl.pallas_call(
        _matmul_kernel,
        grid=(m // tm, n // tn),
        in_specs=[pl.BlockSpec((tm, k), lambda i, j: (i, 0)), pl.BlockSpec((k, tn), lambda i, j: (0, j))],
        out_specs=pl.BlockSpec((tm, tn), lambda i, j: (i, j)),
        out_shape=jax.ShapeDtypeStruct((m, n), out_dtype),
        compiler_params=_cparams(("arbitrary", "arbitrary"), 48),
        name="proj_in",
    )(a, w)


OUT_TM = TB // 8


def _out_proj_kernel(a1_ref, a2_ref, w1_ref, w2_ref, res_ref, gl_ref, gc_ref, ml_ref, mc_ref, al_ref, ac_ref,
                     whi_ref, wlo_ref, rb_ref, tok_ref, h_ref, lg_ref):
    acc = _dot(a1_ref[...], w1_ref[...]) + _dot(a2_ref[...], w2_ref[...])
    row = pl.program_id(0) * OUT_TM + lax.broadcasted_iota(jnp.int32, (OUT_TM, 1), 0)
    is_ctx = row % TB < CTX
    t = res_ref[...] + jnp.where(is_ctx, gc_ref[0], gl_ref[0]) * acc
    tok_ref[...] = t
    h = _rms(t) * jnp.where(is_ctx, mc_ref[0], ml_ref[0]) + jnp.where(is_ctx, ac_ref[0], al_ref[0])
    h_ref[...] = h
    hi = h.astype(BF16)
    lo = (h - hi.astype(F32)).astype(BF16)
    lg_ref[...] = (_dot(hi, whi_ref[...]) + _dot(hi, wlo_ref[...]) + _dot(lo, whi_ref[...])) + rb_ref[...]


def _out_proj(a1, a2, w1, w2, tok, gate, mul, add, w_hi, w_lo, r_bias):
    tm = OUT_TM
    k1, k2 = a1.shape[1], a2.shape[1]
    row = lambda n: pl.BlockSpec((tm, n), lambda i: (i, 0))
    whole = lambda shape: pl.BlockSpec(shape, lambda i: (0,) * len(shape))
    lat = pl.BlockSpec((1, 1, D), lambda i: (i // (TB // tm), 0, 0))
    cx = pl.BlockSpec((1, 1, D), lambda i: (BATCH, 0, 0))
    return pl.pallas_call(
        _out_proj_kernel,
        grid=(M // tm,),
        in_specs=[row(k1), row(k2), whole((k1, D)), whole((k2, D)), row(D), lat, cx, lat, cx, lat, cx,
                  whole((D, ROUTER_PAD)), whole((D, ROUTER_PAD)), whole((1, ROUTER_PAD))],
        out_specs=[row(D), row(D), row(ROUTER_PAD)],
        out_shape=[jax.ShapeDtypeStruct((M, D), F32), jax.ShapeDtypeStruct((M, D), F32),
                   jax.ShapeDtypeStruct((M, ROUTER_PAD), F32)],
        compiler_params=_cparams(("arbitrary",), 56),
        name="proj_out",
    )(a1, a2, w1, w2, tok, gate, gate, mul, mul, add, add, w_hi, w_lo, r_bias)


def _rope_tables(dim, reps):
    n_f = dim // 4
    inv = ROPE_BASE ** (-jnp.arange(n_f, dtype=F32) / n_f)
    t = jnp.arange(SEQ)
    pos = jnp.stack([t // GRID_W, t % GRID_W], axis=-1).astype(F32)
    ang = pos[:, :, None] * inv
    cos = jnp.cos(ang)[:, :, None, :]
    sin = jnp.sin(ang)[:, :, None, :]
    cos = jnp.broadcast_to(cos, (SEQ, 2, 2, n_f)).reshape(SEQ, dim)
    sin = jnp.concatenate([-sin, sin], axis=2).reshape(SEQ, dim)
    cos = jnp.tile(cos, (1, reps))
    sin = jnp.tile(sin, (1, reps))
    cos = jnp.concatenate([jnp.ones((CTX, dim * reps), F32), cos], axis=0)
    sin = jnp.concatenate([jnp.zeros((CTX, dim * reps), F32), sin], axis=0)
    return cos, sin


def _rope(x, cos, sin_signed, n_f):
    lane = lax.broadcasted_iota(jnp.int32, x.shape, x.ndim - 1)
    partner = jnp.where(lane % (2 * n_f) < n_f,
                        pltpu.roll(x, HEAD_DIM - n_f, axis=x.ndim - 1),
                        pltpu.roll(x, n_f, axis=x.ndim - 1))
    return x * cos + partner * sin_signed


_NA_GROUP_ROWS = RB // GRID_W
_NA_KEY_ROWS = 3 * _NA_GROUP_ROWS
_NA_PATTERN = (0, 1, 2, 2, 2, 2, 2, 3, 4)


def _na_key_block(g):
    return jnp.clip(g - 1, 0, GRID_H // _NA_GROUP_ROWS - 3)


def _na_bias_tables(rpb):
    c = np.arange(GRID_W)
    cs = np.clip(c - NA_WIN_W // 2, 0, GRID_W - NA_WIN_W)
    col_valid = (c[None, :] >= cs[:, None]) & (c[None, :] < cs[:, None] + NA_WIN_W)
    co = np.clip(c[None, :] - c[:, None] + NA_WIN_W - 1, 0, 2 * NA_WIN_W - 2)
    onehot = jnp.asarray((co[None] == np.arange(2 * NA_WIN_W - 1)[:, None, None]) & col_valid[None], F32)
    tcol = jnp.einsum("hro,oqk->hrqk", rpb.astype(F32), onehot, precision=lax.Precision.HIGHEST)
    tcol = jnp.where(jnp.asarray(col_valid)[None, None], tcol, NEG)
    neg_blk = jnp.full((NH, GRID_W, GRID_W), NEG, F32)
    tabs = []
    for g in (0, 1, 2, 7):
        r0 = _NA_GROUP_ROWS * g
        ks = int(np.clip(g - 1, 0, GRID_H // _NA_GROUP_ROWS - 3)) * _NA_GROUP_ROWS
        rows = []
        for dr in range(_NA_GROUP_ROWS):
            r = r0 + dr
            rs = int(np.clip(r - NA_WIN_H // 2, 0, GRID_H - NA_WIN_H))
            blks = []
            for krel in range(_NA_KEY_ROWS):
                kr = ks + krel
                blks.append(tcol[:, kr - r + NA_WIN_H - 1] if rs <= kr < rs + NA_WIN_H else neg_blk)
            rows.append(jnp.concatenate(blks, axis=-1))
        tabs.append(jnp.concatenate(rows, axis=-2))
    tabs.append(jnp.full((NH, RB, 3 * RB), NEG, F32))
    return jnp.stack(tabs, axis=1)


_NA_HPB = 2


def _na_kernel(q_ref, k0_ref, k1_ref, k2_ref, v0_ref, v1_ref, v2_ref, kc_ref, vc_ref, b_ref, o_ref):
    for hh in range(_NA_HPB):
        ln = slice(hh * HEAD_DIM, (hh + 1) * HEAD_DIM)
        q = (q_ref[:, ln].astype(F32) * HEAD_DIM ** -0.5).astype(BF16)
        s = [_dot_nt(q, kr[:, ln]) + b_ref[hh, 0, :, t * RB:(t + 1) * RB]
             for t, kr in enumerate((k0_ref, k1_ref, k2_ref))]
        s.append(_dot_nt(q, kc_ref[:, ln]))
        m = functools.reduce(jnp.maximum, [jnp.max(t, axis=-1, keepdims=True) for t in s])
        p = [jnp.exp(t - m) for t in s]
        l = functools.reduce(jnp.add, [jnp.sum(t, axis=-1, keepdims=True) for t in p])
        vs = (v0_ref, v1_ref, v2_ref, vc_ref)
        o = functools.reduce(jnp.add, [_dot(pt.astype(BF16), vr[:, ln]) for pt, vr in zip(p, vs)])
        o_ref[:, ln] = (o / l).astype(o_ref.dtype)


def _na_attention(proj, bias):
    hb = NH // _NA_HPB

    def qrow(h, g, b):
        return b * BLOCKS_PER_BATCH + jnp.where(g < 8, 1 + g, 0)

    def krow(t):
        return lambda h, g, b: (b * BLOCKS_PER_BATCH + 1 + _na_key_block(jnp.minimum(g, 7)) + t, hb + h)

    def vrow(t):
        return lambda h, g, b: (b * BLOCKS_PER_BATCH + 1 + _na_key_block(jnp.minimum(g, 7)) + t, 2 * hb + h)

    blk = (RB, _NA_HPB * HEAD_DIM)
    pattern = jnp.asarray(_NA_PATTERN, jnp.int32)

    def bias_map(h, g, b, pat_ref):
        return (h, pat_ref[g], 0, 0)

    def wrap(f):
        return lambda h, g, b, pat_ref: f(h, g, b)

    in_specs = [pl.BlockSpec(blk, wrap(lambda h, g, b: (qrow(h, g, b), h)))]
    in_specs += [pl.BlockSpec(blk, wrap(krow(t))) for t in range(3)]
    in_specs += [pl.BlockSpec(blk, wrap(vrow(t))) for t in range(3)]
    in_specs += [pl.BlockSpec(blk, wrap(lambda h, g, b: (b * BLOCKS_PER_BATCH, hb + h))),
                 pl.BlockSpec(blk, wrap(lambda h, g, b: (b * BLOCKS_PER_BATCH, 2 * hb + h))),
                 pl.BlockSpec((_NA_HPB, 1, RB, 3 * RB), bias_map)]

    def body(pat_ref, *refs):
        _na_kernel(*refs)

    return pl.pallas_call(
        body,
        grid_spec=pltpu.PrefetchScalarGridSpec(
            num_scalar_prefetch=1,
            grid=(hb, BLOCKS_PER_BATCH, BATCH),
            in_specs=in_specs,
            out_specs=pl.BlockSpec(blk, wrap(lambda h, g, b: (qrow(h, g, b), h))),
        ),
        out_shape=jax.ShapeDtypeStruct((M, NH * HEAD_DIM), BF16),
        compiler_params=_cparams(("arbitrary", "arbitrary", "arbitrary"), 32),
        name="na_attention",
    )(pattern, *([proj] * 9), bias)


_RET_NC = TB // RET_CHUNK
_RET_CTX_NC = CTX // RET_CHUNK


def _ret_kernel(lg_ref, q_ref, k_ref, v_ref, g_ref, cos_ref, sin_ref, o_ref, qr_ref, kr_ref, sf_ref, sb_ref):
    h = pl.program_id(1)
    lgf = lg_ref[0, h]
    lgb = lg_ref[1, h]
    c = RET_CHUNK
    qr_ref[...] = _rope(q_ref[...].astype(F32), cos_ref[...], sin_ref[...], HEAD_DIM // 4)
    kr_ref[...] = _rope(k_ref[...].astype(F32), cos_ref[...], sin_ref[...], HEAD_DIM // 4) * HEAD_DIM ** -0.5

    pos = lax.broadcasted_iota(jnp.int32, (c, 1), 0).astype(F32)
    kdec_f = jnp.exp(lgf * (c - 1.0 - pos))
    kdec_b = jnp.exp(lgb * pos)
    qdec_f = jnp.exp(lgf * (pos + 1.0))
    qdec_b = jnp.exp(lgb * (c - pos))
    gc_f = jnp.exp(lgf * c)
    gc_b = jnp.exp(lgb * c)
    rel = (lax.broadcasted_iota(jnp.int32, (c, c), 0) - lax.broadcasted_iota(jnp.int32, (c, c), 1)).astype(F32)
    dmask = (jnp.where(rel >= 0, jnp.exp(jnp.maximum(rel, 0.0) * lgf), 0.0)
             + jnp.where(rel <= 0, jnp.exp(jnp.maximum(-rel, 0.0) * lgb), 0.0))

    def kv_step(n, carry):
        rows = pl.ds(pl.multiple_of(n * c, c), c)
        kn = kr_ref[rows, :]
        vn = v_ref[rows, :]
        sf_ref[n] = _dot_tn((kn * kdec_f).astype(BF16), vn)
        sb_ref[n] = _dot_tn((kn * kdec_b).astype(BF16), vn)
        return carry

    lax.fori_loop(0, _RET_NC, kv_step, 0, unroll=3)

    def state_step(n, s, s_ref, gc):
        u = s_ref[n]
        s_ref[n] = s
        return s * gc + u

    zero = jnp.zeros((HEAD_DIM, RET_DV), F32)
    lax.fori_loop(0, _RET_NC, lambda n, s: state_step(n, s, sf_ref, gc_f), zero)
    s = lax.fori_loop(0, _RET_CTX_NC, lambda i, s: state_step(_RET_CTX_NC - 1 - i, s, sb_ref, gc_b), zero)
    lax.fori_loop(0, _RET_NC - _RET_CTX_NC, lambda i, s: state_step(_RET_NC - 1 - i, s, sb_ref, gc_b), s)

    def out_step(n, carry):
        rows = pl.ds(pl.multiple_of(n * c, c), c)
        qn = qr_ref[rows, :]
        vn = v_ref[rows, :]
        scores = _dot_nt(qn.astype(BF16), kr_ref[rows, :].astype(BF16)) * dmask
        intra = _dot(scores.astype(BF16), vn)
        qcat = jnp.concatenate([qn * qdec_f, qn * qdec_b], axis=1).astype(BF16)
        scat = jnp.concatenate([sf_ref[n], sb_ref[n]], axis=0).astype(BF16)
        o = intra + _dot(qcat, scat)
        o = o * lax.rsqrt(jnp.mean(o * o, axis=-1, keepdims=True) + EPS)
        o_ref[rows, :] = (o * _silu(g_ref[rows, :].astype(F32))).astype(o_ref.dtype)
        return carry

    lax.fori_loop(0, _RET_NC, out_step, 0, unroll=3)


def _retention(proj, log_g, cos, sin):
    seq128 = lambda blk0: pl.BlockSpec((TB, HEAD_DIM), lambda b, h, lg: (b, blk0 + h))
    seq256 = lambda blk0: pl.BlockSpec((TB, RET_DV), lambda b, h, lg: (b, blk0 + h))
    tab = pl.BlockSpec((TB, HEAD_DIM), lambda b, h, lg: (0, 0))
    return pl.pallas_call(
        _ret_kernel,
        grid_spec=pltpu.PrefetchScalarGridSpec(
            num_scalar_prefetch=1,
            grid=(BATCH, NH),
            in_specs=[seq128(3072 // HEAD_DIM), seq128(4096 // HEAD_DIM),
                      seq256(5120 // RET_DV), seq256(7168 // RET_DV), tab, tab],
            out_specs=pl.BlockSpec((TB, RET_DV), lambda b, h, lg: (b, h)),
            scratch_shapes=[pltpu.VMEM((TB, HEAD_DIM), F32), pltpu.VMEM((TB, HEAD_DIM), F32),
                            pltpu.VMEM((_RET_NC, HEAD_DIM, RET_DV), F32),
                            pltpu.VMEM((_RET_NC, HEAD_DIM, RET_DV), F32)],
        ),
        out_shape=jax.ShapeDtypeStruct((M, NH * RET_DV), BF16),
        compiler_params=_cparams(("arbitrary", "arbitrary"), 48),
        name="retention",
    )(log_g, proj, proj, proj, proj, cos, sin)


def _diff_kernel(lam_ref, q_ref, k_ref, v_ref, cq_ref, sq_ref, ck_ref, sk_ref, o_ref, kb_ref, vb_ref, *, post_scale):
    qi = pl.program_id(2)

    @pl.when(qi == 0)
    def _():
        kb_ref[...] = _rope(k_ref[...].astype(F32), ck_ref[...], sk_ref[...], DIFF_DK // 4).astype(BF16)
        vb_ref[...] = v_ref[...].astype(BF16)
        o_ref[...] = jnp.zeros_like(o_ref)

    @pl.when(qi > 0)
    def _():
        lam = lam_ref[0]
        q = _rope(q_ref[...].astype(F32), cq_ref[...], sq_ref[...], DIFF_DK // 4) * (DIFF_DK ** -0.5 * LOG2E)
        lane = lax.broadcasted_iota(jnp.int32, q.shape, 1)
        kb = kb_ref[...]
        p, inv = [], []
        for t in range(2):
            qt = jnp.where((lane >= DIFF_DK) == (t == 1), q, 0.0).astype(BF16)
            s = _dot_nt(qt, kb)
            e = jnp.exp2(s - jnp.max(s, axis=-1, keepdims=True))
            p.append(e)
            inv.append(1.0 / jnp.sum(e, axis=-1, keepdims=True))
        a = (p[0] * inv[0] - p[1] * (lam * inv[1])).astype(BF16)
        o = _dot(a, vb_ref[...])
        o = o * lax.rsqrt(jnp.mean(o * o, axis=-1, keepdims=True) + EPS)
        o_ref[...] = (o * post_scale).astype(o_ref.dtype)


def _diff_attention(proj, lam, cos, sin, post_scale):
    blk = (RB, HEAD_DIM)
    seq = (TB, HEAD_DIM)
    qmap = lambda b, h, qi, lam_ref: (b * BLOCKS_PER_BATCH + qi, h)
    return pl.pallas_call(
        functools.partial(_diff_kernel, post_scale=post_scale),
        grid_spec=pltpu.PrefetchScalarGridSpec(
            num_scalar_prefetch=1,
            grid=(BATCH, NH, BLOCKS_PER_BATCH),
            in_specs=[pl.BlockSpec(blk, qmap),
                      pl.BlockSpec(seq, lambda b, h, qi, lam_ref: (b, NH + h)),
                      pl.BlockSpec(seq, lambda b, h, qi, lam_ref: (b, 2 * NH + h)),
                      pl.BlockSpec(blk, lambda b, h, qi, lam_ref: (qi, 0)),
                      pl.BlockSpec(blk, lambda b, h, qi, lam_ref: (qi, 0)),
                      pl.BlockSpec(seq, lambda b, h, qi, lam_ref: (0, 0)),
                      pl.BlockSpec(seq, lambda b, h, qi, lam_ref: (0, 0))],
            out_specs=pl.BlockSpec(blk, qmap),
            scratch_shapes=[pltpu.VMEM(seq, BF16), pltpu.VMEM(seq, BF16)],
        ),
        out_shape=jax.ShapeDtypeStruct((M, NH * HEAD_DIM), BF16),
        compiler_params=_cparams(("arbitrary", "arbitrary", "arbitrary"), 48),
        name="diff_attention",
    )(lam, proj, proj, proj, cos, sin, cos, sin)


_HG_NC = TB // HG_CHUNK
_HG_CTX_NC = CTX // HG_CHUNK


def _chunk_cumsum(x, reverse):
    c = x.shape[0]
    row = lax.broadcasted_iota(jnp.int32, x.shape, 0)
    s = 1
    while s < c:
        if reverse:
            x = x + jnp.where(row < c - s, pltpu.roll(x, c - s, axis=0), 0.0)
        else:
            x = x + jnp.where(row >= s, pltpu.roll(x, s, axis=0), 0.0)
        s *= 2
    return x


def _hg_kernel(q_ref, ff_ref, fb_ref, v_ref, g_ref, lb_ref, o_ref,
               qs_ref, kf_ref, kb_ref, cf_ref, cb_ref, sf_ref, sb_ref, df_ref, db_ref):
    c = HG_CHUNK
    lb = lb_ref[0]
    qraw = q_ref[...].astype(F32)
    qs_ref[...] = qraw * _sigmoid(qraw)

    def gate(raw_ref, k_ref, lf_ref):
        f = lb + (1.0 - lb) * _sigmoid(raw_ref[...].astype(F32))
        k_ref[...] = 1.0 - f
        lf_ref[...] = jnp.log(f)

    gate(ff_ref, kf_ref, cf_ref)
    gate(fb_ref, kb_ref, cb_ref)

    def chunk_prep(n, carry):
        rows = pl.ds(pl.multiple_of(n * c, c), c)
        vn = v_ref[rows, :]
        for cum_ref, k_ref, u_ref, d_ref, reverse in ((cf_ref, kf_ref, sf_ref, df_ref, False),
                                                      (cb_ref, kb_ref, sb_ref, db_ref, True)):
            cum = _chunk_cumsum(cum_ref[rows, :], reverse)
            cum_ref[rows, :] = cum
            last = cum[0:1, :] if reverse else cum[c - 1:c, :]
            u_ref[n] = _dot_tn(vn, (k_ref[rows, :] * jnp.exp(last - cum)).astype(BF16))
            d_ref[n] = jnp.exp(last)
        return carry

    lax.fori_loop(0, _HG_NC, chunk_prep, 0, unroll=4)

    def state_step(n, st, s_ref, d_ref):
        u = s_ref[n]
        s_ref[n] = st
        return st * d_ref[n] + u

    zero = jnp.zeros((HEAD_DIM, HEAD_DIM), F32)
    lax.fori_loop(0, _HG_NC, lambda n, s: state_step(n, s, sf_ref, df_ref), zero)
    s = lax.fori_loop(0, _HG_CTX_NC, lambda i, s: state_step(_HG_CTX_NC - 1 - i, s, sb_ref, db_ref), zero)
    lax.fori_loop(0, _HG_NC - _HG_CTX_NC, lambda i, s: state_step(_HG_NC - 1 - i, s, sb_ref, db_ref), s)

    o_ref[0:CTX, :] = jnp.zeros((CTX, HEAD_DIM), o_ref.dtype)

    def level(q, kf, kb, cf, cb, half):
        span = 2 * half
        row = lax.broadcasted_iota(jnp.int32, (c, 1), 0)
        upper = (row % span) >= half
        ref_f = jnp.concatenate([jnp.broadcast_to(cf[b0 + half - 1:b0 + half, :], (span, HEAD_DIM))
                                 for b0 in range(0, c, span)], axis=0)
        ref_b = jnp.concatenate([jnp.broadcast_to(cb[b0 + half:b0 + half + 1, :], (span, HEAD_DIM))
                                 for b0 in range(0, c, span)], axis=0)
        ef = jnp.exp(jnp.minimum(jnp.where(upper, cf - ref_f, ref_f - cf), 0.0))
        eb = jnp.exp(jnp.minimum(jnp.where(upper, ref_b - cb, cb - ref_b), 0.0))
        qq = jnp.concatenate([jnp.where(upper, q * ef, 0.0), jnp.where(upper, 0.0, q * eb)], axis=1)
        kk = jnp.concatenate([jnp.where(upper, 0.0, kf * ef), jnp.where(upper, kb * eb, 0.0)], axis=1)
        a = _dot_nt(qq.astype(BF16), kk.astype(BF16))
        if span == c:
            return a
        ri = lax.broadcasted_iota(jnp.int32, (c, c), 0)
        ci = lax.broadcasted_iota(jnp.int32, (c, c), 1)
        return jnp.where(ri // span == ci // span, a, 0.0)

    def sub_block(q, kf, kb, lo):
        sub = HG_SUB
        row = lax.broadcasted_iota(jnp.int32, (sub, 1), 0)
        lane = lax.broadcasted_iota(jnp.int32, (sub, c), 1)
        qb, kfb, kbb = q[lo:lo + sub], kf[lo:lo + sub], kb[lo:lo + sub]
        ffb, fbb = 1.0 - kfb, 1.0 - kbb
        pf = [None] * sub
        p = jnp.where(row == sub - 1, 1.0, jnp.zeros((sub, HEAD_DIM), F32))
        pf[sub - 1] = p
        for j in range(sub - 2, -1, -1):
            p = jnp.where(row == j, 1.0, p * ffb[j + 1:j + 2])
            pf[j] = p
        att = jnp.zeros((sub, c), F32)
        p = None
        for j in range(sub):
            p = (jnp.where(row == 0, 1.0, jnp.zeros((sub, HEAD_DIM), F32)) if j == 0
                 else jnp.where(row == j, 1.0, p * fbb[j - 1:j]))
            col = jnp.sum(qb * (pf[j] * kfb[j:j + 1] + p * kbb[j:j + 1]), axis=-1, keepdims=True)
            att = jnp.where(lane == lo + j, col, att)
        return att

    def out_step(n, carry):
        rows = pl.ds(pl.multiple_of(n * c, c), c)
        q = qs_ref[rows, :]
        kf = kf_ref[rows, :]
        kb = kb_ref[rows, :]
        cf = cf_ref[rows, :]
        cb = cb_ref[rows, :]
        qcat = jnp.concatenate([q * jnp.exp(cf), q * jnp.exp(cb)], axis=1).astype(BF16)
        scat = jnp.concatenate([sf_ref[n], sb_ref[n]], axis=1).astype(BF16)
        o = _dot_nt(qcat, scat)
        att = jnp.concatenate([sub_block(q, kf, kb, lo) for lo in range(0, c, HG_SUB)], axis=0)
        half = c // 2
        while half >= HG_SUB:
            att = att + level(q, kf, kb, cf, cb, half)
            half //= 2
        o = o + _dot(att.astype(BF16), v_ref[rows, :])
        o = o * lax.rsqrt(jnp.mean(o * o, axis=-1, keepdims=True) + EPS)
        graw = g_ref[rows, :].astype(F32)
        o_ref[rows, :] = (o * (graw * _sigmoid(graw))).astype(o_ref.dtype)
        return carry

    lax.fori_loop(_HG_CTX_NC, _HG_NC, out_step, 0, unroll=4)


def _hgrn(proj, lb):
    seq = lambda col0: pl.BlockSpec((TB, HEAD_DIM), lambda b, h: (b, col0 // HEAD_DIM + h))
    full = pltpu.VMEM((TB, HEAD_DIM), F32)
    states = pltpu.VMEM((_HG_NC, HEAD_DIM, HEAD_DIM), F32)
    decays = pltpu.VMEM((_HG_NC, 1, HEAD_DIM), F32)
    return pl.pallas_call(
        _hg_kernel,
        grid=(BATCH, NH),
        in_specs=[seq(3072), seq(4096), seq(5120), seq(6144), seq(7168),
                  pl.BlockSpec((1, 1, HEAD_DIM), lambda b, h: (h, 0, 0))],
        out_specs=pl.BlockSpec((TB, HEAD_DIM), lambda b, h: (b, h)),
        out_shape=jax.ShapeDtypeStruct((M, NH * HEAD_DIM), BF16),
        scratch_shapes=[full, full, full, full, full, states, states, decays, decays],
        compiler_params=_cparams(("arbitrary", "arbitrary"), 48),
        name="hgrn",
    )(proj, proj, proj, proj, proj, lb.reshape(NH, 1, HEAD_DIM))


BULK_DMA_PRIORITY = 1


def _row_gather_start(idx_ref, k, src_hbm, dst, sem, n, alternate=False):
    for r in range(n):
        cp = pltpu.make_async_copy(src_hbm.at[pl.ds(idx_ref[0, k, r], 1)], dst.at[pl.ds(r, 1)], sem)
        cp.start(priority=r % 2 if alternate else 0)


def _row_gather_wait(src_hbm, dst, sem, n):
    pltpu.make_async_copy(src_hbm.at[pl.ds(0, n)], dst, sem).wait()


def _expert_weights_step(i, be_ref, nx_ref, w_hbms, wst, wsem, wb_refs, layer):
    e = be_ref[i]

    def fetch(ex):
        return [pltpu.make_async_copy(w.at[layer, ex], wst.at[k], wsem.at[k]) for k, w in enumerate(w_hbms)]

    @pl.when(i == 0)
    def _():
        for cp in fetch(e):
            cp.start(priority=BULK_DMA_PRIORITY)

    @pl.when(jnp.logical_or(i == 0, e != be_ref[jnp.maximum(i - 1, 0)]))
    def _():
        for cp in fetch(e):
            cp.wait()
        for k, wb in enumerate(wb_refs):
            wb[...] = wst[k].astype(BF16)
        nx = nx_ref[i]

        @pl.when(nx != e)
        def _():
            for cp in fetch(nx):
                cp.start(priority=BULK_DMA_PRIORITY)


def _moe_up_kernel(be_ref, nx_ref, nu_ref, rt0_ref, rt1_ref, x_hbm, w1_hbm, w3_hbm, o_ref,
                   xbuf, sem, wst, wsem, w1b_ref, w3b_ref, *, layer):
    i = pl.program_id(0)
    nu = nu_ref[0]

    @pl.when(jnp.logical_and(i == 0, nu > 0))
    def _():
        _row_gather_start(rt0_ref, 0, x_hbm, xbuf.at[0], sem.at[0], MOE_TM)

    @pl.when(i + 1 < nu)
    def _():
        slot = (i + 1) % 2
        _row_gather_start(rt1_ref, 0, x_hbm, xbuf.at[slot], sem.at[slot], MOE_TM)

    _expert_weights_step(i, be_ref, nx_ref, (w1_hbm, w3_hbm), wst, wsem, (w1b_ref, w3b_ref), layer)

    @pl.when(i < nu)
    def _():
        slot = i % 2
        _row_gather_wait(x_hbm, xbuf.at[slot], sem.at[slot], MOE_TM)
        for lo in range(0, MOE_TM, MOE_SUB):
            x = xbuf[slot, lo:lo + MOE_SUB, :].astype(BF16)
            a = _dot(x, w1b_ref[...])
            o_ref[lo:lo + MOE_SUB, :] = (_silu(a) * _dot(x, w3b_ref[...])).astype(o_ref.dtype)

    @pl.when(i >= nu)
    def _():
        o_ref[...] = jnp.zeros_like(o_ref)


def _moe_up(layer, blk_e, nxt_e, n_used, row_tok, h2, w1, w3):
    rt = row_tok.reshape(MOE_BLOCKS, 1, MOE_TM)
    smem_blk = lambda f: pl.BlockSpec((1, 1, MOE_TM), f, memory_space=pltpu.SMEM)
    hbm = pl.BlockSpec(memory_space=pl.ANY)
    return pl.pallas_call(
        functools.partial(_moe_up_kernel, layer=layer),
        grid_spec=pltpu.PrefetchScalarGridSpec(
            num_scalar_prefetch=3,
            grid=(MOE_BLOCKS,),
            in_specs=[smem_blk(lambda i, be, nx, nu: (i, 0, 0)),
                      smem_blk(lambda i, be, nx, nu: (jnp.minimum(i + 1, MOE_BLOCKS - 1), 0, 0)),
                      hbm, hbm, hbm],
            out_specs=pl.BlockSpec((MOE_TM, EXPERT_FF), lambda i, be, nx, nu: (i, 0)),
            scratch_shapes=[pltpu.VMEM((2, MOE_TM, D), F32), pltpu.SemaphoreType.DMA((2,)),
                            pltpu.VMEM((2, D, EXPERT_FF), F32), pltpu.SemaphoreType.DMA((2,)),
                            pltpu.VMEM((D, EXPERT_FF), BF16), pltpu.VMEM((D, EXPERT_FF), BF16)],
        ),
        out_shape=jax.ShapeDtypeStruct((MOE_ROWS, EXPERT_FF), BF16),
        compiler_params=_cparams(("arbitrary",), 52),
        name="moe_up",
    )(blk_e, nxt_e, n_used, rt, rt, h2, w1, w3)


def _moe_down_kernel(be_ref, nx_ref, nu_ref, h_ref, w2_hbm, o_ref, wst, wsem, w2b_ref, *, layer):
    i = pl.program_id(0)
    _expert_weights_step(i, be_ref, nx_ref, (w2_hbm,), wst, wsem, (w2b_ref,), layer)

    @pl.when(i < nu_ref[0])
    def _():
        o_ref[...] = _dot(h_ref[...], w2b_ref[...])

    @pl.when(i >= nu_ref[0])
    def _():
        o_ref[...] = jnp.zeros_like(o_ref)


def _moe_down(layer, blk_e, nxt_e, n_used, hmid, w2):
    return pl.pallas_call(
        functools.partial(_moe_down_kernel, layer=layer),
        grid_spec=pltpu.PrefetchScalarGridSpec(
            num_scalar_prefetch=3,
            grid=(MOE_BLOCKS,),
            in_specs=[pl.BlockSpec((MOE_TM, EXPERT_FF), lambda i, be, nx, nu: (i, 0)),
                      pl.BlockSpec(memory_space=pl.ANY)],
            out_specs=pl.BlockSpec((MOE_TM, D), lambda i, be, nx, nu: (i, 0)),
            scratch_shapes=[pltpu.VMEM((1, EXPERT_FF, D), F32), pltpu.SemaphoreType.DMA((1,)),
                            pltpu.VMEM((EXPERT_FF, D), BF16)],
        ),
        out_shape=jax.ShapeDtypeStruct((MOE_ROWS, D), F32),
        compiler_params=_cparams(("arbitrary",), 44),
        name="moe_down",
    )(blk_e, nxt_e, n_used, hmid, w2)


def _combine_kernel(p0_ref, p1_ref, y_hbm, x_ref, w_ref, g_ref, mul_ref, add_ref, *rest, last):
    out_refs, (ybuf, sem) = rest[:-2], rest[-2:]
    i = pl.program_id(0)
    n = pl.num_programs(0)

    def start(p_ref, slot):
        for k in range(TOP_K):
            _row_gather_start(p_ref, k, y_hbm, ybuf.at[slot, k], sem.at[slot], RB, alternate=True)

    @pl.when(i == 0)
    def _():
        start(p0_ref, 0)

    @pl.when(i + 1 < n)
    def _():
        start(p1_ref, (i + 1) % 2)

    slot = i % 2
    for k in range(TOP_K):
        _row_gather_wait(y_hbm, ybuf.at[slot, k], sem.at[slot], RB)
    w = w_ref[...]
    y = w[:, 0:1] * ybuf[slot, 0] + w[:, 1:2] * ybuf[slot, 1]
    t = x_ref[...] + g_ref[0] * y
    nrm = t * lax.rsqrt(jnp.mean(t * t, axis=-1, keepdims=True) + EPS) * mul_ref[0] + add_ref[0]
    if last:
        out_refs[0][0] = nrm
    else:
        out_refs[0][...] = t
        out_refs[1][...] = nrm.astype(BF16)


def _moe_combine(tok, y, pos, gates, gate_mod, mul, add, last):
    nb = M // RB
    pblk = pos.reshape(nb, RB, TOP_K).transpose(0, 2, 1)
    row = pl.BlockSpec((RB, D), lambda i: (i, 0))
    mod = pl.BlockSpec((1, 1, D), _mod_index)
    smem_blk = lambda f: pl.BlockSpec((1, TOP_K, RB), f, memory_space=pltpu.SMEM)
    if last:
        out_specs = pl.BlockSpec((1, RB, D), lambda i: (i // BLOCKS_PER_BATCH,
                                                        jnp.maximum(i % BLOCKS_PER_BATCH - 1, 0), 0))
        out_shape = jax.ShapeDtypeStruct((BATCH, SEQ, D), F32)
    else:
        out_specs = [row, row]
        out_shape = [jax.ShapeDtypeStruct((M, D), F32), jax.ShapeDtypeStruct((M, D), BF16)]
    return pl.pallas_call(
        functools.partial(_combine_kernel, last=last),
        grid=(nb,),
        in_specs=[smem_blk(lambda i: (i, 0, 0)), smem_blk(lambda i: (jnp.minimum(i + 1, nb - 1), 0, 0)),
                  pl.BlockSpec(memory_space=pl.ANY), row,
                  pl.BlockSpec((RB, TOP_K), lambda i: (i, 0)), mod, mod, mod],
        out_specs=out_specs,
        out_shape=out_shape,
        scratch_shapes=[pltpu.VMEM((2, TOP_K, RB, D), F32), pltpu.SemaphoreType.DMA((2,))],
        compiler_params=_cparams(("arbitrary",), 48),
        name="moe_combine",
    )(pblk, pblk, y, tok, gates, gate_mod, mul, add)


def _route(logits, valid):
    lg_g = logits[:, :N_GROUPS]
    p_g = jax.nn.softmax(lg_g, axis=-1)
    g_top = jnp.argmax(lg_g, axis=-1)
    w_grp = jnp.take_along_axis(p_g, g_top[:, None], axis=1)
    lg_e = logits[:, N_GROUPS:N_GROUPS + N_EXPERTS].reshape(M, N_GROUPS, EXPERTS_PER_GROUP)
    lg_in = jnp.take_along_axis(lg_e, g_top[:, None, None], axis=1)[:, 0]
    top_v, top_i = lax.top_k(lg_in, TOP_K)
    gates = w_grp * jax.nn.softmax(top_v, axis=-1)
    gates = jnp.where(valid[:, None], gates, 0.0)
    eid = g_top[:, None] * EXPERTS_PER_GROUP + top_i
    eid = jnp.where(valid[:, None], eid, N_EXPERTS).reshape(-1).astype(jnp.int32)
    n_slots = M * TOP_K
    slot = jnp.arange(n_slots, dtype=jnp.int32)
    experts = jnp.arange(N_EXPERTS + 1, dtype=jnp.int32)
    counts = jnp.sum((eid[:, None] == experts[None, :]).astype(jnp.int32), axis=0)
    padded = (counts + MOE_TM - 1) // MOE_TM * MOE_TM
    p_end = jnp.cumsum(padded)
    p_start = p_end - padded
    u_start = jnp.cumsum(counts) - counts
    shift = p_start - u_start
    s_e, order = lax.sort((eid, slot), num_keys=1, is_stable=True)
    dest = slot + jnp.sum(jnp.where(s_e[:, None] == experts[None, :], shift[None, :], 0), axis=1)
    dest = jnp.where(s_e < N_EXPERTS, dest, jnp.clip(order, 0, jnp.maximum(p_end[N_EXPERTS - 1] - 1, 0)))
    _, pos = lax.sort((order, dest), num_keys=1)
    pos = pos.reshape(M, TOP_K)
    blk_start = jnp.arange(MOE_BLOCKS, dtype=jnp.int32) * MOE_TM
    blk_e = jnp.sum((p_end[None, :N_EXPERTS] <= blk_start[:, None]).astype(jnp.int32), axis=1)
    blk_e = jnp.minimum(blk_e, N_EXPERTS - 1)
    later = jnp.min(jnp.where(blk_e[None, :] > blk_e[:, None], blk_e[None, :], N_EXPERTS), axis=1)
    nxt_e = jnp.where(later < N_EXPERTS, later, blk_e)
    onehot_be = blk_e[:, None] == experts[None, :N_EXPERTS]
    blk_shift = jnp.sum(jnp.where(onehot_be, shift[None, :N_EXPERTS], 0), axis=1)
    blk_lim = jnp.sum(jnp.where(onehot_be, (p_start + counts)[None, :N_EXPERTS], 0), axis=1)
    rows = jnp.arange(MOE_ROWS, dtype=jnp.int32).reshape(MOE_BLOCKS, MOE_TM)
    src = jnp.clip(rows - blk_shift[:, None], 0, n_slots - 1)
    row_tok = jnp.where(rows < blk_lim[:, None], jnp.take(order // TOP_K, src), 0)
    n_used = (p_end[N_EXPERTS - 1] // MOE_TM).astype(jnp.int32).reshape(1)
    return gates, row_tok, pos, blk_e, nxt_e, n_used


def _router_tables(wg, bg, we, be):
    wr = jnp.zeros((D, ROUTER_PAD), F32).at[:, :N_GROUPS].set(wg).at[:, N_GROUPS:N_GROUPS + N_EXPERTS].set(we)
    w_hi = wr.astype(BF16)
    w_lo = (wr - w_hi.astype(F32)).astype(BF16)
    r_bias = jnp.zeros((1, ROUTER_PAD), F32).at[0, :N_GROUPS].set(bg).at[0, N_GROUPS:N_GROUPS + N_EXPERTS].set(be)
    return w_hi, w_lo, r_bias


def _moe(layer, tok, h2, logits, gate_mod, w1, w3, w2, valid, next_mul, next_add, last):
    gates, row_tok, pos, blk_e, nxt_e, n_used = _route(logits, valid)
    hmid = _moe_up(layer, blk_e, nxt_e, n_used, row_tok, h2, w1, w3)
    y = _moe_down(layer, blk_e, nxt_e, n_used, hmid, w2)
    return _moe_combine(tok, y, pos, gates, gate_mod, next_mul, next_add, last)


def _mod9(t):
    return t[:BATCH + 1].reshape(BATCH + 1, 1, D)


def kernel(x, c, ctx, c_ctx, ada_w, ada_b, norm1_w, norm2_w, w_in_even, w_out_even, na_rpb, ret_decay,
           w_in_odd, w_out_odd, diff_lambda, hg_lb_logits, router_g_w, router_g_b, router_e_w, router_e_b,
           moe_w1, moe_w3, moe_w2, norm_f_w):
    depth = ada_w.shape[0]
    c16 =jnp.concatenate([c, c_ctx[None], jnp.zeros((16 - BATCH - 1, D), F32)], axis=0)
    mod = _ada_mod(c16, ada_w, ada_b)
    lb_all = jnp.cumsum(jax.nn.softmax(hg_lb_logits.astype(F32), axis=0), axis=0)
    lb_all = lb_all - lb_all[0]
    is_latent = (jnp.arange(M) % TB) >= CTX
    all_rows = jnp.ones((M,), bool)

    mods = [[_mod9(t) for t in jnp.split(mod[l], 6, axis=-1)] for l in range(depth)]
    tok, h = _first_norm(x, ctx, norm1_w[0] * (1.0 + mods[0][1]), mods[0][0])
    for l in range(depth):
        with_ctx = l < depth - 1
        sh1, s1, g1, sh2, s2, g2 = mods[l]
        if with_ctx:
            next_mul, next_add = norm1_w[l + 1] * (1.0 + mods[l + 1][1]), mods[l + 1][0]
        else:
            next_mul = jnp.broadcast_to(norm_f_w.reshape(1, 1, D), (BATCH + 1, 1, D))
            next_add = jnp.zeros((BATCH + 1, 1, D), F32)
        j = l // 2
        if l % 2 == 0:
            proj = _matmul(h, w_in_even[j].astype(BF16), BF16)
            m1 = _na_attention(proj, _na_bias_tables(na_rpb[j]))
            cos, sin = _rope_tables(HEAD_DIM, 1)
            m2 = _retention(proj, -jnp.exp(ret_decay[j].astype(F32)), cos, sin)
            w_out = w_out_even[j].astype(BF16)
        else:
            proj = _matmul(h, w_in_odd[j].astype(BF16), BF16)
            lp = diff_lambda[j].astype(F32)
            lam_init = 0.8 - 0.6 * math.exp(-0.3 * l)
            lam = jnp.exp(jnp.sum(lp[0] * lp[1])) - jnp.exp(jnp.sum(lp[2] * lp[3])) + lam_init
            cos, sin = _rope_tables(DIFF_DK, 2)
            m1 = _diff_attention(proj, lam.reshape(1), cos, sin, 1.0 - lam_init)
            m2 = _hgrn(proj, lb_all[l])
            w_out = w_out_odd[j].astype(BF16)
        tok, h2, logits = _out_proj(m1, m2, w_out[:NH * HEAD_DIM], w_out[NH * HEAD_DIM:], tok, g1,
                                    norm2_w[l] * (1.0 + s2), sh2,
                                    *_router_tables(router_g_w[l], router_g_b[l], router_e_w[l], router_e_b[l]))
        res = _moe(l, tok, h2, logits, g2, moe_w1, moe_w3, moe_w2, all_rows if with_ctx else is_latent,
                   next_mul, next_add, not with_ctx)
        if with_ctx:
            tok, h = res
    return res
```
